```python
import math
import jax, jax.numpy as jnp
from jax import lax
import numpy as np

D_MODEL = 4096
BATCH = 1
SEQ = 8192
DEPTH = 1

N_META = 16
BLOCK = 128
META_CHUNK = BLOCK
N_PAD = META_CHUNK - N_META

N_HEADS = 32
N_KV_HEADS = 4
HEAD_DIM = 64
WINDOW = 128
ATTN_WIDTH = N_HEADS * HEAD_DIM
KV_WIDTH = N_KV_HEADS * HEAD_DIM

SSM_WIDTH = 1024
SSM_GROUP = 16
N_SSM_GROUPS = SSM_WIDTH // SSM_GROUP
SSM_STATE = 64
DT_MIN = 1e-3
DT_MAX = 1e-1

N_EXPERTS = 64
TOP_K = 8
N_EXPERT_GROUPS = 8
TOPK_GROUPS = 4
EXPERT_DIM = 512
SHARED_DIM = 512
ROUTED_SCALE = 2.5
ROUTER_BIAS_SCALE = 0.01
MOE_BLOCK = 128

IN_WIDTH = ATTN_WIDTH + 2 * KV_WIDTH + SSM_WIDTH + 2 * D_MODEL
SPLIT_POINTS = (ATTN_WIDTH,
                ATTN_WIDTH + KV_WIDTH,
                ATTN_WIDTH + 2 * KV_WIDTH,
                ATTN_WIDTH + 2 * KV_WIDTH + SSM_WIDTH,
                ATTN_WIDTH + 2 * KV_WIDTH + SSM_WIDTH + D_MODEL)

DEEPNORM_ALPHA = (2 * DEPTH) ** 0.25
DEEPNORM_BETA = (8 * DEPTH) ** -0.25
LN_EPS = 1e-5
NEG_INF = -1e30

kernel_name = 'hybrid_swa_s5_moe_deepnorm_meta'


def layer_norm(x, g, b):
    xf = x.astype(jnp.float32)
    mu = xf.mean(-1, keepdims=True)
    var = jnp.square(xf - mu).mean(-1, keepdims=True)
    y = (xf - mu) * lax.rsqrt(var + LN_EPS) * g.astype(jnp.float32) + b.astype(jnp.float32)
    return y.astype(x.dtype)


def sliding_window_attention(q, k, v, sinks):
    B, L = q.shape[:2]
    nb = L // BLOCK
    g = N_HEADS // N_KV_HEADS
    qb = q.reshape(B, nb, BLOCK, N_KV_HEADS, g, HEAD_DIM)
    kb = k.reshape(B, nb, BLOCK, N_KV_HEADS, HEAD_DIM)
    vb = v.reshape(B, nb, BLOCK, N_KV_HEADS, HEAD_DIM)

    def with_prev(t):
        prev = jnp.concatenate([jnp.zeros_like(t[:, :1]), t[:, :-1]], axis=1)
        return jnp.concatenate([prev, t], axis=2)

    kband, vband = with_prev(kb), with_prev(vb)
    k_meta = k[:, N_PAD:META_CHUNK]
    v_meta = v[:, N_PAD:META_CHUNK]
    scale = HEAD_DIM ** -0.5
    s_band = jnp.einsum('bnqkgd,bnckd->bnkgqc', qb, kband, preferred_element_type=jnp.float32) * scale
    s_meta = jnp.einsum('bnqkgd,bmkd->bnkgqm', qb, k_meta, preferred_element_type=jnp.float32) * scale

    qpos = jnp.arange(nb)[:, None] * BLOCK + jnp.arange(BLOCK)[None, :]
    kpos = (jnp.arange(nb)[:, None] - 1) * BLOCK + jnp.arange(2 * BLOCK)[None, :]
    dist = qpos[:, :, None] - kpos[:, None, :]
    band_ok = (dist >= 0) & (dist < WINDOW) & (kpos[:, None, :] >= META_CHUNK)
    mpos = N_PAD + jnp.arange(N_META)
    meta_ok = mpos[None, None, :] <= qpos[:, :, None]
    s_band = jnp.where(band_ok[None, :, None, None], s_band, NEG_INF)
    s_meta = jnp.where(meta_ok[None, :, None, None], s_meta, NEG_INF)
    sink = jnp.broadcast_to(
        sinks.astype(jnp.float32).reshape(N_KV_HEADS, g)[None, None, :, :, None, None],
        s_band.shape[:-1] + (1,))
    p = jax.nn.softmax(jnp.concatenate([s_band, s_meta, sink], axis=-1), axis=-1)
    p_band = p[..., :2 * BLOCK].astype(v.dtype)
    p_meta = p[..., 2 * BLOCK:2 * BLOCK + N_META].astype(v.dtype)
    out = (jnp.einsum('bnkgqc,bnckd->bnqkgd', p_band, vband)
           + jnp.einsum('bnkgqm,bmkd->bnqkgd', p_meta, v_meta))
    return out.reshape(B, L, ATTN_WIDTH)


def _complex_scan_combine(c1, c2):
    a1r, a1i, b1r, b1i = c1
    a2r, a2i, b2r, b2i = c2
    ar = a2r * a1r - a2i * a1i
    ai = a2r * a1i + a2i * a1r
    br = a2r * b1r - a2i * b1i + b2r
    bi = a2r * b1i + a2i * b1r + b2i
    return (ar, ai, br, bi)


def s5_ssm(u, valid, lam_re, lam_im, log_dt, b_re, b_im, c_re, c_im, d_skip):
    B, L = u.shape[:2]
    uf = jnp.where(valid[None, :, None], u.astype(jnp.float32), 0.0)
    uf = uf.reshape(B, L, N_SSM_GROUPS, SSM_GROUP)
    dt = jnp.exp(log_dt.astype(jnp.float32))[:, None]
    lr = lam_re.astype(jnp.float32)
    li = lam_im.astype(jnp.float32)
    mag = jnp.exp(dt * lr)
    abar_re = mag * jnp.cos(dt * li)
    abar_im = mag * jnp.sin(dt * li)
    den = lr * lr + li * li
    num_re = abar_re - 1.0
    coef_re = (num_re * lr + abar_im * li) / den
    coef_im = (abar_im * lr - num_re * li) / den
    br, bi = b_re.astype(jnp.float32), b_im.astype(jnp.float32)
    bbar_re = coef_re[..., None] * br - coef_im[..., None] * bi
    bbar_im = coef_re[..., None] * bi + coef_im[..., None] * br
    bu_re = jnp.einsum('blgh,gph->blgp', uf, bbar_re)
    bu_im = jnp.einsum('blgh,gph->blgp', uf, bbar_im)
    a_re = jnp.broadcast_to(abar_re[None, None], bu_re.shape)
    a_im = jnp.broadcast_to(abar_im[None, None], bu_re.shape)
    _, _, x_re, x_im = lax.associative_scan(_complex_scan_combine, (a_re, a_im, bu_re, bu_im), axis=1)
    y = (jnp.einsum('blgp,ghp->blgh', x_re, c_re.astype(jnp.float32))
         - jnp.einsum('blgp,ghp->blgh', x_im, c_im.astype(jnp.float32))
         + d_skip.astype(jnp.float32).reshape(N_SSM_GROUPS, SSM_GROUP) * uf)
    return y.reshape(B, L, SSM_WIDTH).astype(u.dtype)


def route(h, w_router, router_bias):
    T = h.shape[0]
    scores = jax.nn.sigmoid(jnp.einsum('td,de->te', h, w_router, preferred_element_type=jnp.float32))
    sel = scores + router_bias.astype(jnp.float32)
    grp = sel.reshape(T, N_EXPERT_GROUPS, N_EXPERTS // N_EXPERT_GROUPS)
    grp_score = lax.top_k(grp, 2)[0].sum(-1)
    _, top_groups = lax.top_k(grp_score, TOPK_GROUPS)
    group_mask = jax.nn.one_hot(top_groups, N_EXPERT_GROUPS).sum(-2) > 0
    expert_mask = jnp.repeat(group_mask, N_EXPERTS // N_EXPERT_GROUPS, axis=-1)
    _, idx = lax.top_k(jnp.where(expert_mask, sel, NEG_INF), TOP_K)
    w = jnp.take_along_axis(scores, idx, axis=-1)
    w = w / w.sum(-1, keepdims=True) * ROUTED_SCALE
    return idx, w


def routed_experts(h, idx, wts, w1, w3, w2):
    T, D = h.shape
    n_assign = T * TOP_K
    n_blocks = -(-n_assign // MOE_BLOCK) + N_EXPERTS
    n_rows = n_blocks * MOE_BLOCK
    e_flat = idx.reshape(-1)
    tok_flat = jnp.repeat(jnp.arange(T, dtype=jnp.int32), TOP_K)
    w_flat = wts.reshape(-1)
    onehot = jax.nn.one_hot(e_flat, N_EXPERTS, dtype=jnp.int32)
    rank = jnp.take_along_axis(jnp.cumsum(onehot, axis=0), e_flat[:, None], axis=1)[:, 0] - 1
    counts = onehot.sum(0)
    padded = (counts + MOE_BLOCK - 1) // MOE_BLOCK * MOE_BLOCK
    ends = jnp.cumsum(padded)
    starts = ends - padded
    dest = starts[e_flat] + rank
    row_tok = jnp.zeros((n_rows,), jnp.int32).at[dest].set(tok_flat)
    row_w = jnp.zeros((n_rows,), w_flat.dtype).at[dest].set(w_flat)
    block_expert = jnp.minimum(
        jnp.searchsorted(ends, jnp.arange(n_blocks, dtype=ends.dtype) * MOE_BLOCK, side='right'),
        N_EXPERTS - 1)

    def expert_block(args):
        toks, e = args
        xb = h[toks]
        return (jax.nn.silu(xb @ w1[e]) * (xb @ w3[e])) @ w2[e]

    yb = lax.map(expert_block, (row_tok.reshape(n_blocks, MOE_BLOCK), block_expert)).reshape(n_rows, D)
    return jax.ops.segment_sum(yb * row_w[:, None].astype(yb.dtype), row_tok, num_segments=T)


def setup_inputs(seed: int = 0) -> dict:
    key = jax.random.key(seed)
    ks = iter(jax.random.split(key, 32))

    def nrm(shape, scale):
        return jax.random.normal(next(ks), shape, jnp.float32) * scale

    n_idx = jnp.arange(SSM_STATE, dtype=jnp.float32)
    return {
        'x': nrm((BATCH, SEQ, D_MODEL), 1.0),
        'meta_tokens': nrm((N_META, D_MODEL), 1.0),
        'ln0_g': 1.0 + nrm((D_MODEL,), 0.01),
        'ln0_b': nrm((D_MODEL,), 0.01),
        'w_in': nrm((DEPTH, D_MODEL, IN_WIDTH), D_MODEL ** -0.5),
        'attn_sinks': nrm((DEPTH, N_HEADS), 0.5),
        'w_attn_o': nrm((DEPTH, ATTN_WIDTH, D_MODEL), ATTN_WIDTH ** -0.5),
        'ssm_lam_re': -0.5 + nrm((DEPTH, N_SSM_GROUPS, SSM_STATE), 0.01),
        'ssm_lam_im': jnp.pi * n_idx + nrm((DEPTH, N_SSM_GROUPS, SSM_STATE), 0.01),
        'ssm_log_dt': jax.random.uniform(next(ks), (DEPTH, N_SSM_GROUPS), jnp.float32,
                                         minval=math.log(DT_MIN), maxval=math.log(DT_MAX)),
        'ssm_b_re': nrm((DEPTH, N_SSM_GROUPS, SSM_STATE, SSM_GROUP), (2 * SSM_GROUP) ** -0.5),
        'ssm_b_im': nrm((DEPTH, N_SSM_GROUPS, SSM_STATE, SSM_GROUP), (2 * SSM_GROUP) ** -0.5),
        'ssm_c_re': nrm((DEPTH, N_SSM_GROUPS, SSM_GROUP, SSM_STATE), (2 * SSM_STATE) ** -0.5),
        'ssm_c_im': nrm((DEPTH, N_SSM_GROUPS, SSM_GROUP, SSM_STATE), (2 * SSM_STATE) ** -0.5),
        'ssm_d': nrm((DEPTH, SSM_WIDTH), 1.0),
        'w_glu': nrm((DEPTH, SSM_WIDTH, SSM_WIDTH), SSM_WIDTH ** -0.5),
        'w_ssm_o': nrm((DEPTH, SSM_WIDTH, D_MODEL), SSM_WIDTH ** -0.5),
        'w_out': nrm((DEPTH, D_MODEL, D_MODEL), D_MODEL ** -0.5 * DEEPNORM_BETA),
        'ln1_g': 1.0 + nrm((DEPTH, D_MODEL), 0.01),
        'ln1_b': nrm((DEPTH, D_MODEL), 0.01),
        'w_router': nrm((DEPTH, D_MODEL, N_EXPERTS), D_MODEL ** -0.5),
        'router_bias': nrm((DEPTH, N_EXPERTS), ROUTER_BIAS_SCALE),
        'expert_w1': nrm((DEPTH, N_EXPERTS, D_MODEL, EXPERT_DIM), D_MODEL ** -0.5),
        'expert_w3': nrm((DEPTH, N_EXPERTS, D_MODEL, EXPERT_DIM), D_MODEL ** -0.5),
        'expert_w2': nrm((DEPTH, N_EXPERTS, EXPERT_DIM, D_MODEL), EXPERT_DIM ** -0.5 * DEEPNORM_BETA),
        'shared_w1': nrm((DEPTH, D_MODEL, SHARED_DIM), D_MODEL ** -0.5),
        'shared_w3': nrm((DEPTH, D_MODEL, SHARED_DIM), D_MODEL ** -0.5),
        'shared_w2': nrm((DEPTH, SHARED_DIM, D_MODEL), SHARED_DIM ** -0.5 * DEEPNORM_BETA),
        'ln2_g': 1.0 + nrm((DEPTH, D_MODEL), 0.01),
        'ln2_b': nrm((DEPTH, D_MODEL), 0.01),
    }


def reference(x, meta_tokens, ln0_g, ln0_b, w_in, attn_sinks, w_attn_o,
              ssm_lam_re, ssm_lam_im, ssm_log_dt, ssm_b_re, ssm_b_im, ssm_c_re, ssm_c_im, ssm_d,
              w_glu, w_ssm_o, w_out, ln1_g, ln1_b, w_router, router_bias,
              expert_w1, expert_w3, expert_w2, shared_w1, shared_w3, shared_w2, ln2_g, ln2_b):
    B = x.shape[0]
    pad = jnp.zeros((B, N_PAD, D_MODEL), x.dtype)
    meta = jnp.broadcast_to(meta_tokens.astype(x.dtype)[None], (B, N_META, D_MODEL))
    h = layer_norm(jnp.concatenate([pad, meta, x], axis=1), ln0_g, ln0_b)
    L = h.shape[1]
    valid = jnp.arange(L) >= N_PAD

    for l in range(DEPTH):
        z = h @ w_in[l]
        q, k, v, u, gate_a, gate_s = jnp.split(z, SPLIT_POINTS, axis=-1)
        attn = sliding_window_attention(q.reshape(B, L, N_HEADS, HEAD_DIM),
                                        k.reshape(B, L, N_KV_HEADS, HEAD_DIM),
                                        v.reshape(B, L, N_KV_HEADS, HEAD_DIM),
                                        attn_sinks[l])
        attn_d = attn @ w_attn_o[l]
        y = s5_ssm(u, valid, ssm_lam_re[l], ssm_lam_im[l], ssm_log_dt[l],
                   ssm_b_re[l], ssm_b_im[l], ssm_c_re[l], ssm_c_im[l], ssm_d[l])
        y = jax.nn.gelu(y, approximate=False)
        y = y * jax.nn.sigmoid(y @ w_glu[l])
        ssm_dd = y @ w_ssm_o[l]
        mix = jax.nn.sigmoid(gate_a) * attn_d + jax.nn.sigmoid(gate_s) * ssm_dd
        h = layer_norm(DEEPNORM_ALPHA * h + mix @ w_out[l], ln1_g[l], ln1_b[l])

        ht = h.reshape(B * L, D_MODEL)
        idx, wts = route(ht, w_router[l], router_bias[l])
        shared = (jax.nn.silu(ht @ shared_w1[l]) * (ht @ shared_w3[l])) @ shared_w2[l]
        routed = routed_experts(ht, idx, wts, expert_w1[l], expert_w3[l], expert_w2[l])
        moe = (shared + routed).reshape(B, L, D_MODEL)
        h = layer_norm(DEEPNORM_ALPHA * h + moe, ln2_g[l], ln2_b[l])

    return h[:, META_CHUNK:]
```

```python
import functools
import math

import jax
import jax.numpy as jnp
import numpy as np
from jax import lax
from jax.experimental import pallas as pl
from jax.experimental.pallas import tpu as pltpu

F32 = jnp.float32
BF16 = jnp.bfloat16

D_MODEL = 4096
N_META = 16
BLOCK = 128
N_PAD = BLOCK - N_META

N_HEADS = 32
N_KV_HEADS = 4
HEAD_DIM = 64
HEADS_PER_KV = N_HEADS // N_KV_HEADS
ATTN_WIDTH = N_HEADS * HEAD_DIM
KV_WIDTH = N_KV_HEADS * HEAD_DIM

SSM_WIDTH = 1024
SSM_GROUP = 16
N_SSM_GROUPS = SSM_WIDTH // SSM_GROUP
SSM_STATE = 64
SSM_CHUNK = 16
SSM_PAIRS = N_SSM_GROUPS // 2

N_EXPERTS = 64
TOP_K = 8
N_EXPERT_GROUPS = 8
EXPERTS_PER_GROUP = N_EXPERTS // N_EXPERT_GROUPS
TOPK_GROUPS = 4
EXPERT_DIM = 512
SHARED_DIM = 512
ROUTED_SCALE = 2.5
ROW_BLOCK = 256

IN_WIDTH = ATTN_WIDTH + 2 * KV_WIDTH + SSM_WIDTH + 2 * D_MODEL
COL_KV = ATTN_WIDTH
COL_U = ATTN_WIDTH + 2 * KV_WIDTH
COL_GA = COL_U + SSM_WIDTH
COL_GS = COL_GA + D_MODEL

DEEPNORM_ALPHA = 2.0 ** 0.25
LN_EPS = 1e-5
NEG_INF = -1e30

MIB = 1024 * 1024


def _params(semantics, vmem_mib):
    return pltpu.CompilerParams(dimension_semantics=semantics, vmem_limit_bytes=vmem_mib * MIB)


def _tile(n, pref):
    t = min(n, pref)
    while n % t:
        t -= 128
    return t


def _ln_rows(x, g, b):
    mu = jnp.mean(x, axis=-1, keepdims=True)
    xc = x - mu
    var = jnp.mean(xc * xc, axis=-1, keepdims=True)
    return xc * lax.rsqrt(var + LN_EPS) * g + b


def _ln_kernel(x_ref, g_ref, b_ref, o_ref, ob_ref):
    y = _ln_rows(x_ref[...], g_ref[...], b_ref[...])
    o_ref[...] = y
    ob_ref[...] = y.astype(BF16)


def _layer_norm(x, g, b):
    m, d = x.shape
    tm = _tile(m, 256)
    return pl.pallas_call(
        _ln_kernel,
        grid=(m // tm,),
        in_specs=[pl.BlockSpec((tm, d), lambda i: (i, 0)),
                  pl.BlockSpec((1, d), lambda i: (0, 0)),
                  pl.BlockSpec((1, d), lambda i: (0, 0))],
        out_specs=[pl.BlockSpec((tm, d), lambda i: (i, 0)),
                   pl.BlockSpec((tm, d), lambda i: (i, 0))],
        out_shape=[jax.ShapeDtypeStruct((m, d), F32), jax.ShapeDtypeStruct((m, d), BF16)],
        compiler_params=_params(("parallel",), 40),
        name="layer_norm",
    )(x, g.reshape(1, d), b.reshape(1, d))


def _proj_kernel(a_ref, w_ref, o_ref):
    o_ref[...] = jnp.dot(a_ref[...], w_ref[...].astype(BF16),
                         preferred_element_type=F32).astype(o_ref.dtype)


def _project(a, w, col0, n_cols, tm_pref=1024, tn=512):
    m, k = a.shape
    tm = _tile(m, tm_pref)
    off = col0 // tn
    return pl.pallas_call(
        _proj_kernel,
        grid=(m // tm, n_cols // tn),
        in_specs=[pl.BlockSpec((tm, k), lambda i, j: (i, 0)),
                  pl.BlockSpec((k, tn), lambda i, j: (0, j + off))],
        out_specs=pl.BlockSpec((tm, tn), lambda i, j: (i, j)),
        out_shape=jax.ShapeDtypeStruct((m, n_cols), BF16),
        compiler_params=_params(("parallel", "arbitrary"), 56),
        name="in_proj",
    )(a, w)


def _attn_kernel(sink_ref, q_ref, kvc_ref, kvp_ref, kvm_ref, o_ref):
    first = pl.program_id(0) == 0
    kv_all = jnp.concatenate([kvp_ref[...], kvc_ref[...], kvm_ref[N_PAD:BLOCK, :]], axis=0)
    n_keys = 2 * BLOCK + N_META
    rows = HEADS_PER_KV * BLOCK

    lane = lax.broadcasted_iota(jnp.int32, (BLOCK, BLOCK), 1)
    lo_q = lane < HEAD_DIM
    lane_kv = lax.broadcasted_iota(jnp.int32, (n_keys, BLOCK), 1)
    lo_kv = lane_kv < HEAD_DIM

    qi = lax.broadcasted_iota(jnp.int32, (rows, n_keys), 0) & (BLOCK - 1)
    kj = lax.broadcasted_iota(jnp.int32, (rows, n_keys), 1)
    in_window = (kj > qi) & (kj <= qi + BLOCK) & ((kj >= BLOCK) | jnp.logical_not(first))
    visible = in_window | (kj >= 2 * BLOCK)
    local_head = lax.broadcasted_iota(jnp.int32, (rows, 1), 0) // BLOCK

    def dup_head(tile, odd):
        t = tile.astype(F32)
        r = pltpu.roll(t, HEAD_DIM, 1)
        keep = jnp.logical_not(lo_kv) if odd else lo_kv
        return jnp.where(keep, t, r).astype(BF16)

    for kh in range(N_KV_HEADS):
        t0 = (kh // 2) * BLOCK
        k2 = dup_head(kv_all[:, t0:t0 + BLOCK], kh % 2)
        v2 = dup_head(kv_all[:, KV_WIDTH + t0:KV_WIDTH + t0 + BLOCK], kh % 2)
        pieces = []
        for p in range(HEADS_PER_KV // 2):
            c0 = kh * HEADS_PER_KV * HEAD_DIM + p * BLOCK
            qp = q_ref[:, c0:c0 + BLOCK]
            zero = jnp.zeros_like(qp)
            pieces.append(jnp.where(lo_q, qp, zero))
            pieces.append(jnp.where(lo_q, zero, qp))
        qs = jnp.concatenate(pieces, axis=0)
        s = lax.dot_general(qs, k2, (((1,), (1,)), ((), ())), preferred_element_type=F32)
        s = jnp.where(visible, s * (HEAD_DIM ** -0.5), NEG_INF)
        sink = jnp.zeros((rows, 1), F32)
        for t in range(HEADS_PER_KV):
            sink = jnp.where(local_head == t, sink_ref[kh * HEADS_PER_KV + t], sink)
        m = jnp.maximum(jnp.max(s, axis=-1, keepdims=True), sink)
        e = jnp.exp(s - m)
        den = jnp.sum(e, axis=-1, keepdims=True) + jnp.exp(sink - m)
        o2 = jnp.dot(e.astype(BF16), v2, preferred_element_type=F32) / den
        for p in range(HEADS_PER_KV // 2):
            c0 = kh * HEADS_PER_KV * HEAD_DIM + p * BLOCK
            r0 = 2 * p * BLOCK
            o_ref[:, c0:c0 + BLOCK] = jnp.where(
                lo_q, o2[r0:r0 + BLOCK], o2[r0 + BLOCK:r0 + 2 * BLOCK]).astype(BF16)


def _attention(z, z_lead, sinks):
    t = z.shape[0]
    kv_blk = COL_KV // (2 * KV_WIDTH)
    return pl.pallas_call(
        _attn_kernel,
        grid=(t // BLOCK,),
        in_specs=[pl.BlockSpec(memory_space=pltpu.SMEM),
                  pl.BlockSpec((BLOCK, ATTN_WIDTH), lambda n: (n, 0)),
                  pl.BlockSpec((BLOCK, 2 * KV_WIDTH), lambda n: (n, kv_blk)),
                  pl.BlockSpec((BLOCK, 2 * KV_WIDTH), lambda n: (jnp.maximum(n - 1, 0), kv_blk)),
                  pl.BlockSpec((BLOCK, 2 * KV_WIDTH), lambda n: (0, 0))],
        out_specs=pl.BlockSpec((BLOCK, ATTN_WIDTH), lambda n: (n, 0)),
        out_shape=jax.ShapeDtypeStruct((t, ATTN_WIDTH), BF16),
        compiler_params=_params(("parallel",), 40),
        name="swa_attention",
    )(sinks.astype(F32), z, z, z, z_lead)


def _cmul(ar, ai, br, bi):
    return ar * br - ai * bi, ar * bi + ai * br


def _ssm_tables(lam_re, lam_im, log_dt, b_re, b_im, c_re, c_im, d_skip, n_chunks):
    hi = lax.Precision.HIGHEST
    g, p, h, c = N_SSM_GROUPS, SSM_STATE, SSM_GROUP, SSM_CHUNK
    dt = jnp.exp(log_dt)[:, None]
    mag = jnp.exp(dt * lam_re)
    are = mag * jnp.cos(dt * lam_im)
    aim = mag * jnp.sin(dt * lam_im)
    den = lam_re * lam_re + lam_im * lam_im
    num_re = are - 1.0
    coef_re = (num_re * lam_re + aim * lam_im) / den
    coef_im = (aim * lam_re - num_re * lam_im) / den
    bb_re = coef_re[..., None] * b_re - coef_im[..., None] * b_im
    bb_im = coef_re[..., None] * b_im + coef_im[..., None] * b_re

    pr, pi = [jnp.ones_like(are)], [jnp.zeros_like(are)]
    for _ in range(c):
        r, i = _cmul(pr[-1], pi[-1], are, aim)
        pr.append(r)
        pi.append(i)
    pr, pi = jnp.stack(pr), jnp.stack(pi)

    m_re = pr[:c, :, :, None] * bb_re[None] - pi[:c, :, :, None] * bb_im[None]
    m_im = pr[:c, :, :, None] * bb_im[None] + pi[:c, :, :, None] * bb_re[None]
    lag = (jnp.einsum('ghp,kgpj->kghj', c_re, m_re, precision=hi)
           - jnp.einsum('ghp,kgpj->kghj', c_im, m_im, precision=hi))
    si = np.arange(c)
    delta = si[None, :] - si[:, None]
    toep = jnp.where((delta >= 0)[:, :, None, None, None], lag[np.maximum(delta, 0)], 0.0)
    toep = toep.transpose(2, 0, 4, 1, 3).reshape(g, c * h, c * h)

    wout_re = m_re[::-1].transpose(1, 0, 3, 2).reshape(g, c * h, p)
    wout_im = m_im[::-1].transpose(1, 0, 3, 2).reshape(g, c * h, p)
    wout = jnp.zeros((SSM_PAIRS, 2 * c * h, 4 * p), F32)
    wout = wout.at[:, :c * h, 0:p].set(wout_re[0::2]).at[:, :c * h, 2 * p:3 * p].set(wout_im[0::2])
    wout = wout.at[:, c * h:, p:2 * p].set(wout_re[1::2]).at[:, c * h:, 3 * p:].set(wout_im[1::2])

    cin_re = c_re[None] * pr[1:, :, None, :] - c_im[None] * pi[1:, :, None, :]
    cin_im = -(c_re[None] * pi[1:, :, None, :] + c_im[None] * pr[1:, :, None, :])
    cin_re = cin_re.transpose(1, 3, 0, 2).reshape(g, p, c * h)
    cin_im = cin_im.transpose(1, 3, 0, 2).reshape(g, p, c * h)
    wcin = jnp.zeros((SSM_PAIRS, 4 * p, 2 * c * h), F32)
    wcin = wcin.at[:, 0:p, :c * h].set(cin_re[0::2]).at[:, p:2 * p, c * h:].set(cin_re[1::2])
    wcin = wcin.at[:, 2 * p:3 * p, :c * h].set(cin_im[0::2]).at[:, 3 * p:, c * h:].set(cin_im[1::2])

    n_steps = max(1, math.ceil(math.log2(n_chunks)))
    qr, qi = [pr[c]], [pi[c]]
    for _ in range(n_steps - 1):
        r, i = _cmul(qr[-1], qi[-1], qr[-1], qi[-1])
        qr.append(r)
        qi.append(i)
    apr = jnp.stack(qr, axis=1).reshape(SSM_PAIRS, 2, n_steps, p).transpose(0, 2, 1, 3)
    api = jnp.stack(qi, axis=1).reshape(SSM_PAIRS, 2, n_steps, p).transpose(0, 2, 1, 3)
    apr = apr.reshape(SSM_PAIRS, n_steps, 2 * p)
    api = api.reshape(SSM_PAIRS, n_steps, 2 * p)
    dsk = jnp.tile(d_skip.reshape(SSM_PAIRS, 2, 1, h), (1, 1, c, 1)).reshape(SSM_PAIRS, 1, 2 * c * h)
    return toep.astype(BF16), wout.astype(BF16), wcin.astype(BF16), apr, api, dsk, n_steps


def _ssm_kernel(u_ref, toep_ref, wout_ref, wcin_ref, apr_ref, api_ref, d_ref, y_ref, *, n_steps):
    u = u_ref[0]
    nc = u.shape[0]
    half = SSM_CHUNK * SSM_GROUP
    st = jnp.dot(u, wout_ref[0], preferred_element_type=F32)
    xr, xi = st[:, :2 * SSM_STATE], st[:, 2 * SSM_STATE:]
    row = lax.broadcasted_iota(jnp.int32, (nc, 2 * SSM_STATE), 0)

    def shift(x, d):
        return jnp.where(row >= d, pltpu.roll(x, d, 0), 0.0)

    for j in range(n_steps):
        ar, ai = apr_ref[0, j:j + 1, :], api_ref[0, j:j + 1, :]
        sr, si = shift(xr, 1 << j), shift(xi, 1 << j)
        xr, xi = xr + (ar * sr - ai * si), xi + (ar * si + ai * sr)
    prev = jnp.concatenate([shift(xr, 1), shift(xi, 1)], axis=1).astype(BF16)
    y = jnp.concatenate(
        [jnp.dot(u[:, :half], toep_ref[0], preferred_element_type=F32),
         jnp.dot(u[:, half:], toep_ref[1], preferred_element_type=F32)], axis=1)
    y = y + jnp.dot(prev, wcin_ref[0], preferred_element_type=F32) + d_ref[0] * u.astype(F32)
    y = 0.5 * y * (1.0 + lax.erf(y * (0.5 ** 0.5)))
    y_ref[0] = y.astype(BF16)


def _ssm(u_full, tables):
    toep, wout, wcin, apr, api, dsk, n_steps = tables
    l = u_full.shape[0]
    nc = l // SSM_CHUNK
    width = 2 * SSM_CHUNK * SSM_GROUP
    up = u_full.reshape(nc, SSM_CHUNK, SSM_PAIRS, 2, SSM_GROUP)
    up = up.transpose(2, 0, 3, 1, 4).reshape(SSM_PAIRS, nc, width)
    yp = pl.pallas_call(
        functools.partial(_ssm_kernel, n_steps=n_steps),
        grid=(SSM_PAIRS,),
        in_specs=[pl.BlockSpec((1, nc, width), lambda g: (g, 0, 0)),
                  pl.BlockSpec((2, width // 2, width // 2), lambda g: (g, 0, 0)),
                  pl.BlockSpec((1, width, 4 * SSM_STATE), lambda g: (g, 0, 0)),
                  pl.BlockSpec((1, 4 * SSM_STATE, width), lambda g: (g, 0, 0)),
                  pl.BlockSpec((1, n_steps, 2 * SSM_STATE), lambda g: (g, 0, 0)),
                  pl.BlockSpec((1, n_steps, 2 * SSM_STATE), lambda g: (g, 0, 0)),
                  pl.BlockSpec((1, 1, width), lambda g: (g, 0, 0))],
        out_specs=pl.BlockSpec((1, nc, width), lambda g: (g, 0, 0)),
        out_shape=jax.ShapeDtypeStruct((SSM_PAIRS, nc, width), BF16),
        compiler_params=_params(("parallel",), 40),
        name="s5_scan",
    )(up, toep, wout, wcin, apr, api, dsk)
    yp = yp.reshape(SSM_PAIRS, nc, 2, SSM_CHUNK, SSM_GROUP).transpose(1, 3, 0, 2, 4)
    return yp.reshape(l, SSM_WIDTH)


def _glu_kernel(y_ref, w_ref, o_ref):
    y = y_ref[...]
    gate = jax.nn.sigmoid(jnp.dot(y, w_ref[...].astype(BF16), preferred_element_type=F32))
    o_ref[...] = (y.astype(F32) * gate).astype(BF16)


def _glu(y, w):
    m, d = y.shape
    tm = _tile(m, 1024)
    return pl.pallas_call(
        _glu_kernel,
        grid=(m // tm,),
        in_specs=[pl.BlockSpec((tm, d), lambda i: (i, 0)),
                  pl.BlockSpec((d, d), lambda i: (0, 0))],
        out_specs=pl.BlockSpec((tm, d), lambda i: (i, 0)),
        out_shape=jax.ShapeDtypeStruct((m, d), BF16),
        compiler_params=_params(("parallel",), 40),
        name="ssm_glu",
    )(y, w)


def _mix_kernel(a_ref, y_ref, wa_ref, ws_ref, ga_ref, gs_ref, o_ref):
    attn_d = jnp.dot(a_ref[...], wa_ref[...].astype(BF16), preferred_element_type=F32)
    ssm_d = jnp.dot(y_ref[...], ws_ref[...].astype(BF16), preferred_element_type=F32)
    mix = (jax.nn.sigmoid(ga_ref[...].astype(F32)) * attn_d
           + jax.nn.sigmoid(gs_ref[...].astype(F32)) * ssm_d)
    o_ref[...] = mix.astype(BF16)


def _mix(attn, y2, w_attn_o, w_ssm_o, z):
    m = attn.shape[0]
    tm, tn = _tile(m, 1024), 512
    ga_blk, gs_blk = COL_GA // tn, COL_GS // tn
    return pl.pallas_call(
        _mix_kernel,
        grid=(m // tm, D_MODEL // tn),
        in_specs=[pl.BlockSpec((tm, ATTN_WIDTH), lambda i, j: (i, 0)),
                  pl.BlockSpec((tm, SSM_WIDTH), lambda i, j: (i, 0)),
                  pl.BlockSpec((ATTN_WIDTH, tn), lambda i, j: (0, j)),
                  pl.BlockSpec((SSM_WIDTH, tn), lambda i, j: (0, j)),
                  pl.BlockSpec((tm, tn), lambda i, j: (i, j + ga_blk)),
                  pl.BlockSpec((tm, tn), lambda i, j: (i, j + gs_blk))],
        out_specs=pl.BlockSpec((tm, tn), lambda i, j: (i, j)),
        out_shape=jax.ShapeDtypeStruct((m, D_MODEL), BF16),
        compiler_params=_params(("parallel", "arbitrary"), 48),
        name="gated_merge",
    )(attn, y2, w_attn_o, w_ssm_o, z, z)


def _resid_kernel(a_ref, w_ref, h_ref, o_ref):
    o_ref[...] = DEEPNORM_ALPHA * h_ref[...] + jnp.dot(
        a_ref[...], w_ref[...].astype(BF16), preferred_element_type=F32)


def _out_proj(mix, w_out, h0):
    m = mix.shape[0]
    tm, tn = _tile(m, 1024), 512
    return pl.pallas_call(
        _resid_kernel,
        grid=(m // tm, D_MODEL // tn),
        in_specs=[pl.BlockSpec((tm, D_MODEL), lambda i, j: (i, 0)),
                  pl.BlockSpec((D_MODEL, tn), lambda i, j: (0, j)),
                  pl.BlockSpec((tm, tn), lambda i, j: (i, j))],
        out_specs=pl.BlockSpec((tm, tn), lambda i, j: (i, j)),
        out_shape=jax.ShapeDtypeStruct((m, D_MODEL), F32),
        compiler_params=_params(("parallel", "arbitrary"), 56),
        name="out_proj",
    )(mix, w_out, h0)


def _token_mixer(x, meta_tokens, ln0_g, ln0_b, w_in, attn_sinks, w_attn_o, ssm_tables_args,
                 w_glu, w_ssm_o, w_out, ln1_g, ln1_b):
    t = x.shape[0]
    lead = jnp.concatenate([jnp.zeros((N_PAD, D_MODEL), F32), meta_tokens.astype(F32)], axis=0)
    h0, h0b = _layer_norm(x, ln0_g, ln0_b)
    _, leadb = _layer_norm(lead, ln0_g, ln0_b)

    z = _project(h0b, w_in, 0, IN_WIDTH)
    z_lead = _project(leadb, w_in, COL_KV, 2 * KV_WIDTH + SSM_WIDTH)

    attn = _attention(z, z_lead, attn_sinks)

    lead_valid = (jnp.arange(BLOCK) >= N_PAD)[:, None]
    u_full = jnp.concatenate(
        [jnp.where(lead_valid, z_lead[:, 2 * KV_WIDTH:], 0).astype(BF16), z[:, COL_U:COL_GA]], axis=0)
    tables = _ssm_tables(*ssm_tables_args, n_chunks=(t + BLOCK) // SSM_CHUNK)
    y = _ssm(u_full, tables)[BLOCK:]
    y2 = _glu(y, w_glu)

    mix = _mix(attn, y2, w_attn_o, w_ssm_o, z)
    r = _out_proj(mix, w_out, h0)
    return _layer_norm(r, ln1_g, ln1_b)


def _router_kernel(h_ref, wr_ref, bias_ref, idx_ref, wt_ref):
    tm = h_ref.shape[0]
    ng, ne = N_EXPERT_GROUPS, EXPERTS_PER_GROUP
    logits = lax.dot_general(wr_ref[...], h_ref[...], (((1,), (1,)), ((), ())),
                             precision=lax.Precision.HIGHEST, preferred_element_type=F32)
    scores = jax.nn.sigmoid(logits)
    sel = (scores + bias_ref[...]).reshape(ng, ne, tm)
    scores = scores.reshape(ng, ne, tm)
    e_in = lax.broadcasted_iota(jnp.int32, (ng, ne, tm), 1)
    e_id = lax.broadcasted_iota(jnp.int32, (ng, ne, tm), 0) * ne + e_in
    neg = -jnp.inf

    top1 = jnp.max(sel, axis=1, keepdims=True)
    first = jnp.min(jnp.where(sel == top1, e_in, ne), axis=1, keepdims=True)
    top2 = jnp.max(jnp.where(e_in == first, neg, sel), axis=1, keepdims=True)
    grp = (top1 + top2)[:, 0, :]

    g_id = lax.broadcasted_iota(jnp.int32, (ng, tm), 0)
    chosen = jnp.zeros((ng, tm), jnp.bool_)
    for _ in range(TOPK_GROUPS):
        best = jnp.max(grp, axis=0, keepdims=True)
        pick = g_id == jnp.min(jnp.where(grp == best, g_id, ng), axis=0, keepdims=True)
        chosen = chosen | pick
        grp = jnp.where(pick, neg, grp)

    cur = jnp.where(chosen[:, None, :], sel, NEG_INF)
    ids, wts = [], []
    for _ in range(TOP_K):
        best = jnp.max(jnp.max(cur, axis=1, keepdims=True), axis=0, keepdims=True)
        cand = jnp.where(cur == best, e_id, N_EXPERTS)
        win = jnp.min(jnp.min(cand, axis=1, keepdims=True), axis=0, keepdims=True)
        pick = e_id == win
        wsel = jnp.where(pick, scores, 0.0)
        ids.append(win[0])
        wts.append(jnp.sum(jnp.sum(wsel, axis=1, keepdims=True), axis=0, keepdims=True)[0])
        cur = jnp.where(pick, neg, cur)
    w = jnp.concatenate(wts, axis=0)
    idx_ref[...] = jnp.concatenate(ids, axis=0)
    wt_ref[...] = w / jnp.sum(w, axis=0, keepdims=True) * ROUTED_SCALE


def _router(h1, w_router, router_bias):
    t, d = h1.shape
    tm = _tile(t, 512)
    return pl.pallas_call(
        _router_kernel,
        grid=(t // tm,),
        in_specs=[pl.BlockSpec((tm, d), lambda i: (i, 0)),
                  pl.BlockSpec((N_EXPERTS, d), lambda i: (0, 0)),
                  pl.BlockSpec((N_EXPERTS, 1), lambda i: (0, 0))],
        out_specs=[pl.BlockSpec((TOP_K, tm), lambda i: (0, i)),
                   pl.BlockSpec((TOP_K, tm), lambda i: (0, i))],
        out_shape=[jax.ShapeDtypeStruct((TOP_K, t), jnp.int32),
                   jax.ShapeDtypeStruct((TOP_K, t), F32)],
        compiler_params=_params(("parallel",), 40),
        name="router",
    )(h1, w_router.T, router_bias.reshape(N_EXPERTS, 1))


def _dispatch(idx_t, w_t):
    t = idx_t.shape[1]
    n_assign = TOP_K * t
    n_blocks = -(-n_assign // ROW_BLOCK) + N_EXPERTS
    n_rows = n_blocks * ROW_BLOCK
    e_flat = idx_t.reshape(-1)
    onehot = (e_flat[:, None] == jnp.arange(N_EXPERTS, dtype=jnp.int32)[None, :]).astype(jnp.int32)
    csum = jnp.cumsum(onehot, axis=0)
    rank = jnp.take_along_axis(csum, e_flat[:, None], axis=1)[:, 0] - 1
    counts = csum[-1]
    padded = (counts + ROW_BLOCK - 1) // ROW_BLOCK * ROW_BLOCK
    ends = jnp.cumsum(padded)
    starts = ends - padded
    dest = starts[e_flat] + rank
    tok_flat = jnp.tile(jnp.arange(t, dtype=jnp.int32), TOP_K)
    row_tok = jnp.zeros((n_rows,), jnp.int32).at[dest].set(tok_flat)
    row_w = jnp.zeros((n_rows,), F32).at[dest].set(w_t.reshape(-1))
    block_expert = jnp.minimum(
        jnp.searchsorted(ends, jnp.arange(n_blocks, dtype=ends.dtype) * ROW_BLOCK, side='right'),
        N_EXPERTS - 1).astype(jnp.int32)
    n_used = (ends[-1] // ROW_BLOCK).astype(jnp.int32).reshape(1)
    has = counts > 0
    cand = jnp.where(has[None, :] & (jnp.arange(N_EXPERTS)[None, :] > jnp.arange(N_EXPERTS)[:, None]),
                     jnp.arange(N_EXPERTS)[None, :], N_EXPERTS)
    nxt = jnp.min(cand, axis=1)
    next_expert = jnp.where(nxt < N_EXPERTS, nxt, -1).astype(jnp.int32)
    return (row_tok.reshape(n_blocks, 1, ROW_BLOCK), row_w.reshape(n_rows, 1), block_expert, n_used,
            next_expert, dest.reshape(TOP_K, t))


def _gather_rows(src_hbm, idx_smem, dst, sem, n):
    def body(i, carry):
        pltpu.make_async_copy(src_hbm.at[pl.ds(idx_smem[0, i], 1)], dst.at[pl.ds(i, 1)], sem).start()
        return carry
    lax.fori_loop(0, n, body, 0)


def _wait_rows(src_hbm, dst, sem, n):
    pltpu.make_async_copy(src_hbm.at[pl.ds(0, n)], dst, sem).wait()


def _expert_up_kernel(be_ref, nu_ref, nx_ref, tok_hbm, h_hbm, w1_hbm, w3_hbm, o_ref,
                      idx_smem, xbuf, wstage, wcast, isem, xsem, wsem):
    b = pl.program_id(0)
    n_used = nu_ref[0]
    e = be_ref[b]
    slot = b % 2

    def idx_copy(blk, s):
        return pltpu.make_async_copy(tok_hbm.at[blk], idx_smem.at[s], isem.at[s])

    def weight_copies(ex):
        return (pltpu.make_async_copy(w1_hbm.at[ex], wstage.at[0], wsem.at[0]),
                pltpu.make_async_copy(w3_hbm.at[ex], wstage.at[1], wsem.at[1]))

    @pl.when(b == 0)
    def _():
        for c in weight_copies(e):
            c.start()
        idx_copy(0, 0).start()
        idx_copy(0, 0).wait()
        _gather_rows(h_hbm, idx_smem.at[0], xbuf.at[0], xsem.at[0], ROW_BLOCK)

        @pl.when(n_used > 1)
        def _():
            idx_copy(1, 1).start()

    @pl.when(b + 1 < n_used)
    def _():
        idx_copy(b + 1, 1 - slot).wait()
        _gather_rows(h_hbm, idx_smem.at[1 - slot], xbuf.at[1 - slot], xsem.at[1 - slot], ROW_BLOCK)

        @pl.when(b + 2 < n_used)
        def _():
            idx_copy(b + 2, slot).start()

    @pl.when(b < n_used)
    def _():
        first_of_expert = jnp.logical_or(b == 0, be_ref[jnp.maximum(b - 1, 0)] != e)

        @pl.when(first_of_expert)
        def _():
            for c in weight_copies(e):
                c.wait()
            wcast[0] = wstage[0].astype(BF16)
            wcast[1] = wstage[1].astype(BF16)
            nxt = nx_ref[e]

            @pl.when(nxt >= 0)
            def _():
                for c in weight_copies(nxt):
                    c.start()

        _wait_rows(h_hbm, xbuf.at[slot], xsem.at[slot], ROW_BLOCK)
        x = xbuf[slot].astype(BF16)
        a = jnp.dot(x, wcast[0], preferred_element_type=F32)
        g = jnp.dot(x, wcast[1], preferred_element_type=F32)
        o_ref[...] = (a * jax.nn.sigmoid(a) * g).astype(BF16)

    @pl.when(b >= n_used)
    def _():
        o_ref[...] = jnp.zeros_like(o_ref)


def _expert_up(h1, row_tok, block_expert, n_used, next_expert, w1, w3):
    n_blocks = row_tok.shape[0]
    d = h1.shape[1]
    grid_spec = pltpu.PrefetchScalarGridSpec(
        num_scalar_prefetch=3,
        grid=(n_blocks,),
        in_specs=[pl.BlockSpec(memory_space=pl.ANY)] * 4,
        out_specs=pl.BlockSpec((ROW_BLOCK, EXPERT_DIM), lambda b, *_: (b, 0)),
        scratch_shapes=[pltpu.SMEM((2, 1, ROW_BLOCK), jnp.int32),
                        pltpu.VMEM((2, ROW_BLOCK, d), F32),
                        pltpu.VMEM((2, d, EXPERT_DIM), F32),
                        pltpu.VMEM((2, d, EXPERT_DIM), BF16),
                        pltpu.SemaphoreType.DMA((2,)),
                        pltpu.SemaphoreType.DMA((2,)),
                        pltpu.SemaphoreType.DMA((2,))])
    return pl.pallas_call(
        _expert_up_kernel,
        grid_spec=grid_spec,
        out_shape=jax.ShapeDtypeStruct((n_blocks * ROW_BLOCK, EXPERT_DIM), BF16),
        compiler_params=_params(("arbitrary",), 48),
        name="expert_up",
    )(block_expert, n_used, next_expert, row_tok, h1, w1, w3)


def _expert_down_kernel(be_ref, nu_ref, nx_ref, a_ref, rw_ref, w2_hbm, o_ref, wstage, wcast, wsem):
    b = pl.program_id(0)
    n_used = nu_ref[0]
    e = be_ref[b]

    def weight_copy(ex):
        return pltpu.make_async_copy(w2_hbm.at[ex], wstage, wsem)

    @pl.when(b == 0)
    def _():
        weight_copy(e).start()

    @pl.when(b < n_used)
    def _():
        first_of_expert = jnp.logical_or(b == 0, be_ref[jnp.maximum(b - 1, 0)] != e)

        @pl.when(first_of_expert)
        def _():
            weight_copy(e).wait()
            wcast[...] = wstage[...].astype(BF16)
            nxt = nx_ref[e]

            @pl.when(nxt >= 0)
            def _():
                weight_copy(nxt).start()

        o_ref[...] = jnp.dot(a_ref[...], wcast[...], preferred_element_type=F32) * rw_ref[...]

    @pl.when(b >= n_used)
    def _():
        o_ref[...] = jnp.zeros_like(o_ref)


def _expert_down(act, row_w, block_expert, n_used, next_expert, w2):
    n_rows = act.shape[0]
    n_blocks = n_rows // ROW_BLOCK
    d = w2.shape[2]
    grid_spec = pltpu.PrefetchScalarGridSpec(
        num_scalar_prefetch=3,
        grid=(n_blocks,),
        in_specs=[pl.BlockSpec((ROW_BLOCK, EXPERT_DIM), lambda b, *_: (b, 0)),
                  pl.BlockSpec((ROW_BLOCK, 1), lambda b, *_: (b, 0)),
                  pl.BlockSpec(memory_space=pl.ANY)],
        out_specs=pl.BlockSpec((ROW_BLOCK, d), lambda b, *_: (b, 0)),
        scratch_shapes=[pltpu.VMEM((EXPERT_DIM, d), F32),
                        pltpu.VMEM((EXPERT_DIM, d), BF16),
                        pltpu.SemaphoreType.DMA(())])
    return pl.pallas_call(
        _expert_down_kernel,
        grid_spec=grid_spec,
        out_shape=jax.ShapeDtypeStruct((n_rows, d), F32),
        compiler_params=_params(("arbitrary",), 40),
        name="expert_down",
    )(block_expert, n_used, next_expert, act, row_w, w2)


def _shared_kernel(h_ref, w1_ref, w3_ref, w2_ref, o_ref):
    x = h_ref[...]
    a = jnp.dot(x, w1_ref[...], preferred_element_type=F32)
    g = jnp.dot(x, w3_ref[...], preferred_element_type=F32)
    act = (a * jax.nn.sigmoid(a) * g).astype(BF16)
    o_ref[...] = jnp.dot(act, w2_ref[...], preferred_element_type=F32)


def _shared_expert(h1b, w1, w3, w2):
    t, d = h1b.shape
    tm = _tile(t, 512)
    return pl.pallas_call(
        _shared_kernel,
        grid=(t // tm,),
        in_specs=[pl.BlockSpec((tm, d), lambda i: (i, 0)),
                  pl.BlockSpec((d, SHARED_DIM), lambda i: (0, 0)),
                  pl.BlockSpec((d, SHARED_DIM), lambda i: (0, 0)),
                  pl.BlockSpec((SHARED_DIM, d), lambda i: (0, 0))],
        out_specs=pl.BlockSpec((tm, d), lambda i: (i, 0)),
        out_shape=jax.ShapeDtypeStruct((t, d), F32),
        compiler_params=_params(("parallel",), 56),
        name="shared_expert",
    )(h1b, w1.astype(BF16), w3.astype(BF16), w2.astype(BF16))


COMBINE_TOKENS = 64


def _combine_kernel(dest_hbm, y_hbm, h_ref, s_ref, g_ref, b_ref, o_ref, idx_smem, ybuf, isem, ysem):
    i = pl.program_id(0)
    n = pl.num_programs(0)
    slot = i % 2
    rows = TOP_K * COMBINE_TOKENS

    def idx_copy(blk, s):
        return pltpu.make_async_copy(dest_hbm.at[blk], idx_smem.at[s], isem.at[s])

    @pl.when(i == 0)
    def _():
        idx_copy(0, 0).start()
        idx_copy(0, 0).wait()
        _gather_rows(y_hbm, idx_smem.at[0], ybuf.at[0], ysem.at[0], rows)

        @pl.when(n > 1)
        def _():
            idx_copy(1, 1).start()

    @pl.when(i + 1 < n)
    def _():
        idx_copy(i + 1, 1 - slot).wait()
        _gather_rows(y_hbm, idx_smem.at[1 - slot], ybuf.at[1 - slot], ysem.at[1 - slot], rows)

        @pl.when(i + 2 < n)
        def _():
            idx_copy(i + 2, slot).start()

    _wait_rows(y_hbm, ybuf.at[slot], ysem.at[slot], rows)
    acc = DEEPNORM_ALPHA * h_ref[...] + s_ref[...]
    for j in range(TOP_K):
        acc = acc + ybuf[slot, j * COMBINE_TOKENS:(j + 1) * COMBINE_TOKENS, :]
    o_ref[...] = _ln_rows(acc, g_ref[...], b_ref[...])


def _combine(yb, dest, h1, shared, g, b):
    t, d = h1.shape
    tm = COMBINE_TOKENS
    n_tiles = t // tm
    dest_tiles = dest.reshape(TOP_K, n_tiles, tm).transpose(1, 0, 2).reshape(n_tiles, 1, TOP_K * tm)
    return pl.pallas_call(
        _combine_kernel,
        grid=(n_tiles,),
        in_specs=[pl.BlockSpec(memory_space=pl.ANY),
                  pl.BlockSpec(memory_space=pl.ANY),
                  pl.BlockSpec((tm, d), lambda i: (i, 0)),
                  pl.BlockSpec((tm, d), lambda i: (i, 0)),
                  pl.BlockSpec((1, d), lambda i: (0, 0)),
                  pl.BlockSpec((1, d), lambda i: (0, 0))],
        out_specs=pl.BlockSpec((tm, d), lambda i: (i, 0)),
        out_shape=jax.ShapeDtypeStruct((t, d), F32),
        scratch_shapes=[pltpu.SMEM((2, 1, TOP_K * tm), jnp.int32),
                        pltpu.VMEM((2, TOP_K * tm, d), F32),
                        pltpu.SemaphoreType.DMA((2,)),
                        pltpu.SemaphoreType.DMA((2,))],
        compiler_params=_params(("arbitrary",), 40),
        name="combine_norm",
    )(dest_tiles, yb, h1, shared, g.reshape(1, d), b.reshape(1, d))


def _channel_mixer(h1, h1b, w_router, router_bias, expert_w1, expert_w3, expert_w2,
                   shared_w1, shared_w3, shared_w2, ln2_g, ln2_b):
    idx_t, w_t = _router(h1, w_router, router_bias)
    row_tok, row_w, block_expert, n_used, next_expert, dest = _dispatch(idx_t, w_t)
    shared = _shared_expert(h1b, shared_w1, shared_w3, shared_w2)
    act = _expert_up(h1, row_tok, block_expert, n_used, next_expert, expert_w1, expert_w3)
    yb = _expert_down(act, row_w, block_expert, n_used, next_expert, expert_w2)
    return _combine(yb, dest, h1, shared, ln2_g, ln2_b)


def kernel(x, meta_tokens, ln0_g, ln0_b, w_in, attn_sinks, w_attn_o, ssm_lam_re, ssm_lam_im, ssm_log_dt,
           ssm_b_re, ssm_b_im, ssm_c_re, ssm_c_im, ssm_d, w_glu, w_ssm_o, w_out, ln1_g, ln1_b, w_router,
           router_bias, expert_w1, expert_w3, expert_w2, shared_w1, shared_w3, shared_w2, ln2_g, ln2_b):
    assert x.shape[0] == 1 and w_in.shape[0] == 1
    ssm_args = (ssm_lam_re[0], ssm_lam_im[0], ssm_log_dt[0], ssm_b_re[0], ssm_b_im[0],
                ssm_c_re[0], ssm_c_im[0], ssm_d[0])
    h1, h1b = _token_mixer(x[0], meta_tokens, ln0_g, ln0_b, w_in[0], attn_sinks[0], w_attn_o[0],
                           ssm_args, w_glu[0], w_ssm_o[0], w_out[0], ln1_g[0], ln1_b[0])
    out = _channel_mixer(h1, h1b, w_router[0], router_bias[0], expert_w1[0], expert_w3[0], expert_w2[0],
                         shared_w1[0], shared_w3[0], shared_w2[0], ln2_g[0], ln2_b[0])
    return out[None]
```

```python
import functools
import math

import jax
import jax.numpy as jnp
import numpy as np
from jax import lax
from jax.experimental import pallas as pl
from jax.experimental.pallas import tpu as pltpu

F32 = jnp.float32
BF16 = jnp.bfloat16

D_MODEL = 4096
N_META = 16
BLOCK = 128
N_PAD = BLOCK - N_META

N_HEADS = 32
N_KV_HEADS = 4
HEAD_DIM = 64
HEADS_PER_KV = N_HEADS // N_KV_HEADS
ATTN_WIDTH = N_HEADS * HEAD_DIM
KV_WIDTH = N_KV_HEADS * HEAD_DIM

SSM_WIDTH = 1024
SSM_GROUP = 16
N_SSM_GROUPS = SSM_WIDTH // SSM_GROUP
SSM_STATE = 64
SSM_CHUNK = 16
SSM_TILE_GROUPS = 128 // SSM_GROUP
SSM_TILES = N_SSM_GROUPS // SSM_TILE_GROUPS

N_EXPERTS = 64
TOP_K = 8
N_EXPERT_GROUPS = 8
EXPERTS_PER_GROUP = N_EXPERTS // N_EXPERT_GROUPS
TOPK_GROUPS = 4
EXPERT_DIM = 512
SHARED_DIM = 512
ROUTED_SCALE = 2.5
ROW_BLOCK = 256

IN_WIDTH = ATTN_WIDTH + 2 * KV_WIDTH + SSM_WIDTH + 2 * D_MODEL
COL_KV = ATTN_WIDTH
COL_U = ATTN_WIDTH + 2 * KV_WIDTH
COL_GA = COL_U + SSM_WIDTH
COL_GS = COL_GA + D_MODEL

DEEPNORM_ALPHA = 2.0 ** 0.25
LN_EPS = 1e-5
NEG_INF = -1e30

MIB = 1024 * 1024


def _params(semantics, vmem_mib):
    return pltpu.CompilerParams(dimension_semantics=semantics, vmem_limit_bytes=vmem_mib * MIB)


def _tile(n, pref):
    t = min(n, pref)
    while n % t:
        t -= 128
    return t


def _ln_rows(x, g, b):
    mu = jnp.mean(x, axis=-1, keepdims=True)
    xc = x - mu
    var = jnp.mean(xc * xc, axis=-1, keepdims=True)
    return xc * lax.rsqrt(var + LN_EPS) * g + b


def _ln_kernel(x_ref, g_ref, b_ref, o_ref, ob_ref):
    y = _ln_rows(x_ref[...], g_ref[...], b_ref[...])
    o_ref[...] = y
    ob_ref[...] = y.astype(BF16)


def _layer_norm(x, g, b):
    m, d = x.shape
    tm = _tile(m, 256)
    out_specs = [pl.BlockSpec((tm, d), lambda i: (i, 0)), pl.BlockSpec((tm, d), lambda i: (i, 0))]
    out_shape = [jax.ShapeDtypeStruct((m, d), F32), jax.ShapeDtypeStruct((m, d), BF16)]
    return pl.pallas_call(
        _ln_kernel,
        grid=(m // tm,),
        in_specs=[pl.BlockSpec((tm, d), lambda i: (i, 0)),
                  pl.BlockSpec((1, d), lambda i: (0, 0)),
                  pl.BlockSpec((1, d), lambda i: (0, 0))],
        out_specs=out_specs,
        out_shape=out_shape,
        compiler_params=_params(("parallel",), 40),
        name="layer_norm",
    )(x, g.reshape(1, d), b.reshape(1, d))


def _proj_kernel(a_ref, w_ref, o_ref):
    o_ref[...] = jnp.dot(a_ref[...], w_ref[...].astype(BF16),
                         preferred_element_type=F32).astype(o_ref.dtype)


def _project(a, w, col0, n_cols, tm_pref=1024, tn=512):
    m, k = a.shape
    tm = _tile(m, tm_pref)
    off = col0 // tn
    return pl.pallas_call(
        _proj_kernel,
        grid=(m // tm, n_cols // tn),
        in_specs=[pl.BlockSpec((tm, k), lambda i, j: (i, 0)),
                  pl.BlockSpec((k, tn), lambda i, j: (0, j + off))],
        out_specs=pl.BlockSpec((tm, tn), lambda i, j: (i, j)),
        out_shape=jax.ShapeDtypeStruct((m, n_cols), BF16),
        compiler_params=_params(("parallel", "arbitrary"), 56),
        name="in_proj",
    )(a, w)


def _attn_kernel(sink_ref, q_ref, kvc_ref, kvp_ref, kvm_ref, o_ref):
    first = pl.program_id(0) == 0
    kv_all = jnp.concatenate([kvp_ref[...], kvc_ref[...], kvm_ref[N_PAD:BLOCK, :]], axis=0)
    n_keys = 2 * BLOCK + N_META
    rows = HEADS_PER_KV * BLOCK

    lane = lax.broadcasted_iota(jnp.int32, (BLOCK, BLOCK), 1)
    lo_q = lane < HEAD_DIM
    lane_kv = lax.broadcasted_iota(jnp.int32, (n_keys, BLOCK), 1)
    lo_kv = lane_kv < HEAD_DIM

    qi = lax.broadcasted_iota(jnp.int32, (rows, n_keys), 0) & (BLOCK - 1)
    kj = lax.broadcasted_iota(jnp.int32, (rows, n_keys), 1)
    in_window = (kj > qi) & (kj <= qi + BLOCK) & ((kj >= BLOCK) | jnp.logical_not(first))
    visible = in_window | (kj >= 2 * BLOCK)
    local_head = lax.broadcasted_iota(jnp.int32, (rows, 1), 0) // BLOCK

    def dup_head(tile, odd):
        t = tile.astype(F32)
        r = pltpu.roll(t, HEAD_DIM, 1)
        keep = jnp.logical_not(lo_kv) if odd else lo_kv
        return jnp.where(keep, t, r).astype(BF16)

    for kh in range(N_KV_HEADS):
        t0 = (kh // 2) * BLOCK
        k2 = dup_head(kv_all[:, t0:t0 + BLOCK], kh % 2)
        v2 = dup_head(kv_all[:, KV_WIDTH + t0:KV_WIDTH + t0 + BLOCK], kh % 2)
        pieces = []
        for p in range(HEADS_PER_KV // 2):
            c0 = kh * HEADS_PER_KV * HEAD_DIM + p * BLOCK
            qp = q_ref[:, c0:c0 + BLOCK]
            zero = jnp.zeros_like(qp)
            pieces.append(jnp.where(lo_q, qp, zero))
            pieces.append(jnp.where(lo_q, zero, qp))
        qs = jnp.concatenate(pieces, axis=0)
        s = lax.dot_general(qs, k2, (((1,), (1,)), ((), ())), preferred_element_type=F32)
        s = jnp.where(visible, s * (HEAD_DIM ** -0.5), NEG_INF)
        sink = jnp.zeros((rows, 1), F32)
        for t in range(HEADS_PER_KV):
            sink = jnp.where(local_head == t, sink_ref[kh * HEADS_PER_KV + t], sink)
        m = jnp.maximum(jnp.max(s, axis=-1, keepdims=True), sink)
        e = jnp.exp(s - m)
        den = jnp.sum(e, axis=-1, keepdims=True) + jnp.exp(sink - m)
        o2 = jnp.dot(e.astype(BF16), v2, preferred_element_type=F32) / den
        for p in range(HEADS_PER_KV // 2):
            c0 = kh * HEADS_PER_KV * HEAD_DIM + p * BLOCK
            r0 = 2 * p * BLOCK
            o_ref[:, c0:c0 + BLOCK] = jnp.where(
                lo_q, o2[r0:r0 + BLOCK], o2[r0 + BLOCK:r0 + 2 * BLOCK]).astype(BF16)


def _attention(z, z_lead, sinks):
    t = z.shape[0]
    kv_blk = COL_KV // (2 * KV_WIDTH)
    return pl.pallas_call(
        _attn_kernel,
        grid=(t // BLOCK,),
        in_specs=[pl.BlockSpec(memory_space=pltpu.SMEM),
                  pl.BlockSpec((BLOCK, ATTN_WIDTH), lambda n: (n, 0)),
                  pl.BlockSpec((BLOCK, 2 * KV_WIDTH), lambda n: (n, kv_blk)),
                  pl.BlockSpec((BLOCK, 2 * KV_WIDTH), lambda n: (jnp.maximum(n - 1, 0), kv_blk)),
                  pl.BlockSpec((BLOCK, 2 * KV_WIDTH), lambda n: (0, 0))],
        out_specs=pl.BlockSpec((BLOCK, ATTN_WIDTH), lambda n: (n, 0)),
        out_shape=jax.ShapeDtypeStruct((t, ATTN_WIDTH), BF16),
        compiler_params=_params(("parallel",), 40),
        name="swa_attention",
    )(sinks.astype(F32), z, z, z, z_lead)


def _cmul(ar, ai, br, bi):
    return ar * br - ai * bi, ar * bi + ai * br


def _ssm_tables(lam_re, lam_im, log_dt, b_re, b_im, c_re, c_im, d_skip, n_chunks):
    hi = lax.Precision.HIGHEST
    g, p, h, c = N_SSM_GROUPS, SSM_STATE, SSM_GROUP, SSM_CHUNK
    dt = jnp.exp(log_dt)[:, None]
    mag = jnp.exp(dt * lam_re)
    are = mag * jnp.cos(dt * lam_im)
    aim = mag * jnp.sin(dt * lam_im)
    den = lam_re * lam_re + lam_im * lam_im
    num_re = are - 1.0
    coef_re = (num_re * lam_re + aim * lam_im) / den
    coef_im = (aim * lam_re - num_re * lam_im) / den
    bb_re = coef_re[..., None] * b_re - coef_im[..., None] * b_im
    bb_im = coef_re[..., None] * b_im + coef_im[..., None] * b_re

    pr, pi = [jnp.ones_like(are)], [jnp.zeros_like(are)]
    for _ in range(c):
        r, i = _cmul(pr[-1], pi[-1], are, aim)
        pr.append(r)
        pi.append(i)
    pr, pi = jnp.stack(pr), jnp.stack(pi)

    m_re = pr[:c, :, :, None] * bb_re[None] - pi[:c, :, :, None] * bb_im[None]
    m_im = pr[:c, :, :, None] * bb_im[None] + pi[:c, :, :, None] * bb_re[None]
    lag = (jnp.einsum('ghp,kgpj->kghj', c_re, m_re, precision=hi)
           - jnp.einsum('ghp,kgpj->kghj', c_im, m_im, precision=hi))
    nt, gt = SSM_TILES, SSM_TILE_GROUPS
    lagc = lag.reshape(c, nt, gt, h, h).transpose(1, 0, 2, 4, 3)
    lagc = lagc.reshape(nt, c, gt * h, h)
    wo = jnp.stack([m_re[::-1], m_im[::-1]])
    wo = wo.reshape(2, c, nt, gt, p, h).transpose(2, 1, 3, 5, 0, 4)
    wout_a = wo.reshape(nt, c, gt * h, 2 * p)
    wout_b = wo[:, :, :, :, ::-1, :].reshape(nt, c, gt * h, 2 * p)
    cin_re = c_re[None] * pr[1:, :, None, :] - c_im[None] * pi[1:, :, None, :]
    cin_im = -(c_re[None] * pi[1:, :, None, :] + c_im[None] * pr[1:, :, None, :])
    cinc = jnp.stack([cin_re, cin_im]).reshape(2, c, nt, gt, h, p)
    cinc = cinc.transpose(2, 1, 0, 3, 5, 4).reshape(nt, c, 2 * gt * p, h)
    spread = jnp.tile(jnp.eye(h, dtype=BF16), (1, gt))

    n_steps = max(1, math.ceil(math.log2(n_chunks)))
    qr, qi = [pr[c]], [pi[c]]
    for _ in range(n_steps - 1):
        r, i = _cmul(qr[-1], qi[-1], qr[-1], qi[-1])
        qr.append(r)
        qi.append(i)
    apr = jnp.stack(qr, axis=0).reshape(n_steps, nt, gt * p).transpose(1, 0, 2)
    api = jnp.stack(qi, axis=0).reshape(n_steps, nt, gt * p).transpose(1, 0, 2)
    dsk = d_skip.reshape(nt, 1, gt * h)
    return (lagc.astype(BF16), wout_a.astype(BF16), wout_b.astype(BF16), cinc.astype(BF16), spread,
            apr, api, dsk, n_steps)


def _chunk_steps(u_ref):
    n = u_ref.shape[0] // SSM_CHUNK
    return [u_ref[pl.ds(s, n, stride=SSM_CHUNK), :] for s in range(SSM_CHUNK)]


def _chunk_lanes(u_ref):
    return jnp.concatenate(_chunk_steps(u_ref), axis=1)


def _ssm_state_kernel(u_ref, lead_ref, wa_ref, wb_ref, apr_ref, api_ref, prev_ref, wout_s, *, n_steps):
    p, gh = SSM_STATE, SSM_TILE_GROUPS * SSM_GROUP
    row_g = lax.broadcasted_iota(jnp.int32, (gh, 2 * p), 0) // SSM_GROUP
    lo = lax.broadcasted_iota(jnp.int32, (gh, 2 * p), 1) < p
    zero = jnp.zeros((gh, 2 * p), BF16)
    for s in range(SSM_CHUNK):
        a, b = wa_ref[0, s], wb_ref[0, s]
        for j in range(SSM_TILE_GROUPS // 2):
            even, odd = (row_g == 2 * j) & lo, (row_g == 2 * j + 1) & jnp.logical_not(lo)
            wout_s[s * gh:(s + 1) * gh, j * 2 * p:(j + 1) * 2 * p] = (
                jnp.where(even, a, jnp.where(odd, b, zero)))
            wout_s[s * gh:(s + 1) * gh, (SSM_TILE_GROUPS // 2 + j) * 2 * p:(SSM_TILE_GROUPS // 2 + j + 1) * 2 * p] = (
                jnp.where(even, b, jnp.where(odd, a, zero)))
    wout = wout_s[...]
    st = jnp.dot(_chunk_lanes(u_ref).astype(BF16), wout, preferred_element_type=F32)
    lead = jnp.dot(_chunk_lanes(lead_ref).astype(BF16), wout, preferred_element_type=F32)
    nc, half = st.shape[0], st.shape[1] // 2
    xr, xi = st[:, :half], st[:, half:]
    mr, mi = lead[-1:, :half], lead[-1:, half:]
    row = lax.broadcasted_iota(jnp.int32, (nc, half), 0)
    ar, ai = apr_ref[0, 0:1, :], api_ref[0, 0:1, :]
    xr = xr + jnp.where(row == 0, ar * mr - ai * mi, 0.0)
    xi = xi + jnp.where(row == 0, ar * mi + ai * mr, 0.0)

    def shift(x, d):
        return jnp.where(row >= d, pltpu.roll(x, d, 0), 0.0)

    for j in range(n_steps):
        ar, ai = apr_ref[0, j:j + 1, :], api_ref[0, j:j + 1, :]
        sr, si = shift(xr, 1 << j), shift(xi, 1 << j)
        xr, xi = xr + (ar * sr - ai * si), xi + (ar * si + ai * sr)
    pr = jnp.where(row == 0, mr, shift(xr, 1))
    pi = jnp.where(row == 0, mi, shift(xi, 1))
    prev_ref[0] = jnp.concatenate([pr, pi], axis=1).astype(BF16)


def _ssm_out_kernel(u_ref, prev_ref, lag_ref, cin_ref, spread_ref, d_ref, y_ref, toep_s, wcin_s):
    gh, c = SSM_TILE_GROUPS * SSM_GROUP, SSM_CHUNK

    @pl.when(pl.program_id(1) == 0)
    def _():
        spread = spread_ref[...]
        diag = (lax.broadcasted_iota(jnp.int32, (gh, gh), 0) // SSM_GROUP
                == lax.broadcasted_iota(jnp.int32, (gh, gh), 1) // SSM_GROUP)
        zero = jnp.zeros((gh, gh), BF16)
        blocks = [jnp.where(diag, jnp.dot(lag_ref[0, k], spread, preferred_element_type=F32), 0.0).astype(BF16)
                  for k in range(c)]
        for s in range(c):
            for i in range(c):
                toep_s[s * gh:(s + 1) * gh, i * gh:(i + 1) * gh] = blocks[i - s] if i >= s else zero
        rows = wcin_s.shape[0]
        row_g = (lax.broadcasted_iota(jnp.int32, (rows, gh), 0) % (rows // 2)) // SSM_STATE
        own = row_g == lax.broadcasted_iota(jnp.int32, (rows, gh), 1) // SSM_GROUP
        for i in range(c):
            wcin_s[:, i * gh:(i + 1) * gh] = jnp.where(
                own, jnp.dot(cin_ref[0, i], spread, preferred_element_type=F32), 0.0).astype(BF16)

    us = _chunk_steps(u_ref)
    cb = us[0].shape[0]
    u = jnp.concatenate(us, axis=1).astype(BF16)
    y = (jnp.dot(u, toep_s[...], preferred_element_type=F32)
         + jnp.dot(prev_ref[0], wcin_s[...], preferred_element_type=F32))
    for s in range(SSM_CHUNK):
        ys = y[:, s * 128:(s + 1) * 128] + d_ref[0] * us[s]
        y_ref[pl.ds(s, cb, stride=SSM_CHUNK), :] = 0.5 * ys * (1.0 + lax.erf(ys * (0.5 ** 0.5)))


def _ssm(u, u_lead, tables):
    lagc, wout_a, wout_b, cinc, spread, apr, api, dsk, n_steps = tables
    t = u.shape[0]
    nc = t // SSM_CHUNK
    cb = _tile(nc, 256) if nc % 128 == 0 else nc
    lanes = 128
    k = SSM_CHUNK * lanes
    states = 2 * SSM_TILE_GROUPS * SSM_STATE
    prev = pl.pallas_call(
        functools.partial(_ssm_state_kernel, n_steps=n_steps),
        grid=(SSM_TILES,),
        in_specs=[pl.BlockSpec((t, lanes), lambda g: (0, g)),
                  pl.BlockSpec((BLOCK, lanes), lambda g: (0, g)),
                  pl.BlockSpec((1, SSM_CHUNK, lanes, 2 * SSM_STATE), lambda g: (g, 0, 0, 0)),
                  pl.BlockSpec((1, SSM_CHUNK, lanes, 2 * SSM_STATE), lambda g: (g, 0, 0, 0)),
                  pl.BlockSpec((1, n_steps, states // 2), lambda g: (g, 0, 0)),
                  pl.BlockSpec((1, n_steps, states // 2), lambda g: (g, 0, 0))],
        out_specs=pl.BlockSpec((1, nc, states), lambda g: (g, 0, 0)),
        out_shape=jax.ShapeDtypeStruct((SSM_TILES, nc, states), BF16),
        scratch_shapes=[pltpu.VMEM((k, states), BF16)],
        compiler_params=_params(("parallel",), 48),
        name="s5_states",
    )(u, u_lead, wout_a, wout_b, apr, api)
    return pl.pallas_call(
        _ssm_out_kernel,
        grid=(SSM_TILES, nc // cb),
        in_specs=[pl.BlockSpec((cb * SSM_CHUNK, lanes), lambda g, c: (c, g)),
                  pl.BlockSpec((1, cb, states), lambda g, c: (g, c, 0)),
                  pl.BlockSpec((1, SSM_CHUNK, lanes, SSM_GROUP), lambda g, c: (g, 0, 0, 0)),
                  pl.BlockSpec((1, SSM_CHUNK, states, SSM_GROUP), lambda g, c: (g, 0, 0, 0)),
                  pl.BlockSpec((SSM_GROUP, lanes), lambda g, c: (0, 0)),
                  pl.BlockSpec((1, 1, lanes), lambda g, c: (g, 0, 0))],
        out_specs=pl.BlockSpec((cb * SSM_CHUNK, lanes), lambda g, c: (c, g)),
        out_shape=jax.ShapeDtypeStruct((t, SSM_WIDTH), F32),
        scratch_shapes=[pltpu.VMEM((k, k), BF16), pltpu.VMEM((states, k), BF16)],
        compiler_params=_params(("parallel", "arbitrary"), 48),
        name="s5_outputs",
    )(u, prev, lagc, cinc, spread, dsk)


def _glu_kernel(y_ref, w_ref, o_ref):
    y = y_ref[...]
    gate = jax.nn.sigmoid(jnp.dot(y.astype(BF16), w_ref[...].astype(BF16), preferred_element_type=F32))
    o_ref[...] = (y * gate).astype(BF16)


def _glu(y, w):
    m, d = y.shape
    tm = _tile(m, 1024)
    return pl.pallas_call(
        _glu_kernel,
        grid=(m // tm,),
        in_specs=[pl.BlockSpec((tm, d), lambda i: (i, 0)),
                  pl.BlockSpec((d, d), lambda i: (0, 0))],
        out_specs=pl.BlockSpec((tm, d), lambda i: (i, 0)),
        out_shape=jax.ShapeDtypeStruct((m, d), BF16),
        compiler_params=_params(("parallel",), 40),
        name="ssm_glu",
    )(y, w)


def _mix_kernel(a_ref, y_ref, wa_ref, ws_ref, ga_ref, gs_ref, o_ref):
    attn_d = jnp.dot(a_ref[...], wa_ref[...].astype(BF16), preferred_element_type=F32)
    ssm_d = jnp.dot(y_ref[...], ws_ref[...].astype(BF16), preferred_element_type=F32)
    mix = (jax.nn.sigmoid(ga_ref[...].astype(F32)) * attn_d
           + jax.nn.sigmoid(gs_ref[...].astype(F32)) * ssm_d)
    o_ref[...] = mix.astype(BF16)


def _mix(attn, y2, w_attn_o, w_ssm_o, z):
    m = attn.shape[0]
    tm, tn = _tile(m, 1024), 512
    ga_blk, gs_blk = COL_GA // tn, COL_GS // tn
    return pl.pallas_call(
        _mix_kernel,
        grid=(m // tm, D_MODEL // tn),
        in_specs=[pl.BlockSpec((tm, ATTN_WIDTH), lambda i, j: (i, 0)),
                  pl.BlockSpec((tm, SSM_WIDTH), lambda i, j: (i, 0)),
                  pl.BlockSpec((ATTN_WIDTH, tn), lambda i, j: (0, j)),
                  pl.BlockSpec((SSM_WIDTH, tn), lambda i, j: (0, j)),
                  pl.BlockSpec((tm, tn), lambda i, j: (i, j + ga_blk)),
                  pl.BlockSpec((tm, tn), lambda i, j: (i, j + gs_blk))],
        out_specs=pl.BlockSpec((tm, tn), lambda i, j: (i, j)),
        out_shape=jax.ShapeDtypeStruct((m, D_MODEL), BF16),
        compiler_params=_params(("parallel", "arbitrary"), 48),
        name="gated_merge",
    )(attn, y2, w_attn_o, w_ssm_o, z, z)


def _resid_kernel(a_ref, w_ref, h_ref, o_ref):
    o_ref[...] = DEEPNORM_ALPHA * h_ref[...] + jnp.dot(
        a_ref[...], w_ref[...].astype(BF16), preferred_element_type=F32)


def _out_proj(mix, w_out, h0):
    m = mix.shape[0]
    tm, tn = _tile(m, 1024), 512
    return pl.pallas_call(
        _resid_kernel,
        grid=(m // tm, D_MODEL // tn),
        in_specs=[pl.BlockSpec((tm, D_MODEL), lambda i, j: (i, 0)),
                  pl.BlockSpec((D_MODEL, tn), lambda i, j: (0, j)),
                  pl.BlockSpec((tm, tn), lambda i, j: (i, j))],
        out_specs=pl.BlockSpec((tm, tn), lambda i, j: (i, j)),
        out_shape=jax.ShapeDtypeStruct((m, D_MODEL), F32),
        compiler_params=_params(("parallel", "arbitrary"), 56),
        name="out_proj",
    )(mix, w_out, h0)


def _token_mixer(x, meta_tokens, ln0_g, ln0_b, w_in, attn_sinks, w_attn_o, ssm_tables_args,
                 w_glu, w_ssm_o, w_out, ln1_g, ln1_b):
    t = x.shape[0]
    lead = jnp.concatenate([jnp.zeros((N_PAD, D_MODEL), F32), meta_tokens.astype(F32)], axis=0)
    h0, h0b = _layer_norm(x, ln0_g, ln0_b)
    _, leadb = _layer_norm(lead, ln0_g, ln0_b)

    z = _project(h0b, w_in, 0, IN_WIDTH)
    z_lead = _project(leadb, w_in, COL_KV, 2 * KV_WIDTH + SSM_WIDTH)

    attn = _attention(z, z_lead, attn_sinks)

    lead_valid = (jnp.arange(BLOCK) >= N_PAD)[:, None]
    u_lead = jnp.where(lead_valid, z_lead[:, 2 * KV_WIDTH:].astype(F32), 0.0)
    tables = _ssm_tables(*ssm_tables_args, n_chunks=t // SSM_CHUNK)
    y = _ssm(z[:, COL_U:COL_GA].astype(F32), u_lead, tables)
    y2 = _glu(y, w_glu)

    mix = _mix(attn, y2, w_attn_o, w_ssm_o, z)
    r = _out_proj(mix, w_out, h0)
    return _layer_norm(r, ln1_g, ln1_b)


def _router_kernel(h_ref, wr_ref, bias_ref, idx_ref, wt_ref):
    tm = h_ref.shape[0]
    ng, ne = N_EXPERT_GROUPS, EXPERTS_PER_GROUP
    logits = lax.dot_general(wr_ref[...], h_ref[...], (((1,), (1,)), ((), ())),
                             precision=lax.Precision.HIGHEST, preferred_element_type=F32)
    scores = jax.nn.sigmoid(logits)
    sel = (scores + bias_ref[...]).reshape(ng, ne, tm)
    scores = scores.reshape(ng, ne, tm)
    e_in = lax.broadcasted_iota(jnp.int32, (ng, ne, tm), 1)
    e_id = lax.broadcasted_iota(jnp.int32, (ng, ne, tm), 0) * ne + e_in
    neg = -jnp.inf

    top1 = jnp.max(sel, axis=1, keepdims=True)
    first = jnp.min(jnp.where(sel == top1, e_in, ne), axis=1, keepdims=True)
    top2 = jnp.max(jnp.where(e_in == first, neg, sel), axis=1, keepdims=True)
    grp = (top1 + top2)[:, 0, :]

    g_id = lax.broadcasted_iota(jnp.int32, (ng, tm), 0)
    chosen = jnp.zeros((ng, tm), jnp.bool_)
    for _ in range(TOPK_GROUPS):
        best = jnp.max(grp, axis=0, keepdims=True)
        pick = g_id == jnp.min(jnp.where(grp == best, g_id, ng), axis=0, keepdims=True)
        chosen = chosen | pick
        grp = jnp.where(pick, neg, grp)

    cur = jnp.where(chosen[:, None, :], sel, NEG_INF)
    ids, wts = [], []
    for _ in range(TOP_K):
        best = jnp.max(jnp.max(cur, axis=1, keepdims=True), axis=0, keepdims=True)
        cand = jnp.where(cur == best, e_id, N_EXPERTS)
        win = jnp.min(jnp.min(cand, axis=1, keepdims=True), axis=0, keepdims=True)
        pick = e_id == win
        wsel = jnp.where(pick, scores, 0.0)
        ids.append(win[0])
        wts.append(jnp.sum(jnp.sum(wsel, axis=1, keepdims=True), axis=0, keepdims=True)[0])
        cur = jnp.where(pick, neg, cur)
    w = jnp.concatenate(wts, axis=0)
    idx_ref[...] = jnp.concatenate(ids, axis=0)
    wt_ref[...] = w / jnp.sum(w, axis=0, keepdims=True) * ROUTED_SCALE


def _router(h1, w_router, router_bias):
    t, d = h1.shape
    tm = _tile(t, 512)
    return pl.pallas_call(
        _router_kernel,
        grid=(t // tm,),
        in_specs=[pl.BlockSpec((tm, d), lambda i: (i, 0)),
                  pl.BlockSpec((N_EXPERTS, d), lambda i: (0, 0)),
                  pl.BlockSpec((N_EXPERTS, 1), lambda i: (0, 0))],
        out_specs=[pl.BlockSpec((TOP_K, tm), lambda i: (0, i)),
                   pl.BlockSpec((TOP_K, tm), lambda i: (0, i))],
        out_shape=[jax.ShapeDtypeStruct((TOP_K, t), jnp.int32),
                   jax.ShapeDtypeStruct((TOP_K, t), F32)],
        compiler_params=_params(("parallel",), 40),
        name="router",
    )(h1, w_router.T, router_bias.reshape(N_EXPERTS, 1))


def _dispatch(idx_t):
    t = idx_t.shape[1]
    n_assign = TOP_K * t
    n_blocks = -(-n_assign // ROW_BLOCK) + N_EXPERTS
    n_rows = n_blocks * ROW_BLOCK
    e_flat = idx_t.reshape(-1)
    onehot = (e_flat[:, None] == jnp.arange(N_EXPERTS, dtype=jnp.int32)[None, :]).astype(jnp.int32)
    csum = jnp.cumsum(onehot, axis=0)
    rank = jnp.take_along_axis(csum, e_flat[:, None], axis=1)[:, 0] - 1
    counts = csum[-1]
    padded = (counts + ROW_BLOCK - 1) // ROW_BLOCK * ROW_BLOCK
    ends = jnp.cumsum(padded)
    starts = ends - padded
    dest = starts[e_flat] + rank
    tok_flat = jnp.tile(jnp.arange(t, dtype=jnp.int32), TOP_K)
    row_tok = jnp.zeros((n_rows,), jnp.int32).at[dest].set(tok_flat)
    block_start = jnp.arange(n_blocks, dtype=jnp.int32) * ROW_BLOCK
    block_expert = jnp.minimum(
        jnp.sum((ends[None, :] <= block_start[:, None]).astype(jnp.int32), axis=1), N_EXPERTS - 1)
    n_used = (ends[-1] // ROW_BLOCK).astype(jnp.int32).reshape(1)
    has = counts > 0
    cand = jnp.where(has[None, :] & (jnp.arange(N_EXPERTS)[None, :] > jnp.arange(N_EXPERTS)[:, None]),
                     jnp.arange(N_EXPERTS)[None, :], N_EXPERTS)
    nxt = jnp.min(cand, axis=1)
    next_expert = jnp.where(nxt < N_EXPERTS, nxt, -1).astype(jnp.int32)
    return (row_tok.reshape(n_blocks, 1, ROW_BLOCK), block_expert, n_used, next_expert,
            dest.reshape(TOP_K, t))


def _gather_rows(src_hbm, idx_smem, dst, sem, n):
    def body(i, carry):
        pltpu.make_async_copy(src_hbm.at[pl.ds(idx_smem[0, i], 1)], dst.at[pl.ds(i, 1)], sem).start()
        return carry
    lax.fori_loop(0, n, body, 0, unroll=8)


def _wait_rows(src_hbm, dst, sem, n):
    pltpu.make_async_copy(src_hbm.at[pl.ds(0, n)], dst, sem).wait()


def _expert_up_kernel(be_ref, nu_ref, nx_ref, tok_hbm, h_hbm, w1_hbm, w3_hbm, o_ref,
                      idx_smem, xbuf, wstage, wcast, isem, xsem, wsem):
    b = pl.program_id(0)
    n_used = nu_ref[0]
    e = be_ref[b]
    slot = b % 2

    def idx_copy(blk, s):
        return pltpu.make_async_copy(tok_hbm.at[blk], idx_smem.at[s], isem.at[s])

    def weight_copies(ex):
        return (pltpu.make_async_copy(w1_hbm.at[ex], wstage.at[0], wsem.at[0]),
                pltpu.make_async_copy(w3_hbm.at[ex], wstage.at[1], wsem.at[1]))

    @pl.when(b == 0)
    def _():
        for c in weight_copies(e):
            c.start()
        idx_copy(0, 0).start()
        idx_copy(0, 0).wait()
        _gather_rows(h_hbm, idx_smem.at[0], xbuf.at[0], xsem.at[0], ROW_BLOCK)

        @pl.when(n_used > 1)
        def _():
            idx_copy(1, 1).start()

    @pl.when(b + 1 < n_used)
    def _():
        idx_copy(b + 1, 1 - slot).wait()
        _gather_rows(h_hbm, idx_smem.at[1 - slot], xbuf.at[1 - slot], xsem.at[1 - slot], ROW_BLOCK)

        @pl.when(b + 2 < n_used)
        def _():
            idx_copy(b + 2, slot).start()

    @pl.when(b < n_used)
    def _():
        first_of_expert = jnp.logical_or(b == 0, be_ref[jnp.maximum(b - 1, 0)] != e)

        @pl.when(first_of_expert)
        def _():
            for c in weight_copies(e):
                c.wait()
            wcast[0] = wstage[0].astype(BF16)
            wcast[1] = wstage[1].astype(BF16)
            nxt = nx_ref[e]

            @pl.when(nxt >= 0)
            def _():
                for c in weight_copies(nxt):
                    c.start()

        _wait_rows(h_hbm, xbuf.at[slot], xsem.at[slot], ROW_BLOCK)
        x = xbuf[slot].astype(BF16)
        a = jnp.dot(x, wcast[0], preferred_element_type=F32)
        g = jnp.dot(x, wcast[1], preferred_element_type=F32)
        o_ref[...] = (a * jax.nn.sigmoid(a) * g).astype(BF16)

    @pl.when(b >= n_used)
    def _():
        o_ref[...] = jnp.zeros_like(o_ref)


def _expert_up(h1, row_tok, block_expert, n_used, next_expert, w1, w3):
    n_blocks = row_tok.shape[0]
    d = w1.shape[1]
    grid_spec = pltpu.PrefetchScalarGridSpec(
        num_scalar_prefetch=3,
        grid=(n_blocks,),
        in_specs=[pl.BlockSpec(memory_space=pl.ANY)] * 4,
        out_specs=pl.BlockSpec((ROW_BLOCK, EXPERT_DIM), lambda b, *_: (b, 0)),
        scratch_shapes=[pltpu.SMEM((2, 1, ROW_BLOCK), jnp.int32),
                        pltpu.VMEM((2, ROW_BLOCK, d), F32),
                        pltpu.VMEM((2, d, EXPERT_DIM), F32),
                        pltpu.VMEM((2, d, EXPERT_DIM), BF16),
                        pltpu.SemaphoreType.DMA((2,)),
                        pltpu.SemaphoreType.DMA((2,)),
                        pltpu.SemaphoreType.DMA((2,))])
    return pl.pallas_call(
        _expert_up_kernel,
        grid_spec=grid_spec,
        out_shape=jax.ShapeDtypeStruct((n_blocks * ROW_BLOCK, EXPERT_DIM), BF16),
        compiler_params=_params(("arbitrary",), 48),
        name="expert_up",
    )(block_expert, n_used, next_expert, row_tok, h1, w1, w3)


def _expert_down_kernel(be_ref, nu_ref, nx_ref, a_ref, w2_hbm, o_ref, wstage, wcast, wsem):
    b = pl.program_id(0)
    n_used = nu_ref[0]
    e = be_ref[b]

    def weight_copy(ex):
        return pltpu.make_async_copy(w2_hbm.at[ex], wstage, wsem)

    @pl.when(b == 0)
    def _():
        weight_copy(e).start()

    @pl.when(b < n_used)
    def _():
        first_of_expert = jnp.logical_or(b == 0, be_ref[jnp.maximum(b - 1, 0)] != e)

        @pl.when(first_of_expert)
        def _():
            weight_copy(e).wait()
            wcast[...] = wstage[...].astype(BF16)
            nxt = nx_ref[e]

            @pl.when(nxt >= 0)
            def _():
                weight_copy(nxt).start()

        o_ref[...] = jnp.dot(a_ref[...], wcast[...], preferred_element_type=F32)

    @pl.when(b >= n_used)
    def _():
        o_ref[...] = jnp.zeros_like(o_ref)


def _expert_down(act, block_expert, n_used, next_expert, w2):
    n_rows = act.shape[0]
    n_blocks = n_rows // ROW_BLOCK
    d = w2.shape[2]
    grid_spec = pltpu.PrefetchScalarGridSpec(
        num_scalar_prefetch=3,
        grid=(n_blocks,),
        in_specs=[pl.BlockSpec((ROW_BLOCK, EXPERT_DIM), lambda b, *_: (b, 0)),
                  pl.BlockSpec(memory_space=pl.ANY)],
        out_specs=pl.BlockSpec((ROW_BLOCK, d), lambda b, *_: (b, 0)),
        scratch_shapes=[pltpu.VMEM((EXPERT_DIM, d), F32),
                        pltpu.VMEM((EXPERT_DIM, d), BF16),
                        pltpu.SemaphoreType.DMA(())])
    return pl.pallas_call(
        _expert_down_kernel,
        grid_spec=grid_spec,
        out_shape=jax.ShapeDtypeStruct((n_rows, d), F32),
        compiler_params=_params(("arbitrary",), 40),
        name="expert_down",
    )(block_expert, n_used, next_expert, act, w2)


def _shared_kernel(h_ref, w1_ref, w3_ref, w2_ref, o_ref):
    x = h_ref[...]
    a = jnp.dot(x, w1_ref[...], preferred_element_type=F32)
    g = jnp.dot(x, w3_ref[...], preferred_element_type=F32)
    act = (a * jax.nn.sigmoid(a) * g).astype(BF16)
    o_ref[...] = jnp.dot(act, w2_ref[...], preferred_element_type=F32)


def _shared_expert(h1b, w1, w3, w2):
    t, d = h1b.shape
    tm = _tile(t, 512)
    return pl.pallas_call(
        _shared_kernel,
        grid=(t // tm,),
        in_specs=[pl.BlockSpec((tm, d), lambda i: (i, 0)),
                  pl.BlockSpec((d, SHARED_DIM), lambda i: (0, 0)),
                  pl.BlockSpec((d, SHARED_DIM), lambda i: (0, 0)),
                  pl.BlockSpec((SHARED_DIM, d), lambda i: (0, 0))],
        out_specs=pl.BlockSpec((tm, d), lambda i: (i, 0)),
        out_shape=jax.ShapeDtypeStruct((t, d), F32),
        compiler_params=_params(("parallel",), 56),
        name="shared_expert",
    )(h1b, w1.astype(BF16), w3.astype(BF16), w2.astype(BF16))


COMBINE_TOKENS = 64


def _combine_kernel(dest_hbm, y_hbm, h_ref, s_ref, w_ref, g_ref, b_ref, o_ref,
                    idx_smem, ybuf, isem, ysem):
    i = pl.program_id(0)
    n = pl.num_programs(0)
    slot = i % 2
    rows = TOP_K * COMBINE_TOKENS

    def idx_copy(blk, s):
        return pltpu.make_async_copy(dest_hbm.at[blk], idx_smem.at[s], isem.at[s])

    @pl.when(i == 0)
    def _():
        idx_copy(0, 0).start()
        idx_copy(0, 0).wait()
        _gather_rows(y_hbm, idx_smem.at[0], ybuf.at[0], ysem.at[0], rows)

        @pl.when(n > 1)
        def _():
            idx_copy(1, 1).start()

    @pl.when(i + 1 < n)
    def _():
        idx_copy(i + 1, 1 - slot).wait()
        _gather_rows(y_hbm, idx_smem.at[1 - slot], ybuf.at[1 - slot], ysem.at[1 - slot], rows)

        @pl.when(i + 2 < n)
        def _():
            idx_copy(i + 2, slot).start()

    _wait_rows(y_hbm, ybuf.at[slot], ysem.at[slot], rows)
    acc = DEEPNORM_ALPHA * h_ref[...] + s_ref[...]
    for j in range(TOP_K):
        acc = acc + w_ref[:, j:j + 1] * ybuf[slot, j * COMBINE_TOKENS:(j + 1) * COMBINE_TOKENS, :]
    o_ref[...] = _ln_rows(acc, g_ref[...], b_ref[...])


def _combine(yb, dest, w_t, h1, shared, g, b):
    t, d = h1.shape
    tm = COMBINE_TOKENS
    n_tiles = t // tm
    dest_tiles = dest.reshape(TOP_K, n_tiles, tm).transpose(1, 0, 2).reshape(n_tiles, 1, TOP_K * tm)
    return pl.pallas_call(
        _combine_kernel,
        grid=(n_tiles,),
        in_specs=[pl.BlockSpec(memory_space=pl.ANY),
                  pl.BlockSpec(memory_space=pl.ANY),
                  pl.BlockSpec((tm, d), lambda i: (i, 0)),
                  pl.BlockSpec((tm, d), lambda i: (i, 0)),
                  pl.BlockSpec((tm, TOP_K), lambda i: (i, 0)),
                  pl.BlockSpec((1, d), lambda i: (0, 0)),
                  pl.BlockSpec((1, d), lambda i: (0, 0))],
        out_specs=pl.BlockSpec((tm, d), lambda i: (i, 0)),
        out_shape=jax.ShapeDtypeStruct((t, d), F32),
        scratch_shapes=[pltpu.SMEM((2, 1, TOP_K * tm), jnp.int32),
                        pltpu.VMEM((2, TOP_K * tm, d), F32),
                        pltpu.SemaphoreType.DMA((2,)),
                        pltpu.SemaphoreType.DMA((2,))],
        compiler_params=_params(("arbitrary",), 40),
        name="combine_norm",
    )(dest_tiles, yb, h1, shared, w_t.T, g.reshape(1, d), b.reshape(1, d))


def _channel_mixer(h1, h1b, w_router, router_bias, expert_w1, expert_w3, expert_w2,
                   shared_w1, shared_w3, shared_w2, ln2_g, ln2_b):
    idx_t, w_t = _router(h1, w_router, router_bias)
    row_tok, block_expert, n_used, next_expert, dest = _dispatch(idx_t)
    shared = _shared_expert(h1b, shared_w1, shared_w3, shared_w2)
    act = _expert_up(h1, row_tok, block_expert, n_used, next_expert, expert_w1, expert_w3)
    yb = _expert_down(act, block_expert, n_used, next_expert, expert_w2)
    return _combine(yb, dest, w_t, h1, shared, ln2_g, ln2_b)


def kernel(x, meta_tokens, ln0_g, ln0_b, w_in, attn_sinks, w_attn_o, ssm_lam_re, ssm_lam_im, ssm_log_dt,
           ssm_b_re, ssm_b_im, ssm_c_re, ssm_c_im, ssm_d, w_glu, w_ssm_o, w_out, ln1_g, ln1_b, w_router,
           router_bias, expert_w1, expert_w3, expert_w2, shared_w1, shared_w3, shared_w2, ln2_g, ln2_b):
    assert x.shape[0] == 1 and w_in.shape[0] == 1
    ssm_args = (ssm_lam_re[0], ssm_lam_im[0], ssm_log_dt[0], ssm_b_re[0], ssm_b_im[0],
                ssm_c_re[0], ssm_c_im[0], ssm_d[0])
    h1, h1b = _token_mixer(x[0], meta_tokens, ln0_g, ln0_b, w_in[0], attn_sinks[0], w_attn_o[0],
                           ssm_args, w_glu[0], w_ssm_o[0], w_out[0], ln1_g[0], ln1_b[0])
    out = _channel_mixer(h1, h1b, w_router[0], router_bias[0], expert_w1[0], expert_w3[0],
                         expert_w2[0], shared_w1[0], shared_w3[0], shared_w2[0], ln2_g[0], ln2_b[0])
    return out[None]
```

```python
import functools
import math

import jax
import jax.numpy as jnp
import numpy as np
from jax import lax
from jax.experimental import pallas as pl
from jax.experimental.pallas import tpu as pltpu

F32 = jnp.float32
BF16 = jnp.bfloat16

D_MODEL = 4096
N_META = 16
BLOCK = 128
N_PAD = BLOCK - N_META

N_HEADS = 32
N_KV_HEADS = 4
HEAD_DIM = 64
HEADS_PER_KV = N_HEADS // N_KV_HEADS
ATTN_WIDTH = N_HEADS * HEAD_DIM
KV_WIDTH = N_KV_HEADS * HEAD_DIM

SSM_WIDTH = 1024
SSM_GROUP = 16
N_SSM_GROUPS = SSM_WIDTH // SSM_GROUP
SSM_STATE = 64
SSM_CHUNK = 16
SSM_TILE_GROUPS = 128 // SSM_GROUP
SSM_TILES = N_SSM_GROUPS // SSM_TILE_GROUPS

N_EXPERTS = 64
TOP_K = 8
N_EXPERT_GROUPS = 8
EXPERTS_PER_GROUP = N_EXPERTS // N_EXPERT_GROUPS
TOPK_GROUPS = 4
EXPERT_DIM = 512
SHARED_DIM = 512
ROUTED_SCALE = 2.5
ROW_BLOCK = 512

IN_WIDTH = ATTN_WIDTH + 2 * KV_WIDTH + SSM_WIDTH + 2 * D_MODEL
COL_KV = ATTN_WIDTH
COL_U = ATTN_WIDTH + 2 * KV_WIDTH
COL_GA = COL_U + SSM_WIDTH
COL_GS = COL_GA + D_MODEL

DEEPNORM_ALPHA = 2.0 ** 0.25
LN_EPS = 1e-5
NEG_INF = -1e30

MIB = 1024 * 1024


def _params(semantics, vmem_mib):
    return pltpu.CompilerParams(dimension_semantics=semantics, vmem_limit_bytes=vmem_mib * MIB)


def _tile(n, pref):
    t = min(n, pref)
    while n % t:
        t -= 128
    return t


def _ln_rows(x, g, b):
    mu = jnp.mean(x, axis=-1, keepdims=True)
    xc = x - mu
    var = jnp.mean(xc * xc, axis=-1, keepdims=True)
    return xc * lax.rsqrt(var + LN_EPS) * g + b


def _ln_kernel(x_ref, g_ref, b_ref, o_ref, ob_ref):
    y = _ln_rows(x_ref[...], g_ref[...], b_ref[...])
    o_ref[...] = y
    ob_ref[...] = y.astype(BF16)


def _layer_norm(x, g, b):
    m, d = x.shape
    tm = _tile(m, 256)
    out_specs = [pl.BlockSpec((tm, d), lambda i: (i, 0)), pl.BlockSpec((tm, d), lambda i: (i, 0))]
    out_shape = [jax.ShapeDtypeStruct((m, d), F32), jax.ShapeDtypeStruct((m, d), BF16)]
    return pl.pallas_call(
        _ln_kernel,
        grid=(m // tm,),
        in_specs=[pl.BlockSpec((tm, d), lambda i: (i, 0)),
                  pl.BlockSpec((1, d), lambda i: (0, 0)),
                  pl.BlockSpec((1, d), lambda i: (0, 0))],
        out_specs=out_specs,
        out_shape=out_shape,
        compiler_params=_params(("parallel",), 40),
        name="layer_norm",
    )(x, g.reshape(1, d), b.reshape(1, d))


def _proj_kernel(a_ref, w_ref, o_ref):
    o_ref[...] = jnp.dot(a_ref[...], w_ref[...].astype(BF16),
                         preferred_element_type=F32).astype(o_ref.dtype)


def _project(a, w, col0, n_cols, tm_pref=1024, tn=512):
    m, k = a.shape
    tm = _tile(m, tm_pref)
    off = col0 // tn
    return pl.pallas_call(
        _proj_kernel,
        grid=(m // tm, n_cols // tn),
        in_specs=[pl.BlockSpec((tm, k), lambda i, j: (i, 0)),
                  pl.BlockSpec((k, tn), lambda i, j: (0, j + off))],
        out_specs=pl.BlockSpec((tm, tn), lambda i, j: (i, j)),
        out_shape=jax.ShapeDtypeStruct((m, n_cols), BF16),
        compiler_params=_params(("parallel", "arbitrary"), 56),
        name="in_proj",
    )(a, w)


def _attn_kernel(sink_ref, q_ref, kvc_ref, kvp_ref, kvm_ref, o_ref):
    first = pl.program_id(0) == 0
    kv_all = jnp.concatenate([kvp_ref[...], kvc_ref[...], kvm_ref[N_PAD:BLOCK, :]], axis=0)
    n_keys = 2 * BLOCK + N_META
    rows = HEADS_PER_KV * BLOCK

    lane = lax.broadcasted_iota(jnp.int32, (BLOCK, BLOCK), 1)
    lo_q = lane < HEAD_DIM
    lane_kv = lax.broadcasted_iota(jnp.int32, (n_keys, BLOCK), 1)
    lo_kv = lane_kv < HEAD_DIM

    qi = lax.broadcasted_iota(jnp.int32, (rows, n_keys), 0) & (BLOCK - 1)
    kj = lax.broadcasted_iota(jnp.int32, (rows, n_keys), 1)
    in_window = (kj > qi) & (kj <= qi + BLOCK) & ((kj >= BLOCK) | jnp.logical_not(first))
    visible = in_window | (kj >= 2 * BLOCK)
    local_head = lax.broadcasted_iota(jnp.int32, (rows, 1), 0) // BLOCK

    def dup_head(tile, odd):
        t = tile.astype(F32)
        r = pltpu.roll(t, HEAD_DIM, 1)
        keep = jnp.logical_not(lo_kv) if odd else lo_kv
        return jnp.where(keep, t, r).astype(BF16)

    for kh in range(N_KV_HEADS):
        t0 = (kh // 2) * BLOCK
        k2 = dup_head(kv_all[:, t0:t0 + BLOCK], kh % 2)
        v2 = dup_head(kv_all[:, KV_WIDTH + t0:KV_WIDTH + t0 + BLOCK], kh % 2)
        pieces = []
        for p in range(HEADS_PER_KV // 2):
            c0 = kh * HEADS_PER_KV * HEAD_DIM + p * BLOCK
            qp = q_ref[:, c0:c0 + BLOCK]
            zero = jnp.zeros_like(qp)
            pieces.append(jnp.where(lo_q, qp, zero))
            pieces.append(jnp.where(lo_q, zero, qp))
        qs = jnp.concatenate(pieces, axis=0)
        s = lax.dot_general(qs, k2, (((1,), (1,)), ((), ())), preferred_element_type=F32)
        s = jnp.where(visible, s * (HEAD_DIM ** -0.5), NEG_INF)
        sink = jnp.zeros((rows, 1), F32)
        for t in range(HEADS_PER_KV):
            sink = jnp.where(local_head == t, sink_ref[kh * HEADS_PER_KV + t], sink)
        m = jnp.maximum(jnp.max(s, axis=-1, keepdims=True), sink)
        e = jnp.exp(s - m)
        den = jnp.sum(e, axis=-1, keepdims=True) + jnp.exp(sink - m)
        o2 = jnp.dot(e.astype(BF16), v2, preferred_element_type=F32) / den
        for p in range(HEADS_PER_KV // 2):
            c0 = kh * HEADS_PER_KV * HEAD_DIM + p * BLOCK
            r0 = 2 * p * BLOCK
            o_ref[:, c0:c0 + BLOCK] = jnp.where(
                lo_q, o2[r0:r0 + BLOCK], o2[r0 + BLOCK:r0 + 2 * BLOCK]).astype(BF16)


def _attention(z, z_lead, sinks):
    t = z.shape[0]
    kv_blk = COL_KV // (2 * KV_WIDTH)
    return pl.pallas_call(
        _attn_kernel,
        grid=(t // BLOCK,),
        in_specs=[pl.BlockSpec(memory_space=pltpu.SMEM),
                  pl.BlockSpec((BLOCK, ATTN_WIDTH), lambda n: (n, 0)),
                  pl.BlockSpec((BLOCK, 2 * KV_WIDTH), lambda n: (n, kv_blk)),
                  pl.BlockSpec((BLOCK, 2 * KV_WIDTH), lambda n: (jnp.maximum(n - 1, 0), kv_blk)),
                  pl.BlockSpec((BLOCK, 2 * KV_WIDTH), lambda n: (0, 0))],
        out_specs=pl.BlockSpec((BLOCK, ATTN_WIDTH), lambda n: (n, 0)),
        out_shape=jax.ShapeDtypeStruct((t, ATTN_WIDTH), BF16),
        compiler_params=_params(("parallel",), 40),
        name="swa_attention",
    )(sinks.astype(F32), z, z, z, z_lead)


def _cmul(ar, ai, br, bi):
    return ar * br - ai * bi, ar * bi + ai * br


def _ssm_tables(lam_re, lam_im, log_dt, b_re, b_im, c_re, c_im, d_skip, n_chunks):
    hi = lax.Precision.HIGHEST
    g, p, h, c = N_SSM_GROUPS, SSM_STATE, SSM_GROUP, SSM_CHUNK
    dt = jnp.exp(log_dt)[:, None]
    mag = jnp.exp(dt * lam_re)
    are = mag * jnp.cos(dt * lam_im)
    aim = mag * jnp.sin(dt * lam_im)
    den = lam_re * lam_re + lam_im * lam_im
    num_re = are - 1.0
    coef_re = (num_re * lam_re + aim * lam_im) / den
    coef_im = (aim * lam_re - num_re * lam_im) / den
    bb_re = coef_re[..., None] * b_re - coef_im[..., None] * b_im
    bb_im = coef_re[..., None] * b_im + coef_im[..., None] * b_re

    pr, pi = [jnp.ones_like(are)], [jnp.zeros_like(are)]
    for _ in range(c):
        r, i = _cmul(pr[-1], pi[-1], are, aim)
        pr.append(r)
        pi.append(i)
    pr, pi = jnp.stack(pr), jnp.stack(pi)

    m_re = pr[:c, :, :, None] * bb_re[None] - pi[:c, :, :, None] * bb_im[None]
    m_im = pr[:c, :, :, None] * bb_im[None] + pi[:c, :, :, None] * bb_re[None]
    lag = (jnp.einsum('ghp,kgpj->kghj', c_re, m_re, precision=hi)
           - jnp.einsum('ghp,kgpj->kghj', c_im, m_im, precision=hi))
    nt, gt = SSM_TILES, SSM_TILE_GROUPS
    lagc = lag.reshape(c, nt, gt, h, h).transpose(1, 0, 2, 4, 3)
    lagc = lagc.reshape(nt, c, gt * h, h)
    rev = np.arange(c - 1, -1, -1)
    wo = jnp.stack([pr[rev][:, :, :, None] * bb_re[None] - pi[rev][:, :, :, None] * bb_im[None],
                    pr[rev][:, :, :, None] * bb_im[None] + pi[rev][:, :, :, None] * bb_re[None]])
    wo = wo.reshape(2, c, nt, gt, p, h).transpose(2, 1, 3, 5, 0, 4)
    wout_a = wo.reshape(nt, c, gt * h, 2 * p)
    wout_b = wo[:, :, :, :, ::-1, :].reshape(nt, c, gt * h, 2 * p)
    cin_re = c_re[None] * pr[1:, :, None, :] - c_im[None] * pi[1:, :, None, :]
    cin_im = -(c_re[None] * pi[1:, :, None, :] + c_im[None] * pr[1:, :, None, :])
    cinc = jnp.stack([cin_re, cin_im]).reshape(2, c, nt, gt, h, p)
    cinc = cinc.transpose(2, 1, 0, 3, 5, 4).reshape(nt, c, 2 * gt * p, h)
    spread = jnp.tile(jnp.eye(h, dtype=BF16), (1, gt))

    n_steps = max(1, math.ceil(math.log2(n_chunks)))
    qr, qi = [pr[c]], [pi[c]]
    for _ in range(n_steps - 1):
        r, i = _cmul(qr[-1], qi[-1], qr[-1], qi[-1])
        qr.append(r)
        qi.append(i)
    apr = jnp.stack(qr, axis=0).reshape(n_steps, nt, gt * p).transpose(1, 0, 2)
    api = jnp.stack(qi, axis=0).reshape(n_steps, nt, gt * p).transpose(1, 0, 2)
    dsk = d_skip.reshape(nt, 1, gt * h)
    return (lagc.astype(BF16), wout_a.astype(BF16), wout_b.astype(BF16), cinc.astype(BF16), spread,
            apr, api, dsk, n_steps)


def _chunk_steps(u_ref):
    n = u_ref.shape[0] // SSM_CHUNK
    return [u_ref[pl.ds(s, n, stride=SSM_CHUNK), :] for s in range(SSM_CHUNK)]


def _chunk_lanes(u_ref):
    return jnp.concatenate(_chunk_steps(u_ref), axis=1)


def _ssm_state_kernel(u_ref, lead_ref, wa_ref, wb_ref, apr_ref, api_ref, prev_ref, wout_s, *, n_steps):
    p, gh = SSM_STATE, SSM_TILE_GROUPS * SSM_GROUP
    row_g = lax.broadcasted_iota(jnp.int32, (gh, 2 * p), 0) // SSM_GROUP
    lo = lax.broadcasted_iota(jnp.int32, (gh, 2 * p), 1) < p
    zero = jnp.zeros((gh, 2 * p), BF16)
    for s in range(SSM_CHUNK):
        a, b = wa_ref[0, s], wb_ref[0, s]
        for j in range(SSM_TILE_GROUPS // 2):
            even, odd = (row_g == 2 * j) & lo, (row_g == 2 * j + 1) & jnp.logical_not(lo)
            wout_s[s * gh:(s + 1) * gh, j * 2 * p:(j + 1) * 2 * p] = (
                jnp.where(even, a, jnp.where(odd, b, zero)))
            wout_s[s * gh:(s + 1) * gh, (SSM_TILE_GROUPS // 2 + j) * 2 * p:(SSM_TILE_GROUPS // 2 + j + 1) * 2 * p] = (
                jnp.where(even, b, jnp.where(odd, a, zero)))
    wout = wout_s[...]
    st = jnp.dot(_chunk_lanes(u_ref).astype(BF16), wout, preferred_element_type=F32)
    lead = jnp.dot(_chunk_lanes(lead_ref).astype(BF16), wout, preferred_element_type=F32)
    nc, half = st.shape[0], st.shape[1] // 2
    xr, xi = st[:, :half], st[:, half:]
    mr, mi = lead[-1:, :half], lead[-1:, half:]
    row = lax.broadcasted_iota(jnp.int32, (nc, half), 0)
    ar, ai = apr_ref[0, 0:1, :], api_ref[0, 0:1, :]
    xr = xr + jnp.where(row == 0, ar * mr - ai * mi, 0.0)
    xi = xi + jnp.where(row == 0, ar * mi + ai * mr, 0.0)

    def shift(x, d):
        return jnp.where(row >= d, pltpu.roll(x, d, 0), 0.0)

    for j in range(n_steps):
        ar, ai = apr_ref[0, j:j + 1, :], api_ref[0, j:j + 1, :]
        sr, si = shift(xr, 1 << j), shift(xi, 1 << j)
        xr, xi = xr + (ar * sr - ai * si), xi + (ar * si + ai * sr)
    pr = jnp.where(row == 0, mr, shift(xr, 1))
    pi = jnp.where(row == 0, mi, shift(xi, 1))
    prev_ref[0] = jnp.concatenate([pr, pi], axis=1).astype(BF16)


def _ssm_out_kernel(u_ref, prev_ref, lag_ref, cin_ref, spread_ref, d_ref, y_ref, toep_s, wcin_s):
    gh, c = SSM_TILE_GROUPS * SSM_GROUP, SSM_CHUNK

    @pl.when(pl.program_id(1) == 0)
    def _():
        spread = spread_ref[...]
        diag = (lax.broadcasted_iota(jnp.int32, (gh, gh), 0) // SSM_GROUP
                == lax.broadcasted_iota(jnp.int32, (gh, gh), 1) // SSM_GROUP)
        zero = jnp.zeros((gh, gh), BF16)
        blocks = [jnp.where(diag, jnp.dot(lag_ref[0, k], spread, preferred_element_type=F32), 0.0).astype(BF16)
                  for k in range(c)]
        for s in range(c):
            for i in range(c):
                toep_s[s * gh:(s + 1) * gh, i * gh:(i + 1) * gh] = blocks[i - s] if i >= s else zero
        rows = wcin_s.shape[0]
        row_g = (lax.broadcasted_iota(jnp.int32, (rows, gh), 0) % (rows // 2)) // SSM_STATE
        own = row_g == lax.broadcasted_iota(jnp.int32, (rows, gh), 1) // SSM_GROUP
        for i in range(c):
            wcin_s[:, i * gh:(i + 1) * gh] = jnp.where(
                own, jnp.dot(cin_ref[0, i], spread, preferred_element_type=F32), 0.0).astype(BF16)

    us = _chunk_steps(u_ref)
    cb = us[0].shape[0]
    u = jnp.concatenate(us, axis=1).astype(BF16)
    y = (jnp.dot(u, toep_s[...], preferred_element_type=F32)
         + jnp.dot(prev_ref[0], wcin_s[...], preferred_element_type=F32))
    for s in range(SSM_CHUNK):
        ys = y[:, s * 128:(s + 1) * 128] + d_ref[0] * us[s]
        y_ref[pl.ds(s, cb, stride=SSM_CHUNK), :] = 0.5 * ys * (1.0 + lax.erf(ys * (0.5 ** 0.5)))


def _ssm(u, u_lead, tables):
    lagc, wout_a, wout_b, cinc, spread, apr, api, dsk, n_steps = tables
    t = u.shape[0]
    nc = t // SSM_CHUNK
    cb = _tile(nc, 256) if nc % 128 == 0 else nc
    lanes = 128
    k = SSM_CHUNK * lanes
    states = 2 * SSM_TILE_GROUPS * SSM_STATE
    prev = pl.pallas_call(
        functools.partial(_ssm_state_kernel, n_steps=n_steps),
        grid=(SSM_TILES,),
        in_specs=[pl.BlockSpec((t, lanes), lambda g: (0, g)),
                  pl.BlockSpec((BLOCK, lanes), lambda g: (0, g)),
                  pl.BlockSpec((1, SSM_CHUNK, lanes, 2 * SSM_STATE), lambda g: (g, 0, 0, 0)),
                  pl.BlockSpec((1, SSM_CHUNK, lanes, 2 * SSM_STATE), lambda g: (g, 0, 0, 0)),
                  pl.BlockSpec((1, n_steps, states // 2), lambda g: (g, 0, 0)),
                  pl.BlockSpec((1, n_steps, states // 2), lambda g: (g, 0, 0))],
        out_specs=pl.BlockSpec((1, nc, states), lambda g: (g, 0, 0)),
        out_shape=jax.ShapeDtypeStruct((SSM_TILES, nc, states), BF16),
        scratch_shapes=[pltpu.VMEM((k, states), BF16)],
        compiler_params=_params(("parallel",), 48),
        name="s5_states",
    )(u, u_lead, wout_a, wout_b, apr, api)
    return pl.pallas_call(
        _ssm_out_kernel,
        grid=(SSM_TILES, nc // cb),
        in_specs=[pl.BlockSpec((cb * SSM_CHUNK, lanes), lambda g, c: (c, g)),
                  pl.BlockSpec((1, cb, states), lambda g, c: (g, c, 0)),
                  pl.BlockSpec((1, SSM_CHUNK, lanes, SSM_GROUP), lambda g, c: (g, 0, 0, 0)),
                  pl.BlockSpec((1, SSM_CHUNK, states, SSM_GROUP), lambda g, c: (g, 0, 0, 0)),
                  pl.BlockSpec((SSM_GROUP, lanes), lambda g, c: (0, 0)),
                  pl.BlockSpec((1, 1, lanes), lambda g, c: (g, 0, 0))],
        out_specs=pl.BlockSpec((cb * SSM_CHUNK, lanes), lambda g, c: (c, g)),
        out_shape=jax.ShapeDtypeStruct((t, SSM_WIDTH), F32),
        scratch_shapes=[pltpu.VMEM((k, k), BF16), pltpu.VMEM((states, k), BF16)],
        compiler_params=_params(("parallel", "arbitrary"), 48),
        name="s5_outputs",
    )(u, prev, lagc, cinc, spread, dsk)


def _glu_kernel(y_ref, w_ref, o_ref):
    y = y_ref[...]
    gate = jax.nn.sigmoid(jnp.dot(y.astype(BF16), w_ref[...].astype(BF16), preferred_element_type=F32))
    o_ref[...] = (y * gate).astype(BF16)


def _glu(y, w):
    m, d = y.shape
    tm = _tile(m, 1024)
    return pl.pallas_call(
        _glu_kernel,
        grid=(m // tm,),
        in_specs=[pl.BlockSpec((tm, d), lambda i: (i, 0)),
                  pl.BlockSpec((d, d), lambda i: (0, 0))],
        out_specs=pl.BlockSpec((tm, d), lambda i: (i, 0)),
        out_shape=jax.ShapeDtypeStruct((m, d), BF16),
        compiler_params=_params(("parallel",), 40),
        name="ssm_glu",
    )(y, w)


def _mix_kernel(a_ref, y_ref, wa_ref, ws_ref, ga_ref, gs_ref, o_ref):
    attn_d = jnp.dot(a_ref[...], wa_ref[...].astype(BF16), preferred_element_type=F32)
    ssm_d = jnp.dot(y_ref[...], ws_ref[...].astype(BF16), preferred_element_type=F32)
    mix = (jax.nn.sigmoid(ga_ref[...].astype(F32)) * attn_d
           + jax.nn.sigmoid(gs_ref[...].astype(F32)) * ssm_d)
    o_ref[...] = mix.astype(BF16)


def _mix(attn, y2, w_attn_o, w_ssm_o, z):
    m = attn.shape[0]
    tm, tn = _tile(m, 1024), 512
    ga_blk, gs_blk = COL_GA // tn, COL_GS // tn
    return pl.pallas_call(
        _mix_kernel,
        grid=(m // tm, D_MODEL // tn),
        in_specs=[pl.BlockSpec((tm, ATTN_WIDTH), lambda i, j: (i, 0)),
                  pl.BlockSpec((tm, SSM_WIDTH), lambda i, j: (i, 0)),
                  pl.BlockSpec((ATTN_WIDTH, tn), lambda i, j: (0, j)),
                  pl.BlockSpec((SSM_WIDTH, tn), lambda i, j: (0, j)),
                  pl.BlockSpec((tm, tn), lambda i, j: (i, j + ga_blk)),
                  pl.BlockSpec((tm, tn), lambda i, j: (i, j + gs_blk))],
        out_specs=pl.BlockSpec((tm, tn), lambda i, j: (i, j)),
        out_shape=jax.ShapeDtypeStruct((m, D_MODEL), BF16),
        compiler_params=_params(("parallel", "arbitrary"), 48),
        name="gated_merge",
    )(attn, y2, w_attn_o, w_ssm_o, z, z)


def _resid_kernel(a_ref, w_ref, h_ref, o_ref):
    o_ref[...] = DEEPNORM_ALPHA * h_ref[...] + jnp.dot(
        a_ref[...], w_ref[...].astype(BF16), preferred_element_type=F32)


def _out_proj(mix, w_out, h0):
    m = mix.shape[0]
    tm, tn = _tile(m, 1024), 512
    return pl.pallas_call(
        _resid_kernel,
        grid=(m // tm, D_MODEL // tn),
        in_specs=[pl.BlockSpec((tm, D_MODEL), lambda i, j: (i, 0)),
                  pl.BlockSpec((D_MODEL, tn), lambda i, j: (0, j)),
                  pl.BlockSpec((tm, tn), lambda i, j: (i, j))],
        out_specs=pl.BlockSpec((tm, tn), lambda i, j: (i, j)),
        out_shape=jax.ShapeDtypeStruct((m, D_MODEL), F32),
        compiler_params=_params(("parallel", "arbitrary"), 56),
        name="out_proj",
    )(mix, w_out, h0)


def _token_mixer(x, meta_tokens, ln0_g, ln0_b, w_in, attn_sinks, w_attn_o, ssm_tables_args,
                 w_glu, w_ssm_o, w_out, ln1_g, ln1_b):
    t = x.shape[0]
    lead = jnp.concatenate([jnp.zeros((N_PAD, D_MODEL), F32), meta_tokens.astype(F32)], axis=0)
    h0, h0b = _layer_norm(x, ln0_g, ln0_b)
    _, leadb = _layer_norm(lead, ln0_g, ln0_b)

    z = _project(h0b, w_in, 0, IN_WIDTH)
    z_lead = _project(leadb, w_in, COL_KV, 2 * KV_WIDTH + SSM_WIDTH)

    attn = _attention(z, z_lead, attn_sinks)

    lead_valid = (jnp.arange(BLOCK) >= N_PAD)[:, None]
    u_lead = jnp.where(lead_valid, z_lead[:, 2 * KV_WIDTH:].astype(F32), 0.0)
    tables = _ssm_tables(*ssm_tables_args, n_chunks=t // SSM_CHUNK)
    y = _ssm(z[:, COL_U:COL_GA].astype(F32), u_lead, tables)
    y2 = _glu(y, w_glu)

    mix = _mix(attn, y2, w_attn_o, w_ssm_o, z)
    r = _out_proj(mix, w_out, h0)
    return _layer_norm(r, ln1_g, ln1_b)


def _router_kernel(h_ref, wr_ref, bias_ref, idx_ref, wt_ref):
    tm = h_ref.shape[0]
    ng, ne = N_EXPERT_GROUPS, EXPERTS_PER_GROUP
    logits = lax.dot_general(wr_ref[...], h_ref[...], (((1,), (1,)), ((), ())),
                             precision=lax.Precision.HIGHEST, preferred_element_type=F32)
    scores = jax.nn.sigmoid(logits)
    sel = (scores + bias_ref[...]).reshape(ng, ne, tm)
    scores = scores.reshape(ng, ne, tm)
    e_in = lax.broadcasted_iota(jnp.int32, (ng, ne, tm), 1)
    e_id = lax.broadcasted_iota(jnp.int32, (ng, ne, tm), 0) * ne + e_in
    neg = -jnp.inf

    top1 = jnp.max(sel, axis=1, keepdims=True)
    first = jnp.min(jnp.where(sel == top1, e_in, ne), axis=1, keepdims=True)
    top2 = jnp.max(jnp.where(e_in == first, neg, sel), axis=1, keepdims=True)
    grp = (top1 + top2)[:, 0, :]

    g_id = lax.broadcasted_iota(jnp.int32, (ng, tm), 0)
    chosen = jnp.zeros((ng, tm), jnp.bool_)
    for _ in range(TOPK_GROUPS):
        best = jnp.max(grp, axis=0, keepdims=True)
        pick = g_id == jnp.min(jnp.where(grp == best, g_id, ng), axis=0, keepdims=True)
        chosen = chosen | pick
        grp = jnp.where(pick, neg, grp)

    cur = jnp.where(chosen[:, None, :], sel, NEG_INF)
    ids, wts = [], []
    for _ in range(TOP_K):
        best = jnp.max(jnp.max(cur, axis=1, keepdims=True), axis=0, keepdims=True)
        cand = jnp.where(cur == best, e_id, N_EXPERTS)
        win = jnp.min(jnp.min(cand, axis=1, keepdims=True), axis=0, keepdims=True)
        pick = e_id == win
        wsel = jnp.where(pick, scores, 0.0)
        ids.append(win[0])
        wts.append(jnp.sum(jnp.sum(wsel, axis=1, keepdims=True), axis=0, keepdims=True)[0])
        cur = jnp.where(pick, neg, cur)
    w = jnp.concatenate(wts, axis=0)
    idx_ref[...] = jnp.concatenate(ids, axis=0)
    wt_ref[...] = w / jnp.sum(w, axis=0, keepdims=True) * ROUTED_SCALE


def _router(h1, w_router, router_bias):
    t, d = h1.shape
    tm = _tile(t, 512)
    return pl.pallas_call(
        _router_kernel,
        grid=(t // tm,),
        in_specs=[pl.BlockSpec((tm, d), lambda i: (i, 0)),
                  pl.BlockSpec((N_EXPERTS, d), lambda i: (0, 0)),
                  pl.BlockSpec((N_EXPERTS, 1), lambda i: (0, 0))],
        out_specs=[pl.BlockSpec((TOP_K, tm), lambda i: (0, i)),
                   pl.BlockSpec((TOP_K, tm), lambda i: (0, i))],
        out_shape=[jax.ShapeDtypeStruct((TOP_K, t), jnp.int32),
                   jax.ShapeDtypeStruct((TOP_K, t), F32)],
        compiler_params=_params(("parallel",), 40),
        name="router",
    )(h1, w_router.T, router_bias.reshape(N_EXPERTS, 1))


def _dispatch(idx_t):
    t = idx_t.shape[1]
    n_assign = TOP_K * t
    n_blocks = -(-n_assign // ROW_BLOCK) + N_EXPERTS
    n_rows = n_blocks * ROW_BLOCK
    e_flat = idx_t.reshape(-1)
    onehot = (e_flat[:, None] == jnp.arange(N_EXPERTS, dtype=jnp.int32)[None, :]).astype(jnp.int32)
    csum = jnp.cumsum(onehot, axis=0)
    rank = jnp.take_along_axis(csum, e_flat[:, None], axis=1)[:, 0] - 1
    counts = csum[-1]
    padded = (counts + ROW_BLOCK - 1) // ROW_BLOCK * ROW_BLOCK
    ends = jnp.cumsum(padded)
    starts = ends - padded
    dest = starts[e_flat] + rank
    tok_flat = jnp.tile(jnp.arange(t, dtype=jnp.int32), TOP_K)
    row_tok = jnp.zeros((n_rows,), jnp.int32).at[dest].set(tok_flat)
    block_start = jnp.arange(n_blocks, dtype=jnp.int32) * ROW_BLOCK
    block_expert = jnp.minimum(
        jnp.sum((ends[None, :] <= block_start[:, None]).astype(jnp.int32), axis=1), N_EXPERTS - 1)
    n_used = (ends[-1] // ROW_BLOCK).astype(jnp.int32).reshape(1)
    has = counts > 0
    cand = jnp.where(has[None, :] & (jnp.arange(N_EXPERTS)[None, :] > jnp.arange(N_EXPERTS)[:, None]),
                     jnp.arange(N_EXPERTS)[None, :], N_EXPERTS)
    nxt = jnp.min(cand, axis=1)
    next_expert = jnp.where(nxt < N_EXPERTS, nxt, -1).astype(jnp.int32)
    return (row_tok.reshape(n_blocks, 1, ROW_BLOCK), block_expert, n_used, next_expert,
            dest.reshape(TOP_K, t))


def _gather_rows(src_hbm, idx_smem, dst, sem, n, unrolled=False):
    def body(i, carry):
        pltpu.make_async_copy(src_hbm.at[pl.ds(idx_smem[0, i], 1)], dst.at[pl.ds(i, 1)], sem).start()
        return carry
    if unrolled:
        for i in range(n):
            body(i, 0)
    else:
        lax.fori_loop(0, n, body, 0, unroll=8)


def _wait_rows(src_hbm, dst, sem, n):
    pltpu.make_async_copy(src_hbm.at[pl.ds(0, n)], dst, sem).wait()


def _expert_up_kernel(be_ref, nu_ref, nx_ref, tok_hbm, h_hbm, w1_hbm, w3_hbm, o_ref,
                      idx_smem, xbuf, wstage, wcast, isem, xsem, wsem):
    b = pl.program_id(0)
    n_used = nu_ref[0]
    last = n_used - 1
    e = be_ref[b]
    slot = b % 2

    def idx_copy(blk, s):
        return pltpu.make_async_copy(tok_hbm.at[blk], idx_smem.at[s], isem.at[s])

    def weight_copies(ex):
        return (pltpu.make_async_copy(w1_hbm.at[ex], wstage.at[0], wsem.at[0]),
                pltpu.make_async_copy(w3_hbm.at[ex], wstage.at[1], wsem.at[1]))

    @pl.when(b == 0)
    def _():
        for c in weight_copies(e):
            c.start()
        idx_copy(0, 0).start()
        idx_copy(0, 0).wait()
        _gather_rows(h_hbm, idx_smem.at[0], xbuf.at[0], xsem.at[0], ROW_BLOCK)
        idx_copy(jnp.minimum(1, last), 1).start()

    @pl.when(b < n_used)
    def _():
        first_of_expert = jnp.logical_or(b == 0, be_ref[jnp.maximum(b - 1, 0)] != e)

        @pl.when(first_of_expert)
        def _():
            for c in weight_copies(e):
                c.wait()
            rows = wstage.shape[1] // 8
            for m in range(2):
                for r in range(8):
                    wcast[m, r * rows:(r + 1) * rows] = wstage[m, r * rows:(r + 1) * rows].astype(BF16)
            nxt = nx_ref[e]

            @pl.when(nxt >= 0)
            def _():
                for c in weight_copies(nxt):
                    c.start()

        idx_copy(jnp.minimum(b + 1, last), 1 - slot).wait()
        _wait_rows(h_hbm, xbuf.at[slot], xsem.at[slot], ROW_BLOCK)
        _gather_rows(h_hbm, idx_smem.at[1 - slot], xbuf.at[1 - slot], xsem.at[1 - slot], ROW_BLOCK,
                     unrolled=True)
        idx_copy(jnp.minimum(b + 2, last), slot).start()
        x = xbuf[slot].astype(BF16)
        a = jnp.dot(x, wcast[0], preferred_element_type=F32)
        g = jnp.dot(x, wcast[1], preferred_element_type=F32)
        o_ref[...] = (a * jax.nn.sigmoid(a) * g).astype(BF16)

        @pl.when(b == last)
        def _():
            _wait_rows(h_hbm, xbuf.at[1 - slot], xsem.at[1 - slot], ROW_BLOCK)
            idx_copy(last, slot).wait()

    @pl.when(b >= n_used)
    def _():
        o_ref[...] = jnp.zeros_like(o_ref)


def _expert_up(h1, row_tok, block_expert, n_used, next_expert, w1, w3):
    n_blocks = row_tok.shape[0]
    d = w1.shape[1]
    grid_spec = pltpu.PrefetchScalarGridSpec(
        num_scalar_prefetch=3,
        grid=(n_blocks,),
        in_specs=[pl.BlockSpec(memory_space=pl.ANY)] * 4,
        out_specs=pl.BlockSpec((ROW_BLOCK, EXPERT_DIM), lambda b, *_: (b, 0)),
        scratch_shapes=[pltpu.SMEM((2, 1, ROW_BLOCK), jnp.int32),
                        pltpu.VMEM((2, ROW_BLOCK, d), F32),
                        pltpu.VMEM((2, d, EXPERT_DIM), F32),
                        pltpu.VMEM((2, d, EXPERT_DIM), BF16),
                        pltpu.SemaphoreType.DMA((2,)),
                        pltpu.SemaphoreType.DMA((2,)),
                        pltpu.SemaphoreType.DMA((2,))])
    return pl.pallas_call(
        _expert_up_kernel,
        grid_spec=grid_spec,
        out_shape=jax.ShapeDtypeStruct((n_blocks * ROW_BLOCK, EXPERT_DIM), BF16),
        compiler_params=_params(("arbitrary",), 58),
        name="expert_up",
    )(block_expert, n_used, next_expert, row_tok, h1, w1, w3)


def _expert_down_kernel(be_ref, nu_ref, nx_ref, a_ref, w2_hbm, o_ref, wstage, wcast, wsem):
    b = pl.program_id(0)
    n_used = nu_ref[0]
    e = be_ref[b]

    def weight_copy(ex):
        return pltpu.make_async_copy(w2_hbm.at[ex], wstage, wsem)

    @pl.when(b == 0)
    def _():
        weight_copy(e).start()

    @pl.when(b < n_used)
    def _():
        first_of_expert = jnp.logical_or(b == 0, be_ref[jnp.maximum(b - 1, 0)] != e)

        @pl.when(first_of_expert)
        def _():
            weight_copy(e).wait()
            wcast[...] = wstage[...].astype(BF16)
            nxt = nx_ref[e]

            @pl.when(nxt >= 0)
            def _():
                weight_copy(nxt).start()

        o_ref[...] = jnp.dot(a_ref[...], wcast[...], preferred_element_type=F32)

    @pl.when(b >= n_used)
    def _():
        o_ref[...] = jnp.zeros_like(o_ref)


def _expert_down(act, block_expert, n_used, next_expert, w2):
    n_rows = act.shape[0]
    n_blocks = n_rows // ROW_BLOCK
    d = w2.shape[2]
    grid_spec = pltpu.PrefetchScalarGridSpec(
        num_scalar_prefetch=3,
        grid=(n_blocks,),
        in_specs=[pl.BlockSpec((ROW_BLOCK, EXPERT_DIM), lambda b, *_: (b, 0)),
                  pl.BlockSpec(memory_space=pl.ANY)],
        out_specs=pl.BlockSpec((ROW_BLOCK, d), lambda b, *_: (b, 0)),
        scratch_shapes=[pltpu.VMEM((EXPERT_DIM, d), F32),
                        pltpu.VMEM((EXPERT_DIM, d), BF16),
                        pltpu.SemaphoreType.DMA(())])
    return pl.pallas_call(
        _expert_down_kernel,
        grid_spec=grid_spec,
        out_shape=jax.ShapeDtypeStruct((n_rows, d), F32),
        compiler_params=_params(("arbitrary",), 40),
        name="expert_down",
    )(block_expert, n_used, next_expert, act, w2)


def _shared_kernel(h_ref, w1_ref, w3_ref, w2_ref, o_ref):
    x = h_ref[...]
    a = jnp.dot(x, w1_ref[...], preferred_element_type=F32)
    g = jnp.dot(x, w3_ref[...], preferred_element_type=F32)
    act = (a * jax.nn.sigmoid(a) * g).astype(BF16)
    o_ref[...] = jnp.dot(act, w2_ref[...], preferred_element_type=F32)


def _shared_expert(h1b, w1, w3, w2):
    t, d = h1b.shape
    tm = _tile(t, 512)
    return pl.pallas_call(
        _shared_kernel,
        grid=(t // tm,),
        in_specs=[pl.BlockSpec((tm, d), lambda i: (i, 0)),
                  pl.BlockSpec((d, SHARED_DIM), lambda i: (0, 0)),
                  pl.BlockSpec((d, SHARED_DIM), lambda i: (0, 0)),
                  pl.BlockSpec((SHARED_DIM, d), lambda i: (0, 0))],
        out_specs=pl.BlockSpec((tm, d), lambda i: (i, 0)),
        out_shape=jax.ShapeDtypeStruct((t, d), F32),
        compiler_params=_params(("parallel",), 56),
        name="shared_expert",
    )(h1b, w1.astype(BF16), w3.astype(BF16), w2.astype(BF16))


COMBINE_TOKENS = 64


def _combine_kernel(dest_hbm, y_hbm, h_ref, s_ref, w_ref, g_ref, b_ref, o_ref,
                    idx_smem, ybuf, isem, ysem):
    i = pl.program_id(0)
    n = pl.num_programs(0)
    slot = i % 2
    rows = TOP_K * COMBINE_TOKENS

    def idx_copy(blk, s):
        return pltpu.make_async_copy(dest_hbm.at[blk], idx_smem.at[s], isem.at[s])

    last = n - 1

    @pl.when(i == 0)
    def _():
        idx_copy(0, 0).start()
        idx_copy(0, 0).wait()
        _gather_rows(y_hbm, idx_smem.at[0], ybuf.at[0], ysem.at[0], rows)
        idx_copy(jnp.minimum(1, last), 1).start()

    idx_copy(jnp.minimum(i + 1, last), 1 - slot).wait()
    _wait_rows(y_hbm, ybuf.at[slot], ysem.at[slot], rows)
    _gather_rows(y_hbm, idx_smem.at[1 - slot], ybuf.at[1 - slot], ysem.at[1 - slot], rows, unrolled=True)
    idx_copy(jnp.minimum(i + 2, last), slot).start()
    acc = DEEPNORM_ALPHA * h_ref[...] + s_ref[...]
    for j in range(TOP_K):
        acc = acc + w_ref[:, j:j + 1] * ybuf[slot, j * COMBINE_TOKENS:(j + 1) * COMBINE_TOKENS, :]
    o_ref[...] = _ln_rows(acc, g_ref[...], b_ref[...])

    @pl.when(i == last)
    def _():
        _wait_rows(y_hbm, ybuf.at[1 - slot], ysem.at[1 - slot], rows)
        idx_copy(last, slot).wait()


def _combine(yb, dest, w_t, h1, shared, g, b):
    t, d = h1.shape
    tm = COMBINE_TOKENS
    n_tiles = t // tm
    dest_tiles = dest.reshape(TOP_K, n_tiles, tm).transpose(1, 0, 2).reshape(n_tiles, 1, TOP_K * tm)
    return pl.pallas_call(
        _combine_kernel,
        grid=(n_tiles,),
        in_specs=[pl.BlockSpec(memory_space=pl.ANY),
                  pl.BlockSpec(memory_space=pl.ANY),
                  pl.BlockSpec((tm, d), lambda i: (i, 0)),
                  pl.BlockSpec((tm, d), lambda i: (i, 0)),
                  pl.BlockSpec((tm, TOP_K), lambda i: (i, 0)),
                  pl.BlockSpec((1, d), lambda i: (0, 0)),
                  pl.BlockSpec((1, d), lambda i: (0, 0))],
        out_specs=pl.BlockSpec((tm, d), lambda i: (i, 0)),
        out_shape=jax.ShapeDtypeStruct((t, d), F32),
        scratch_shapes=[pltpu.SMEM((2, 1, TOP_K * tm), jnp.int32),
                        pltpu.VMEM((2, TOP_K * tm, d), F32),
                        pltpu.SemaphoreType.DMA((2,)),
                        pltpu.SemaphoreType.DMA((2,))],
        compiler_params=_params(("arbitrary",), 40),
        name="combine_norm",
    )(dest_tiles, yb, h1, shared, w_t.T, g.reshape(1, d), b.reshape(1, d))


def _channel_mixer(h1, h1b, w_router, router_bias, expert_w1, expert_w3, expert_w2,
                   shared_w1, shared_w3, shared_w2, ln2_g, ln2_b):
    idx_t, w_t = _router(h1, w_router, router_bias)
    row_tok, block_expert, n_used, next_expert, dest = _dispatch(idx_t)
    shared = _shared_expert(h1b, shared_w1, shared_w3, shared_w2)
    act = _expert_up(h1, row_tok, block_expert, n_used, next_expert, expert_w1, expert_w3)
    yb = _expert_down(act, block_expert, n_used, next_expert, expert_w2)
    return _combine(yb, dest, w_t, h1, shared, ln2_g, ln2_b)


def kernel(x, meta_tokens, ln0_g, ln0_b, w_in, attn_sinks, w_attn_o, ssm_lam_re, ssm_lam_im, ssm_log_dt,
           ssm_b_re, ssm_b_im, ssm_c_re, ssm_c_im, ssm_d, w_glu, w_ssm_o, w_out, ln1_g, ln1_b, w_router,
           router_bias, expert_w1, expert_w3, expert_w2, shared_w1, shared_w3, shared_w2, ln2_g, ln2_b):
    assert x.shape[0] == 1 and w_in.shape[0] == 1
    ssm_args = (ssm_lam_re[0], ssm_lam_im[0], ssm_log_dt[0], ssm_b_re[0], ssm_b_im[0],
                ssm_c_re[0], ssm_c_im[0], ssm_d[0])
    h1, h1b = _token_mixer(x[0], meta_tokens, ln0_g, ln0_b, w_in[0], attn_sinks[0], w_attn_o[0],
                           ssm_args, w_glu[0], w_ssm_o[0], w_out[0], ln1_g[0], ln1_b[0])
    out = _channel_mixer(h1, h1b, w_router[0], router_bias[0], expert_w1[0], expert_w3[0],
                         expert_w2[0], shared_w1[0], shared_w3[0], shared_w2[0], ln2_g[0], ln2_b[0])
    return out[None]
```

```python
import functools
import math

import jax
import jax.numpy as jnp
import numpy as np
from jax import lax
from jax.experimental import pallas as pl
from jax.experimental.pallas import tpu as pltpu

F32 = jnp.float32
BF16 = jnp.bfloat16

D_MODEL = 4096
N_META = 16
BLOCK = 128
N_PAD = BLOCK - N_META

N_HEADS = 32
N_KV_HEADS = 4
HEAD_DIM = 64
HEADS_PER_KV = N_HEADS // N_KV_HEADS
ATTN_WIDTH = N_HEADS * HEAD_DIM
KV_WIDTH = N_KV_HEADS * HEAD_DIM

SSM_WIDTH = 1024
SSM_GROUP = 16
N_SSM_GROUPS = SSM_WIDTH // SSM_GROUP
SSM_STATE = 64
SSM_CHUNK = 16
SSM_TILE_GROUPS = 128 // SSM_GROUP
SSM_TILES = N_SSM_GROUPS // SSM_TILE_GROUPS

N_EXPERTS = 64
TOP_K = 8
N_EXPERT_GROUPS = 8
EXPERTS_PER_GROUP = N_EXPERTS // N_EXPERT_GROUPS
TOPK_GROUPS = 4
EXPERT_DIM = 512
SHARED_DIM = 512
ROUTED_SCALE = 2.5
ROW_BLOCK = 512

IN_WIDTH = ATTN_WIDTH + 2 * KV_WIDTH + SSM_WIDTH + 2 * D_MODEL
COL_KV = ATTN_WIDTH
COL_U = ATTN_WIDTH + 2 * KV_WIDTH
COL_GA = COL_U + SSM_WIDTH
COL_GS = COL_GA + D_MODEL

DEEPNORM_ALPHA = 2.0 ** 0.25
LN_EPS = 1e-5
NEG_INF = -1e30

MIB = 1024 * 1024


def _params(semantics, vmem_mib):
    return pltpu.CompilerParams(dimension_semantics=semantics, vmem_limit_bytes=vmem_mib * MIB)


def _tile(n, pref):
    t = min(n, pref)
    while n % t:
        t -= 128
    return t


def _ln_rows(x, g, b):
    mu = jnp.mean(x, axis=-1, keepdims=True)
    xc = x - mu
    var = jnp.mean(xc * xc, axis=-1, keepdims=True)
    return xc * lax.rsqrt(var + LN_EPS) * g + b


def _ln_kernel(x_ref, g_ref, b_ref, o_ref, ob_ref):
    y = _ln_rows(x_ref[...], g_ref[...], b_ref[...])
    o_ref[...] = y
    ob_ref[...] = y.astype(BF16)


def _layer_norm(x, g, b):
    m, d = x.shape
    tm = _tile(m, 256)
    out_specs = [pl.BlockSpec((tm, d), lambda i: (i, 0)), pl.BlockSpec((tm, d), lambda i: (i, 0))]
    out_shape = [jax.ShapeDtypeStruct((m, d), F32), jax.ShapeDtypeStruct((m, d), BF16)]
    return pl.pallas_call(
        _ln_kernel,
        grid=(m // tm,),
        in_specs=[pl.BlockSpec((tm, d), lambda i: (i, 0)),
                  pl.BlockSpec((1, d), lambda i: (0, 0)),
                  pl.BlockSpec((1, d), lambda i: (0, 0))],
        out_specs=out_specs,
        out_shape=out_shape,
        compiler_params=_params(("parallel",), 40),
        name="layer_norm",
    )(x, g.reshape(1, d), b.reshape(1, d))


def _proj_kernel(a_ref, w_ref, o_ref):
    o_ref[...] = jnp.dot(a_ref[...], w_ref[...].astype(BF16),
                         preferred_element_type=F32).astype(o_ref.dtype)


def _project(a, w, col0, n_cols, tm_pref=1024, tn=512):
    m, k = a.shape
    tm = _tile(m, tm_pref)
    off = col0 // tn
    return pl.pallas_call(
        _proj_kernel,
        grid=(m // tm, n_cols // tn),
        in_specs=[pl.BlockSpec((tm, k), lambda i, j: (i, 0)),
                  pl.BlockSpec((k, tn), lambda i, j: (0, j + off))],
        out_specs=pl.BlockSpec((tm, tn), lambda i, j: (i, j)),
        out_shape=jax.ShapeDtypeStruct((m, n_cols), BF16),
        compiler_params=_params(("parallel", "arbitrary"), 56),
        name="in_proj",
    )(a, w)


def _attn_kernel(sink_ref, q_ref, kvc_ref, kvp_ref, kvm_ref, o_ref):
    first = pl.program_id(0) == 0
    kv_all = jnp.concatenate([kvp_ref[...], kvc_ref[...], kvm_ref[N_PAD:BLOCK, :]], axis=0)
    n_keys = 2 * BLOCK + N_META
    rows = HEADS_PER_KV * BLOCK

    lane = lax.broadcasted_iota(jnp.int32, (BLOCK, BLOCK), 1)
    lo_q = lane < HEAD_DIM
    lane_kv = lax.broadcasted_iota(jnp.int32, (n_keys, BLOCK), 1)
    lo_kv = lane_kv < HEAD_DIM

    qi = lax.broadcasted_iota(jnp.int32, (rows, n_keys), 0) & (BLOCK - 1)
    kj = lax.broadcasted_iota(jnp.int32, (rows, n_keys), 1)
    in_window = (kj > qi) & (kj <= qi + BLOCK) & ((kj >= BLOCK) | jnp.logical_not(first))
    visible = in_window | (kj >= 2 * BLOCK)
    local_head = lax.broadcasted_iota(jnp.int32, (rows, 1), 0) // BLOCK

    def dup_head(tile, odd):
        t = tile.astype(F32)
        r = pltpu.roll(t, HEAD_DIM, 1)
        keep = jnp.logical_not(lo_kv) if odd else lo_kv
        return jnp.where(keep, t, r).astype(BF16)

    for kh in range(N_KV_HEADS):
        t0 = (kh // 2) * BLOCK
        k2 = dup_head(kv_all[:, t0:t0 + BLOCK], kh % 2)
        v2 = dup_head(kv_all[:, KV_WIDTH + t0:KV_WIDTH + t0 + BLOCK], kh % 2)
        pieces = []
        for p in range(HEADS_PER_KV // 2):
            c0 = kh * HEADS_PER_KV * HEAD_DIM + p * BLOCK
            qp = q_ref[:, c0:c0 + BLOCK]
            zero = jnp.zeros_like(qp)
            pieces.append(jnp.where(lo_q, qp, zero))
            pieces.append(jnp.where(lo_q, zero, qp))
        qs = jnp.concatenate(pieces, axis=0)
        s = lax.dot_general(qs, k2, (((1,), (1,)), ((), ())), preferred_element_type=F32)
        s = jnp.where(visible, s * (HEAD_DIM ** -0.5), NEG_INF)
        sink = jnp.zeros((rows, 1), F32)
        for t in range(HEADS_PER_KV):
            sink = jnp.where(local_head == t, sink_ref[kh * HEADS_PER_KV + t], sink)
        m = jnp.maximum(jnp.max(s, axis=-1, keepdims=True), sink)
        e = jnp.exp(s - m)
        den = jnp.sum(e, axis=-1, keepdims=True) + jnp.exp(sink - m)
        o2 = jnp.dot(e.astype(BF16), v2, preferred_element_type=F32) / den
        for p in range(HEADS_PER_KV // 2):
            c0 = kh * HEADS_PER_KV * HEAD_DIM + p * BLOCK
            r0 = 2 * p * BLOCK
            o_ref[:, c0:c0 + BLOCK] = jnp.where(
                lo_q, o2[r0:r0 + BLOCK], o2[r0 + BLOCK:r0 + 2 * BLOCK]).astype(BF16)


def _attention(z, z_lead, sinks):
    t = z.shape[0]
    kv_blk = COL_KV // (2 * KV_WIDTH)
    return pl.pallas_call(
        _attn_kernel,
        grid=(t // BLOCK,),
        in_specs=[pl.BlockSpec(memory_space=pltpu.SMEM),
                  pl.BlockSpec((BLOCK, ATTN_WIDTH), lambda n: (n, 0)),
                  pl.BlockSpec((BLOCK, 2 * KV_WIDTH), lambda n: (n, kv_blk)),
                  pl.BlockSpec((BLOCK, 2 * KV_WIDTH), lambda n: (jnp.maximum(n - 1, 0), kv_blk)),
                  pl.BlockSpec((BLOCK, 2 * KV_WIDTH), lambda n: (0, 0))],
        out_specs=pl.BlockSpec((BLOCK, ATTN_WIDTH), lambda n: (n, 0)),
        out_shape=jax.ShapeDtypeStruct((t, ATTN_WIDTH), BF16),
        compiler_params=_params(("parallel",), 40),
        name="swa_attention",
    )(sinks.astype(F32), z, z, z, z_lead)


def _cmul(ar, ai, br, bi):
    return ar * br - ai * bi, ar * bi + ai * br


def _ssm_tables(lam_re, lam_im, log_dt, b_re, b_im, c_re, c_im, d_skip, n_chunks):
    hi = lax.Precision.HIGHEST
    g, p, h, c = N_SSM_GROUPS, SSM_STATE, SSM_GROUP, SSM_CHUNK
    dt = jnp.exp(log_dt)[:, None]
    mag = jnp.exp(dt * lam_re)
    are = mag * jnp.cos(dt * lam_im)
    aim = mag * jnp.sin(dt * lam_im)
    den = lam_re * lam_re + lam_im * lam_im
    num_re = are - 1.0
    coef_re = (num_re * lam_re + aim * lam_im) / den
    coef_im = (aim * lam_re - num_re * lam_im) / den
    bb_re = coef_re[..., None] * b_re - coef_im[..., None] * b_im
    bb_im = coef_re[..., None] * b_im + coef_im[..., None] * b_re

    pr, pi = [jnp.ones_like(are)], [jnp.zeros_like(are)]
    for _ in range(c):
        r, i = _cmul(pr[-1], pi[-1], are, aim)
        pr.append(r)
        pi.append(i)
    pr, pi = jnp.stack(pr), jnp.stack(pi)

    m_re = pr[:c, :, :, None] * bb_re[None] - pi[:c, :, :, None] * bb_im[None]
    m_im = pr[:c, :, :, None] * bb_im[None] + pi[:c, :, :, None] * bb_re[None]
    lag = (jnp.einsum('ghp,kgpj->kghj', c_re, m_re, precision=hi)
           - jnp.einsum('ghp,kgpj->kghj', c_im, m_im, precision=hi))
    nt, gt = SSM_TILES, SSM_TILE_GROUPS
    lagc = lag.reshape(c, nt, gt, h, h).transpose(1, 0, 2, 4, 3)
    lagc = lagc.reshape(nt, c, gt * h, h)
    rev = np.arange(c - 1, -1, -1)
    wo_re = pr[rev][:, :, :, None] * bb_re[None] - pi[rev][:, :, :, None] * bb_im[None]
    wo_im = pr[rev][:, :, :, None] * bb_im[None] + pi[rev][:, :, :, None] * bb_re[None]

    def lanes(first, second):
        w = jnp.stack([first, second]).reshape(2, c, nt, gt, p, h).transpose(2, 1, 3, 5, 0, 4)
        return w.reshape(nt, c, gt * h, 2 * p)

    wout_a, wout_b = lanes(wo_re, wo_im), lanes(wo_im, wo_re)
    cin_re = c_re[None] * pr[1:, :, None, :] - c_im[None] * pi[1:, :, None, :]
    cin_im = -(c_re[None] * pi[1:, :, None, :] + c_im[None] * pr[1:, :, None, :])
    cinc = jnp.stack([cin_re, cin_im]).reshape(2, c, nt, gt, h, p)
    cinc = cinc.transpose(2, 1, 0, 3, 5, 4).reshape(nt, c, 2 * gt * p, h)
    spread = jnp.tile(jnp.eye(h, dtype=BF16), (1, gt))

    n_steps = max(1, math.ceil(math.log2(n_chunks)))
    qr, qi = [pr[c]], [pi[c]]
    for _ in range(n_steps - 1):
        r, i = _cmul(qr[-1], qi[-1], qr[-1], qi[-1])
        qr.append(r)
        qi.append(i)
    apr = jnp.stack(qr, axis=0).reshape(n_steps, nt, gt * p).transpose(1, 0, 2)
    api = jnp.stack(qi, axis=0).reshape(n_steps, nt, gt * p).transpose(1, 0, 2)
    dsk = d_skip.reshape(nt, 1, gt * h)
    return (lagc.astype(BF16), wout_a.astype(BF16), wout_b.astype(BF16), cinc.astype(BF16), spread,
            apr, api, dsk, n_steps)


def _chunk_steps(u_ref):
    n = u_ref.shape[0] // SSM_CHUNK
    return [u_ref[pl.ds(s, n, stride=SSM_CHUNK), :] for s in range(SSM_CHUNK)]


def _chunk_lanes(u_ref):
    return jnp.concatenate(_chunk_steps(u_ref), axis=1)


def _ssm_state_kernel(u_ref, lead_ref, wa_ref, wb_ref, apr_ref, api_ref, prev_ref, wout_s, *, n_steps):
    p, gh = SSM_STATE, SSM_TILE_GROUPS * SSM_GROUP
    row_g = lax.broadcasted_iota(jnp.int32, (gh, 2 * p), 0) // SSM_GROUP
    lo = lax.broadcasted_iota(jnp.int32, (gh, 2 * p), 1) < p
    zero = jnp.zeros((gh, 2 * p), BF16)
    for s in range(SSM_CHUNK):
        a, b = wa_ref[0, s], wb_ref[0, s]
        for j in range(SSM_TILE_GROUPS // 2):
            even, odd = (row_g == 2 * j) & lo, (row_g == 2 * j + 1) & jnp.logical_not(lo)
            wout_s[s * gh:(s + 1) * gh, j * 2 * p:(j + 1) * 2 * p] = (
                jnp.where(even, a, jnp.where(odd, b, zero)))
            wout_s[s * gh:(s + 1) * gh, (SSM_TILE_GROUPS // 2 + j) * 2 * p:(SSM_TILE_GROUPS // 2 + j + 1) * 2 * p] = (
                jnp.where(even, b, jnp.where(odd, a, zero)))
    wout = wout_s[...]
    st = jnp.dot(_chunk_lanes(u_ref).astype(BF16), wout, preferred_element_type=F32)
    lead = jnp.dot(_chunk_lanes(lead_ref).astype(BF16), wout, preferred_element_type=F32)
    nc, half = st.shape[0], st.shape[1] // 2
    xr, xi = st[:, :half], st[:, half:]
    mr, mi = lead[-1:, :half], lead[-1:, half:]
    row = lax.broadcasted_iota(jnp.int32, (nc, half), 0)
    ar, ai = apr_ref[0, 0:1, :], api_ref[0, 0:1, :]
    xr = xr + jnp.where(row == 0, ar * mr - ai * mi, 0.0)
    xi = xi + jnp.where(row == 0, ar * mi + ai * mr, 0.0)

    def shift(x, d):
        return jnp.where(row >= d, pltpu.roll(x, d, 0), 0.0)

    for j in range(n_steps):
        ar, ai = apr_ref[0, j:j + 1, :], api_ref[0, j:j + 1, :]
        sr, si = shift(xr, 1 << j), shift(xi, 1 << j)
        xr, xi = xr + (ar * sr - ai * si), xi + (ar * si + ai * sr)
    pr = jnp.where(row == 0, mr, shift(xr, 1))
    pi = jnp.where(row == 0, mi, shift(xi, 1))
    prev_ref[0] = jnp.concatenate([pr, pi], axis=1).astype(BF16)


def _ssm_out_kernel(u_ref, prev_ref, lag_ref, cin_ref, spread_ref, d_ref, y_ref, toep_s, wcin_s):
    gh, c = SSM_TILE_GROUPS * SSM_GROUP, SSM_CHUNK

    @pl.when(pl.program_id(1) == 0)
    def _():
        spread = spread_ref[...]
        diag = (lax.broadcasted_iota(jnp.int32, (gh, gh), 0) // SSM_GROUP
                == lax.broadcasted_iota(jnp.int32, (gh, gh), 1) // SSM_GROUP)
        zero = jnp.zeros((gh, gh), BF16)
        blocks = [jnp.where(diag, jnp.dot(lag_ref[0, k], spread, preferred_element_type=F32), 0.0).astype(BF16)
                  for k in range(c)]
        for s in range(c):
            for i in range(c):
                toep_s[s * gh:(s + 1) * gh, i * gh:(i + 1) * gh] = blocks[i - s] if i >= s else zero
        rows = wcin_s.shape[0]
        row_g = (lax.broadcasted_iota(jnp.int32, (rows, gh), 0) % (rows // 2)) // SSM_STATE
        own = row_g == lax.broadcasted_iota(jnp.int32, (rows, gh), 1) // SSM_GROUP
        for i in range(c):
            wcin_s[:, i * gh:(i + 1) * gh] = jnp.where(
                own, jnp.dot(cin_ref[0, i], spread, preferred_element_type=F32), 0.0).astype(BF16)

    us = _chunk_steps(u_ref)
    cb = us[0].shape[0]
    u = jnp.concatenate(us, axis=1).astype(BF16)
    y = (jnp.dot(u, toep_s[...], preferred_element_type=F32)
         + jnp.dot(prev_ref[0], wcin_s[...], preferred_element_type=F32))
    for s in range(SSM_CHUNK):
        ys = y[:, s * 128:(s + 1) * 128] + d_ref[0] * us[s]
        y_ref[pl.ds(s, cb, stride=SSM_CHUNK), :] = 0.5 * ys * (1.0 + lax.erf(ys * (0.5 ** 0.5)))


def _ssm(u, u_lead, tables):
    lagc, wout_a, wout_b, cinc, spread, apr, api, dsk, n_steps = tables
    t = u.shape[0]
    nc = t // SSM_CHUNK
    cb = _tile(nc, 256) if nc % 128 == 0 else nc
    lanes = 128
    k = SSM_CHUNK * lanes
    states = 2 * SSM_TILE_GROUPS * SSM_STATE
    prev = pl.pallas_call(
        functools.partial(_ssm_state_kernel, n_steps=n_steps),
        grid=(SSM_TILES,),
        in_specs=[pl.BlockSpec((t, lanes), lambda g: (0, g)),
                  pl.BlockSpec((BLOCK, lanes), lambda g: (0, g)),
                  pl.BlockSpec((1, SSM_CHUNK, lanes, 2 * SSM_STATE), lambda g: (g, 0, 0, 0)),
                  pl.BlockSpec((1, SSM_CHUNK, lanes, 2 * SSM_STATE), lambda g: (g, 0, 0, 0)),
                  pl.BlockSpec((1, n_steps, states // 2), lambda g: (g, 0, 0)),
                  pl.BlockSpec((1, n_steps, states // 2), lambda g: (g, 0, 0))],
        out_specs=pl.BlockSpec((1, nc, states), lambda g: (g, 0, 0)),
        out_shape=jax.ShapeDtypeStruct((SSM_TILES, nc, states), BF16),
        scratch_shapes=[pltpu.VMEM((k, states), BF16)],
        compiler_params=_params(("parallel",), 48),
        name="s5_states",
    )(u, u_lead, wout_a, wout_b, apr, api)
    return pl.pallas_call(
        _ssm_out_kernel,
        grid=(SSM_TILES, nc // cb),
        in_specs=[pl.BlockSpec((cb * SSM_CHUNK, lanes), lambda g, c: (c, g)),
                  pl.BlockSpec((1, cb, states), lambda g, c: (g, c, 0)),
                  pl.BlockSpec((1, SSM_CHUNK, lanes, SSM_GROUP), lambda g, c: (g, 0, 0, 0)),
                  pl.BlockSpec((1, SSM_CHUNK, states, SSM_GROUP), lambda g, c: (g, 0, 0, 0)),
                  pl.BlockSpec((SSM_GROUP, lanes), lambda g, c: (0, 0)),
                  pl.BlockSpec((1, 1, lanes), lambda g, c: (g, 0, 0))],
        out_specs=pl.BlockSpec((cb * SSM_CHUNK, lanes), lambda g, c: (c, g)),
        out_shape=jax.ShapeDtypeStruct((t, SSM_WIDTH), F32),
        scratch_shapes=[pltpu.VMEM((k, k), BF16), pltpu.VMEM((states, k), BF16)],
        compiler_params=_params(("parallel", "arbitrary"), 48),
        name="s5_outputs",
    )(u, prev, lagc, cinc, spread, dsk)


def _glu_kernel(y_ref, w_ref, o_ref):
    y = y_ref[...]
    gate = jax.nn.sigmoid(jnp.dot(y.astype(BF16), w_ref[...].astype(BF16), preferred_element_type=F32))
    o_ref[...] = (y * gate).astype(BF16)


def _glu(y, w):
    m, d = y.shape
    tm = _tile(m, 1024)
    return pl.pallas_call(
        _glu_kernel,
        grid=(m // tm,),
        in_specs=[pl.BlockSpec((tm, d), lambda i: (i, 0)),
                  pl.BlockSpec((d, d), lambda i: (0, 0))],
        out_specs=pl.BlockSpec((tm, d), lambda i: (i, 0)),
        out_shape=jax.ShapeDtypeStruct((m, d), BF16),
        compiler_params=_params(("parallel",), 40),
        name="ssm_glu",
    )(y, w)


def _mix_kernel(a_ref, y_ref, wa_ref, ws_ref, ga_ref, gs_ref, o_ref):
    attn_d = jnp.dot(a_ref[...], wa_ref[...].astype(BF16), preferred_element_type=F32)
    ssm_d = jnp.dot(y_ref[...], ws_ref[...].astype(BF16), preferred_element_type=F32)
    mix = (jax.nn.sigmoid(ga_ref[...].astype(F32)) * attn_d
           + jax.nn.sigmoid(gs_ref[...].astype(F32)) * ssm_d)
    o_ref[...] = mix.astype(BF16)


def _mix(attn, y2, w_attn_o, w_ssm_o, z):
    m = attn.shape[0]
    tm, tn = _tile(m, 1024), 512
    ga_blk, gs_blk = COL_GA // tn, COL_GS // tn
    return pl.pallas_call(
        _mix_kernel,
        grid=(m // tm, D_MODEL // tn),
        in_specs=[pl.BlockSpec((tm, ATTN_WIDTH), lambda i, j: (i, 0)),
                  pl.BlockSpec((tm, SSM_WIDTH), lambda i, j: (i, 0)),
                  pl.BlockSpec((ATTN_WIDTH, tn), lambda i, j: (0, j)),
                  pl.BlockSpec((SSM_WIDTH, tn), lambda i, j: (0, j)),
                  pl.BlockSpec((tm, tn), lambda i, j: (i, j + ga_blk)),
                  pl.BlockSpec((tm, tn), lambda i, j: (i, j + gs_blk))],
        out_specs=pl.BlockSpec((tm, tn), lambda i, j: (i, j)),
        out_shape=jax.ShapeDtypeStruct((m, D_MODEL), BF16),
        compiler_params=_params(("parallel", "arbitrary"), 48),
        name="gated_merge",
    )(attn, y2, w_attn_o, w_ssm_o, z, z)


def _resid_kernel(a_ref, w_ref, h_ref, o_ref):
    o_ref[...] = DEEPNORM_ALPHA * h_ref[...] + jnp.dot(
        a_ref[...], w_ref[...].astype(BF16), preferred_element_type=F32)


def _out_proj(mix, w_out, h0):
    m = mix.shape[0]
    tm, tn = _tile(m, 1024), 512
    return pl.pallas_call(
        _resid_kernel,
        grid=(m // tm, D_MODEL // tn),
        in_specs=[pl.BlockSpec((tm, D_MODEL), lambda i, j: (i, 0)),
                  pl.BlockSpec((D_MODEL, tn), lambda i, j: (0, j)),
                  pl.BlockSpec((tm, tn), lambda i, j: (i, j))],
        out_specs=pl.BlockSpec((tm, tn), lambda i, j: (i, j)),
        out_shape=jax.ShapeDtypeStruct((m, D_MODEL), F32),
        compiler_params=_params(("parallel", "arbitrary"), 56),
        name="out_proj",
    )(mix, w_out, h0)


def _token_mixer(x, meta_tokens, ln0_g, ln0_b, w_in, attn_sinks, w_attn_o, ssm_tables_args,
                 w_glu, w_ssm_o, w_out, ln1_g, ln1_b):
    t = x.shape[0]
    lead = jnp.concatenate([jnp.zeros((N_PAD, D_MODEL), F32), meta_tokens.astype(F32)], axis=0)
    h0, h0b = _layer_norm(x, ln0_g, ln0_b)
    _, leadb = _layer_norm(lead, ln0_g, ln0_b)

    z = _project(h0b, w_in, 0, IN_WIDTH)
    z_lead = _project(leadb, w_in, COL_KV, 2 * KV_WIDTH + SSM_WIDTH)

    attn = _attention(z, z_lead, attn_sinks)

    lead_valid = (jnp.arange(BLOCK) >= N_PAD)[:, None]
    u_lead = jnp.where(lead_valid, z_lead[:, 2 * KV_WIDTH:].astype(F32), 0.0)
    tables = _ssm_tables(*ssm_tables_args, n_chunks=t // SSM_CHUNK)
    y = _ssm(z[:, COL_U:COL_GA].astype(F32), u_lead, tables)
    y2 = _glu(y, w_glu)

    mix = _mix(attn, y2, w_attn_o, w_ssm_o, z)
    r = _out_proj(mix, w_out, h0)
    return _layer_norm(r, ln1_g, ln1_b)


def _router_kernel(h_ref, wr_ref, bias_ref, idx_ref, wt_ref):
    tm = h_ref.shape[0]
    ng, ne = N_EXPERT_GROUPS, EXPERTS_PER_GROUP
    logits = lax.dot_general(wr_ref[...], h_ref[...], (((1,), (1,)), ((), ())),
                             precision=lax.Precision.HIGHEST, preferred_element_type=F32)
    scores = jax.nn.sigmoid(logits)
    sel = (scores + bias_ref[...]).reshape(ng, ne, tm)
    scores = scores.reshape(ng, ne, tm)
    e_in = lax.broadcasted_iota(jnp.int32, (ng, ne, tm), 1)
    e_id = lax.broadcasted_iota(jnp.int32, (ng, ne, tm), 0) * ne + e_in
    neg = -jnp.inf

    top1 = jnp.max(sel, axis=1, keepdims=True)
    first = jnp.min(jnp.where(sel == top1, e_in, ne), axis=1, keepdims=True)
    top2 = jnp.max(jnp.where(e_in == first, neg, sel), axis=1, keepdims=True)
    grp = (top1 + top2)[:, 0, :]

    g_id = lax.broadcasted_iota(jnp.int32, (ng, tm), 0)
    chosen = jnp.zeros((ng, tm), jnp.bool_)
    for _ in range(TOPK_GROUPS):
        best = jnp.max(grp, axis=0, keepdims=True)
        pick = g_id == jnp.min(jnp.where(grp == best, g_id, ng), axis=0, keepdims=True)
        chosen = chosen | pick
        grp = jnp.where(pick, neg, grp)

    cur = jnp.where(chosen[:, None, :], sel, NEG_INF)
    ids, wts = [], []
    for _ in range(TOP_K):
        best = jnp.max(jnp.max(cur, axis=1, keepdims=True), axis=0, keepdims=True)
        cand = jnp.where(cur == best, e_id, N_EXPERTS)
        win = jnp.min(jnp.min(cand, axis=1, keepdims=True), axis=0, keepdims=True)
        pick = e_id == win
        wsel = jnp.where(pick, scores, 0.0)
        ids.append(win[0])
        wts.append(jnp.sum(jnp.sum(wsel, axis=1, keepdims=True), axis=0, keepdims=True)[0])
        cur = jnp.where(pick, neg, cur)
    w = jnp.concatenate(wts, axis=0)
    idx_ref[...] = jnp.concatenate(ids, axis=0)
    wt_ref[...] = w / jnp.sum(w, axis=0, keepdims=True) * ROUTED_SCALE


def _router(h1, w_router, router_bias):
    t, d = h1.shape
    tm = _tile(t, 512)
    return pl.pallas_call(
        _router_kernel,
        grid=(t // tm,),
        in_specs=[pl.BlockSpec((tm, d), lambda i: (i, 0)),
                  pl.BlockSpec((N_EXPERTS, d), lambda i: (0, 0)),
                  pl.BlockSpec((N_EXPERTS, 1), lambda i: (0, 0))],
        out_specs=[pl.BlockSpec((TOP_K, tm), lambda i: (0, i)),
                   pl.BlockSpec((TOP_K, tm), lambda i: (0, i))],
        out_shape=[jax.ShapeDtypeStruct((TOP_K, t), jnp.int32),
                   jax.ShapeDtypeStruct((TOP_K, t), F32)],
        compiler_params=_params(("parallel",), 40),
        name="router",
    )(h1, w_router.T, router_bias.reshape(N_EXPERTS, 1))


def _dispatch(idx_t):
    t = idx_t.shape[1]
    n_assign = TOP_K * t
    n_blocks = -(-n_assign // ROW_BLOCK) + N_EXPERTS
    n_rows = n_blocks * ROW_BLOCK
    e_flat = idx_t.reshape(-1)
    onehot = (e_flat[:, None] == jnp.arange(N_EXPERTS, dtype=jnp.int32)[None, :]).astype(jnp.int32)
    csum = jnp.cumsum(onehot, axis=0)
    rank = jnp.take_along_axis(csum, e_flat[:, None], axis=1)[:, 0] - 1
    counts = csum[-1]
    padded = (counts + ROW_BLOCK - 1) // ROW_BLOCK * ROW_BLOCK
    ends = jnp.cumsum(padded)
    starts = ends - padded
    dest = starts[e_flat] + rank
    tok_flat = jnp.tile(jnp.arange(t, dtype=jnp.int32), TOP_K)
    row_tok = jnp.zeros((n_rows,), jnp.int32).at[dest].set(tok_flat)
    block_start = jnp.arange(n_blocks, dtype=jnp.int32) * ROW_BLOCK
    block_expert = jnp.minimum(
        jnp.sum((ends[None, :] <= block_start[:, None]).astype(jnp.int32), axis=1), N_EXPERTS - 1)
    n_used = (ends[-1] // ROW_BLOCK).astype(jnp.int32).reshape(1)
    has = counts > 0
    cand = jnp.where(has[None, :] & (jnp.arange(N_EXPERTS)[None, :] > jnp.arange(N_EXPERTS)[:, None]),
                     jnp.arange(N_EXPERTS)[None, :], N_EXPERTS)
    nxt = jnp.min(cand, axis=1)
    next_expert = jnp.where(nxt < N_EXPERTS, nxt, -1).astype(jnp.int32)
    return (row_tok.reshape(n_blocks, 1, ROW_BLOCK), block_expert, n_used, next_expert,
            dest.reshape(TOP_K, t))


WEIGHT_DMA_PRIORITY = 1


def _gather_rows(src_hbm, idx_smem, dst, sem, n, unrolled=False):
    def body(i, carry):
        pltpu.make_async_copy(src_hbm.at[pl.ds(idx_smem[0, i], 1)], dst.at[pl.ds(i, 1)], sem).start()
        return carry
    if unrolled:
        for i in range(n):
            body(i, 0)
    else:
        lax.fori_loop(0, n, body, 0, unroll=8)


def _wait_rows(src_hbm, dst, sem, n):
    pltpu.make_async_copy(src_hbm.at[pl.ds(0, n)], dst, sem).wait()


def _expert_up_kernel(be_ref, nu_ref, nx_ref, tok_hbm, h_hbm, w1_hbm, w3_hbm, o_ref,
                      idx_smem, xbuf, wstage, wcast, isem, xsem, wsem):
    b = pl.program_id(0)
    n_used = nu_ref[0]
    last = n_used - 1
    e = be_ref[b]
    slot = b % 2

    def idx_copy(blk, s):
        return pltpu.make_async_copy(tok_hbm.at[blk], idx_smem.at[s], isem.at[s])

    def weight_copies(ex):
        return (pltpu.make_async_copy(w1_hbm.at[ex], wstage.at[0], wsem.at[0]),
                pltpu.make_async_copy(w3_hbm.at[ex], wstage.at[1], wsem.at[1]))

    @pl.when(b == 0)
    def _():
        for c in weight_copies(e):
            c.start(priority=WEIGHT_DMA_PRIORITY)
        idx_copy(0, 0).start()
        idx_copy(0, 0).wait()
        _gather_rows(h_hbm, idx_smem.at[0], xbuf.at[0], xsem.at[0], ROW_BLOCK)
        idx_copy(jnp.minimum(1, last), 1).start()

    @pl.when(b < n_used)
    def _():
        first_of_expert = jnp.logical_or(b == 0, be_ref[jnp.maximum(b - 1, 0)] != e)

        @pl.when(first_of_expert)
        def _():
            for c in weight_copies(e):
                c.wait()
            rows = wstage.shape[1] // 8
            for m in range(2):
                for r in range(8):
                    wcast[m, r * rows:(r + 1) * rows] = wstage[m, r * rows:(r + 1) * rows].astype(BF16)
            nxt = nx_ref[e]

            @pl.when(nxt >= 0)
            def _():
                for c in weight_copies(nxt):
                    c.start(priority=WEIGHT_DMA_PRIORITY)

        idx_copy(jnp.minimum(b + 1, last), 1 - slot).wait()
        _wait_rows(h_hbm, xbuf.at[slot], xsem.at[slot], ROW_BLOCK)
        _gather_rows(h_hbm, idx_smem.at[1 - slot], xbuf.at[1 - slot], xsem.at[1 - slot], ROW_BLOCK,
                     unrolled=True)
        idx_copy(jnp.minimum(b + 2, last), slot).start()
        x = xbuf[slot].astype(BF16)
        a = jnp.dot(x, wcast[0], preferred_element_type=F32)
        g = jnp.dot(x, wcast[1], preferred_element_type=F32)
        o_ref[...] = (a * jax.nn.sigmoid(a) * g).astype(BF16)

        @pl.when(b == last)
        def _():
            _wait_rows(h_hbm, xbuf.at[1 - slot], xsem.at[1 - slot], ROW_BLOCK)
            idx_copy(last, slot).wait()

    @pl.when(b >= n_used)
    def _():
        o_ref[...] = jnp.zeros_like(o_ref)


def _expert_up(h1, row_tok, block_expert, n_used, next_expert, w1, w3):
    n_blocks = row_tok.shape[0]
    d = w1.shape[1]
    grid_spec = pltpu.PrefetchScalarGridSpec(
        num_scalar_prefetch=3,
        grid=(n_blocks,),
        in_specs=[pl.BlockSpec(memory_space=pl.ANY)] * 4,
        out_specs=pl.BlockSpec((ROW_BLOCK, EXPERT_DIM), lambda b, *_: (b, 0)),
        scratch_shapes=[pltpu.SMEM((2, 1, ROW_BLOCK), jnp.int32),
                        pltpu.VMEM((2, ROW_BLOCK, d), F32),
                        pltpu.VMEM((2, d, EXPERT_DIM), F32),
                        pltpu.VMEM((2, d, EXPERT_DIM), BF16),
                        pltpu.SemaphoreType.DMA((2,)),
                        pltpu.SemaphoreType.DMA((2,)),
                        pltpu.SemaphoreType.DMA((2,))])
    return pl.pallas_call(
        _expert_up_kernel,
        grid_spec=grid_spec,
        out_shape=jax.ShapeDtypeStruct((n_blocks * ROW_BLOCK, EXPERT_DIM), BF16),
        compiler_params=_params(("arbitrary",), 58),
        name="expert_up",
    )(block_expert, n_used, next_expert, row_tok, h1, w1, w3)


def _expert_down_kernel(be_ref, nu_ref, nx_ref, a_ref, w2_hbm, o_ref, wstage, wcast, wsem):
    b = pl.program_id(0)
    n_used = nu_ref[0]
    e = be_ref[b]

    def weight_copy(ex):
        return pltpu.make_async_copy(w2_hbm.at[ex], wstage, wsem)

    @pl.when(b == 0)
    def _():
        weight_copy(e).start(priority=WEIGHT_DMA_PRIORITY)

    @pl.when(b < n_used)
    def _():
        first_of_expert = jnp.logical_or(b == 0, be_ref[jnp.maximum(b - 1, 0)] != e)

        @pl.when(first_of_expert)
        def _():
            weight_copy(e).wait()
            wcast[...] = wstage[...].astype(BF16)
            nxt = nx_ref[e]

            @pl.when(nxt >= 0)
            def _():
                weight_copy(nxt).start(priority=WEIGHT_DMA_PRIORITY)

        o_ref[...] = jnp.dot(a_ref[...], wcast[...], preferred_element_type=F32)

    @pl.when(b >= n_used)
    def _():
        o_ref[...] = jnp.zeros_like(o_ref)


def _expert_down(act, block_expert, n_used, next_expert, w2):
    n_rows = act.shape[0]
    n_blocks = n_rows // ROW_BLOCK
    d = w2.shape[2]
    grid_spec = pltpu.PrefetchScalarGridSpec(
        num_scalar_prefetch=3,
        grid=(n_blocks,),
        in_specs=[pl.BlockSpec((ROW_BLOCK, EXPERT_DIM), lambda b, *_: (b, 0)),
                  pl.BlockSpec(memory_space=pl.ANY)],
        out_specs=pl.BlockSpec((ROW_BLOCK, d), lambda b, *_: (b, 0)),
        scratch_shapes=[pltpu.VMEM((EXPERT_DIM, d), F32),
                        pltpu.VMEM((EXPERT_DIM, d), BF16),
                        pltpu.SemaphoreType.DMA(())])
    return pl.pallas_call(
        _expert_down_kernel,
        grid_spec=grid_spec,
        out_shape=jax.ShapeDtypeStruct((n_rows, d), F32),
        compiler_params=_params(("arbitrary",), 40),
        name="expert_down",
    )(block_expert, n_used, next_expert, act, w2)


def _shared_kernel(h_ref, w1_ref, w3_ref, w2_ref, o_ref):
    x = h_ref[...]
    a = jnp.dot(x, w1_ref[...], preferred_element_type=F32)
    g = jnp.dot(x, w3_ref[...], preferred_element_type=F32)
    act = (a * jax.nn.sigmoid(a) * g).astype(BF16)
    o_ref[...] = jnp.dot(act, w2_ref[...], preferred_element_type=F32)


def _shared_expert(h1b, w1, w3, w2):
    t, d = h1b.shape
    tm = _tile(t, 512)
    return pl.pallas_call(
        _shared_kernel,
        grid=(t // tm,),
        in_specs=[pl.BlockSpec((tm, d), lambda i: (i, 0)),
                  pl.BlockSpec((d, SHARED_DIM), lambda i: (0, 0)),
                  pl.BlockSpec((d, SHARED_DIM), lambda i: (0, 0)),
                  pl.BlockSpec((SHARED_DIM, d), lambda i: (0, 0))],
        out_specs=pl.BlockSpec((tm, d), lambda i: (i, 0)),
        out_shape=jax.ShapeDtypeStruct((t, d), F32),
        compiler_params=_params(("parallel",), 56),
        name="shared_expert",
    )(h1b, w1.astype(BF16), w3.astype(BF16), w2.astype(BF16))


COMBINE_TOKENS = 64


def _combine_kernel(dest_hbm, y_hbm, h_ref, s_ref, w_ref, g_ref, b_ref, o_ref,
                    idx_smem, ybuf, isem, ysem):
    i = pl.program_id(0)
    n = pl.num_programs(0)
    slot = i % 2
    rows = TOP_K * COMBINE_TOKENS

    def idx_copy(blk, s):
        return pltpu.make_async_copy(dest_hbm.at[blk], idx_smem.at[s], isem.at[s])

    last = n - 1

    @pl.when(i == 0)
    def _():
        idx_copy(0, 0).start()
        idx_copy(0, 0).wait()
        _gather_rows(y_hbm, idx_smem.at[0], ybuf.at[0], ysem.at[0], rows)
        idx_copy(jnp.minimum(1, last), 1).start()

    idx_copy(jnp.minimum(i + 1, last), 1 - slot).wait()
    _wait_rows(y_hbm, ybuf.at[slot], ysem.at[slot], rows)
    _gather_rows(y_hbm, idx_smem.at[1 - slot], ybuf.at[1 - slot], ysem.at[1 - slot], rows, unrolled=True)
    idx_copy(jnp.minimum(i + 2, last), slot).start()
    acc = DEEPNORM_ALPHA * h_ref[...] + s_ref[...]
    for j in range(TOP_K):
        acc = acc + w_ref[:, j:j + 1] * ybuf[slot, j * COMBINE_TOKENS:(j + 1) * COMBINE_TOKENS, :]
    o_ref[...] = _ln_rows(acc, g_ref[...], b_ref[...])

    @pl.when(i == last)
    def _():
        _wait_rows(y_hbm, ybuf.at[1 - slot], ysem.at[1 - slot], rows)
        idx_copy(last, slot).wait()


def _combine(yb, dest, w_t, h1, shared, g, b):
    t, d = h1.shape
    tm = COMBINE_TOKENS
    n_tiles = t // tm
    dest_tiles = dest.reshape(TOP_K, n_tiles, tm).transpose(1, 0, 2).reshape(n_tiles, 1, TOP_K * tm)
    return pl.pallas_call(
        _combine_kernel,
        grid=(n_tiles,),
        in_specs=[pl.BlockSpec(memory_space=pl.ANY),
                  pl.BlockSpec(memory_space=pl.ANY),
                  pl.BlockSpec((tm, d), lambda i: (i, 0)),
                  pl.BlockSpec((tm, d), lambda i: (i, 0)),
                  pl.BlockSpec((tm, TOP_K), lambda i: (i, 0)),
                  pl.BlockSpec((1, d), lambda i: (0, 0)),
                  pl.BlockSpec((1, d), lambda i: (0, 0))],
        out_specs=pl.BlockSpec((tm, d), lambda i: (i, 0)),
        out_shape=jax.ShapeDtypeStruct((t, d), F32),
        scratch_shapes=[pltpu.SMEM((2, 1, TOP_K * tm), jnp.int32),
                        pltpu.VMEM((2, TOP_K * tm, d), F32),
                        pltpu.SemaphoreType.DMA((2,)),
                        pltpu.SemaphoreType.DMA((2,))],
        compiler_params=_params(("arbitrary",), 40),
        name="combine_norm",
    )(dest_tiles, yb, h1, shared, w_t.T, g.reshape(1, d), b.reshape(1, d))


def _channel_mixer(h1, h1b, w_router, router_bias, expert_w1, expert_w3, expert_w2,
                   shared_w1, shared_w3, shared_w2, ln2_g, ln2_b):
    idx_t, w_t = _router(h1, w_router, router_bias)
    row_tok, block_expert, n_used, next_expert, dest = _dispatch(idx_t)
    shared = _shared_expert(h1b, shared_w1, shared_w3, shared_w2)
    act = _expert_up(h1, row_tok, block_expert, n_used, next_expert, expert_w1, expert_w3)
    yb = _expert_down(act, block_expert, n_used, next_expert, expert_w2)
    return _combine(yb, dest, w_t, h1, shared, ln2_g, ln2_b)


def kernel(x, meta_tokens, ln0_g, ln0_b, w_in, attn_sinks, w_attn_o, ssm_lam_re, ssm_lam_im, ssm_log_dt,
           ssm_b_re, ssm_b_im, ssm_c_re, ssm_c_im, ssm_d, w_glu, w_ssm_o, w_out, ln1_g, ln1_b, w_router,
           router_bias, expert_w1, expert_w3, expert_w2, shared_w1, shared_w3, shared_w2, ln2_g, ln2_b):
    assert x.shape[0] == 1 and w_in.shape[0] == 1
    ssm_args = (ssm_lam_re[0], ssm_lam_im[0], ssm_log_dt[0], ssm_b_re[0], ssm_b_im[0],
                ssm_c_re[0], ssm_c_im[0], ssm_d[0])
    h1, h1b = _token_mixer(x[0], meta_tokens, ln0_g, ln0_b, w_in[0], attn_sinks[0], w_attn_o[0],
                           ssm_args, w_glu[0], w_ssm_o[0], w_out[0], ln1_g[0], ln1_b[0])
    out = _channel_mixer(h1, h1b, w_router[0], router_bias[0], expert_w1[0], expert_w3[0],
                         expert_w2[0], shared_w1[0], shared_w3[0], shared_w2[0], ln2_g[0], ln2_b[0])
    return out[None]
```

```python
import functools
import math

import jax
import jax.numpy as jnp
import numpy as np
from jax import lax
from jax.experimental import pallas as pl
from jax.experimental.pallas import tpu as pltpu

F32 = jnp.float32
BF16 = jnp.bfloat16

D_MODEL = 4096
N_META = 16
BLOCK = 128
N_PAD = BLOCK - N_META

N_HEADS = 32
N_KV_HEADS = 4
HEAD_DIM = 64
HEADS_PER_KV = N_HEADS // N_KV_HEADS
ATTN_WIDTH = N_HEADS * HEAD_DIM
KV_WIDTH = N_KV_HEADS * HEAD_DIM

SSM_WIDTH = 1024
SSM_GROUP = 16
N_SSM_GROUPS = SSM_WIDTH // SSM_GROUP
SSM_STATE = 64
SSM_CHUNK = 16
SSM_TILE_GROUPS = 128 // SSM_GROUP
SSM_TILES = N_SSM_GROUPS // SSM_TILE_GROUPS

N_EXPERTS = 64
TOP_K = 8
N_EXPERT_GROUPS = 8
EXPERTS_PER_GROUP = N_EXPERTS // N_EXPERT_GROUPS
TOPK_GROUPS = 4
EXPERT_DIM = 512
SHARED_DIM = 512
ROUTED_SCALE = 2.5
ROW_BLOCK = 256

IN_WIDTH = ATTN_WIDTH + 2 * KV_WIDTH + SSM_WIDTH + 2 * D_MODEL
COL_KV = ATTN_WIDTH
COL_U = ATTN_WIDTH + 2 * KV_WIDTH
COL_GA = COL_U + SSM_WIDTH
COL_GS = COL_GA + D_MODEL

DEEPNORM_ALPHA = 2.0 ** 0.25
LN_EPS = 1e-5
NEG_INF = -1e30

MIB = 1024 * 1024


def _params(semantics, vmem_mib):
    return pltpu.CompilerParams(dimension_semantics=semantics, vmem_limit_bytes=vmem_mib * MIB)


def _tile(n, pref):
    t = min(n, pref)
    while n % t:
        t -= 128
    return t


def _ln_rows(x, g, b):
    mu = jnp.mean(x, axis=-1, keepdims=True)
    xc = x - mu
    var = jnp.mean(xc * xc, axis=-1, keepdims=True)
    return xc * lax.rsqrt(var + LN_EPS) * g + b


def _ln_kernel(x_ref, g_ref, b_ref, o_ref, ob_ref):
    y = _ln_rows(x_ref[...], g_ref[...], b_ref[...])
    o_ref[...] = y
    ob_ref[...] = y.astype(BF16)


def _layer_norm(x, g, b):
    m, d = x.shape
    tm = _tile(m, 256)
    out_specs = [pl.BlockSpec((tm, d), lambda i: (i, 0)), pl.BlockSpec((tm, d), lambda i: (i, 0))]
    out_shape = [jax.ShapeDtypeStruct((m, d), F32), jax.ShapeDtypeStruct((m, d), BF16)]
    return pl.pallas_call(
        _ln_kernel,
        grid=(m // tm,),
        in_specs=[pl.BlockSpec((tm, d), lambda i: (i, 0)),
                  pl.BlockSpec((1, d), lambda i: (0, 0)),
                  pl.BlockSpec((1, d), lambda i: (0, 0))],
        out_specs=out_specs,
        out_shape=out_shape,
        compiler_params=_params(("parallel",), 40),
        name="layer_norm",
    )(x, g.reshape(1, d), b.reshape(1, d))


def _proj_kernel(a_ref, w_ref, o_ref):
    o_ref[...] = jnp.dot(a_ref[...], w_ref[...].astype(BF16),
                         preferred_element_type=F32).astype(o_ref.dtype)


def _project(a, w, col0, n_cols, tm_pref=1024, tn=512):
    m, k = a.shape
    tm = _tile(m, tm_pref)
    off = col0 // tn
    return pl.pallas_call(
        _proj_kernel,
        grid=(m // tm, n_cols // tn),
        in_specs=[pl.BlockSpec((tm, k), lambda i, j: (i, 0)),
                  pl.BlockSpec((k, tn), lambda i, j: (0, j + off))],
        out_specs=pl.BlockSpec((tm, tn), lambda i, j: (i, j)),
        out_shape=jax.ShapeDtypeStruct((m, n_cols), BF16),
        compiler_params=_params(("parallel", "arbitrary"), 56),
        name="in_proj",
    )(a, w)


def _attn_kernel(sink_ref, q_ref, kvc_ref, kvp_ref, kvm_ref, o_ref):
    first = pl.program_id(0) == 0
    kv_all = jnp.concatenate([kvp_ref[...], kvc_ref[...], kvm_ref[N_PAD:BLOCK, :]], axis=0)
    n_keys = 2 * BLOCK + N_META
    rows = HEADS_PER_KV * BLOCK

    lane = lax.broadcasted_iota(jnp.int32, (BLOCK, BLOCK), 1)
    lo_q = lane < HEAD_DIM
    lane_kv = lax.broadcasted_iota(jnp.int32, (n_keys, BLOCK), 1)
    lo_kv = lane_kv < HEAD_DIM

    qi = lax.broadcasted_iota(jnp.int32, (rows, n_keys), 0) & (BLOCK - 1)
    kj = lax.broadcasted_iota(jnp.int32, (rows, n_keys), 1)
    in_window = (kj > qi) & (kj <= qi + BLOCK) & ((kj >= BLOCK) | jnp.logical_not(first))
    visible = in_window | (kj >= 2 * BLOCK)
    local_head = lax.broadcasted_iota(jnp.int32, (rows, 1), 0) // BLOCK

    def dup_head(tile, odd):
        t = tile.astype(F32)
        r = pltpu.roll(t, HEAD_DIM, 1)
        keep = jnp.logical_not(lo_kv) if odd else lo_kv
        return jnp.where(keep, t, r).astype(BF16)

    for kh in range(N_KV_HEADS):
        t0 = (kh // 2) * BLOCK
        k2 = dup_head(kv_all[:, t0:t0 + BLOCK], kh % 2)
        v2 = dup_head(kv_all[:, KV_WIDTH + t0:KV_WIDTH + t0 + BLOCK], kh % 2)
        pieces = []
        for p in range(HEADS_PER_KV // 2):
            c0 = kh * HEADS_PER_KV * HEAD_DIM + p * BLOCK
            qp = q_ref[:, c0:c0 + BLOCK]
            zero = jnp.zeros_like(qp)
            pieces.append(jnp.where(lo_q, qp, zero))
            pieces.append(jnp.where(lo_q, zero, qp))
        qs = jnp.concatenate(pieces, axis=0)
        s = lax.dot_general(qs, k2, (((1,), (1,)), ((), ())), preferred_element_type=F32)
        s = jnp.where(visible, s * (HEAD_DIM ** -0.5), NEG_INF)
        sink = jnp.zeros((rows, 1), F32)
        for t in range(HEADS_PER_KV):
            sink = jnp.where(local_head == t, sink_ref[kh * HEADS_PER_KV + t], sink)
        m = jnp.maximum(jnp.max(s, axis=-1, keepdims=True), sink)
        e = jnp.exp(s - m)
        den = jnp.sum(e, axis=-1, keepdims=True) + jnp.exp(sink - m)
        o2 = jnp.dot(e.astype(BF16), v2, preferred_element_type=F32) / den
        for p in range(HEADS_PER_KV // 2):
            c0 = kh * HEADS_PER_KV * HEAD_DIM + p * BLOCK
            r0 = 2 * p * BLOCK
            o_ref[:, c0:c0 + BLOCK] = jnp.where(
                lo_q, o2[r0:r0 + BLOCK], o2[r0 + BLOCK:r0 + 2 * BLOCK]).astype(BF16)


def _attention(z, z_lead, sinks):
    t = z.shape[0]
    kv_blk = COL_KV // (2 * KV_WIDTH)
    return pl.pallas_call(
        _attn_kernel,
        grid=(t // BLOCK,),
        in_specs=[pl.BlockSpec(memory_space=pltpu.SMEM),
                  pl.BlockSpec((BLOCK, ATTN_WIDTH), lambda n: (n, 0)),
                  pl.BlockSpec((BLOCK, 2 * KV_WIDTH), lambda n: (n, kv_blk)),
                  pl.BlockSpec((BLOCK, 2 * KV_WIDTH), lambda n: (jnp.maximum(n - 1, 0), kv_blk)),
                  pl.BlockSpec((BLOCK, 2 * KV_WIDTH), lambda n: (0, 0))],
        out_specs=pl.BlockSpec((BLOCK, ATTN_WIDTH), lambda n: (n, 0)),
        out_shape=jax.ShapeDtypeStruct((t, ATTN_WIDTH), BF16),
        compiler_params=_params(("parallel",), 40),
        name="swa_attention",
    )(sinks.astype(F32), z, z, z, z_lead)


def _cmul(ar, ai, br, bi):
    return ar * br - ai * bi, ar * bi + ai * br


def _ssm_tables(lam_re, lam_im, log_dt, b_re, b_im, c_re, c_im, d_skip, n_chunks):
    hi = lax.Precision.HIGHEST
    g, p, h, c = N_SSM_GROUPS, SSM_STATE, SSM_GROUP, SSM_CHUNK
    dt = jnp.exp(log_dt)[:, None]
    mag = jnp.exp(dt * lam_re)
    are = mag * jnp.cos(dt * lam_im)
    aim = mag * jnp.sin(dt * lam_im)
    den = lam_re * lam_re + lam_im * lam_im
    num_re = are - 1.0
    coef_re = (num_re * lam_re + aim * lam_im) / den
    coef_im = (aim * lam_re - num_re * lam_im) / den
    bb_re = coef_re[..., None] * b_re - coef_im[..., None] * b_im
    bb_im = coef_re[..., None] * b_im + coef_im[..., None] * b_re

    pr, pi = [jnp.ones_like(are)], [jnp.zeros_like(are)]
    for _ in range(c):
        r, i = _cmul(pr[-1], pi[-1], are, aim)
        pr.append(r)
        pi.append(i)
    pr, pi = jnp.stack(pr), jnp.stack(pi)

    m_re = pr[:c, :, :, None] * bb_re[None] - pi[:c, :, :, None] * bb_im[None]
    m_im = pr[:c, :, :, None] * bb_im[None] + pi[:c, :, :, None] * bb_re[None]
    lag = (jnp.einsum('ghp,kgpj->kghj', c_re, m_re, precision=hi)
           - jnp.einsum('ghp,kgpj->kghj', c_im, m_im, precision=hi))
    nt, gt = SSM_TILES, SSM_TILE_GROUPS
    lagc = lag.reshape(c, nt, gt, h, h).transpose(1, 0, 2, 4, 3)
    lagc = lagc.reshape(nt, c, gt * h, h)
    rev = np.arange(c - 1, -1, -1)
    wo_re = pr[rev][:, :, :, None] * bb_re[None] - pi[rev][:, :, :, None] * bb_im[None]
    wo_im = pr[rev][:, :, :, None] * bb_im[None] + pi[rev][:, :, :, None] * bb_re[None]

    def lanes(first, second):
        w = jnp.stack([first, second]).reshape(2, c, nt, gt, p, h).transpose(2, 1, 3, 5, 0, 4)
        return w.reshape(nt, c, gt * h, 2 * p)

    wout_a, wout_b = lanes(wo_re, wo_im), lanes(wo_im, wo_re)
    cin_re = c_re[None] * pr[1:, :, None, :] - c_im[None] * pi[1:, :, None, :]
    cin_im = -(c_re[None] * pi[1:, :, None, :] + c_im[None] * pr[1:, :, None, :])
    cinc = jnp.stack([cin_re, cin_im]).reshape(2, c, nt, gt, h, p)
    cinc = cinc.transpose(2, 1, 0, 3, 5, 4).reshape(nt, c, 2 * gt * p, h)
    spread = jnp.tile(jnp.eye(h, dtype=BF16), (1, gt))

    n_steps = max(1, math.ceil(math.log2(n_chunks)))
    qr, qi = [pr[c]], [pi[c]]
    for _ in range(n_steps - 1):
        r, i = _cmul(qr[-1], qi[-1], qr[-1], qi[-1])
        qr.append(r)
        qi.append(i)
    apr = jnp.stack(qr, axis=0).reshape(n_steps, nt, gt * p).transpose(1, 0, 2)
    api = jnp.stack(qi, axis=0).reshape(n_steps, nt, gt * p).transpose(1, 0, 2)
    dsk = d_skip.reshape(nt, 1, gt * h)
    return (lagc.astype(BF16), wout_a.astype(BF16), wout_b.astype(BF16), cinc.astype(BF16), spread,
            apr, api, dsk, n_steps)


def _chunk_steps(u_ref):
    n = u_ref.shape[0] // SSM_CHUNK
    return [u_ref[pl.ds(s, n, stride=SSM_CHUNK), :] for s in range(SSM_CHUNK)]


def _chunk_lanes(u_ref):
    return jnp.concatenate(_chunk_steps(u_ref), axis=1)


def _ssm_state_kernel(u_ref, lead_ref, wa_ref, wb_ref, apr_ref, api_ref, prev_ref, wout_s, *, n_steps):
    p, gh = SSM_STATE, SSM_TILE_GROUPS * SSM_GROUP
    row_g = lax.broadcasted_iota(jnp.int32, (gh, 2 * p), 0) // SSM_GROUP
    lo = lax.broadcasted_iota(jnp.int32, (gh, 2 * p), 1) < p
    zero = jnp.zeros((gh, 2 * p), BF16)
    for s in range(SSM_CHUNK):
        a, b = wa_ref[0, s], wb_ref[0, s]
        for j in range(SSM_TILE_GROUPS // 2):
            even, odd = (row_g == 2 * j) & lo, (row_g == 2 * j + 1) & jnp.logical_not(lo)
            wout_s[s * gh:(s + 1) * gh, j * 2 * p:(j + 1) * 2 * p] = (
                jnp.where(even, a, jnp.where(odd, b, zero)))
            wout_s[s * gh:(s + 1) * gh, (SSM_TILE_GROUPS // 2 + j) * 2 * p:(SSM_TILE_GROUPS // 2 + j + 1) * 2 * p] = (
                jnp.where(even, b, jnp.where(odd, a, zero)))
    wout = wout_s[...]
    st = jnp.dot(_chunk_lanes(u_ref).astype(BF16), wout, preferred_element_type=F32)
    lead = jnp.dot(_chunk_lanes(lead_ref).astype(BF16), wout, preferred_element_type=F32)
    nc, half = st.shape[0], st.shape[1] // 2
    xr, xi = st[:, :half], st[:, half:]
    mr, mi = lead[-1:, :half], lead[-1:, half:]
    row = lax.broadcasted_iota(jnp.int32, (nc, half), 0)
    ar, ai = apr_ref[0, 0:1, :], api_ref[0, 0:1, :]
    xr = xr + jnp.where(row == 0, ar * mr - ai * mi, 0.0)
    xi = xi + jnp.where(row == 0, ar * mi + ai * mr, 0.0)

    def shift(x, d):
        return jnp.where(row >= d, pltpu.roll(x, d, 0), 0.0)

    for j in range(n_steps):
        ar, ai = apr_ref[0, j:j + 1, :], api_ref[0, j:j + 1, :]
        sr, si = shift(xr, 1 << j), shift(xi, 1 << j)
        xr, xi = xr + (ar * sr - ai * si), xi + (ar * si + ai * sr)
    pr = jnp.where(row == 0, mr, shift(xr, 1))
    pi = jnp.where(row == 0, mi, shift(xi, 1))
    prev_ref[0] = jnp.concatenate([pr, pi], axis=1).astype(BF16)


def _ssm_out_kernel(u_ref, prev_ref, lag_ref, cin_ref, spread_ref, d_ref, y_ref, toep_s, wcin_s):
    gh, c = SSM_TILE_GROUPS * SSM_GROUP, SSM_CHUNK

    @pl.when(pl.program_id(1) == 0)
    def _():
        spread = spread_ref[...]
        diag = (lax.broadcasted_iota(jnp.int32, (gh, gh), 0) // SSM_GROUP
                == lax.broadcasted_iota(jnp.int32, (gh, gh), 1) // SSM_GROUP)
        zero = jnp.zeros((gh, gh), BF16)
        blocks = [jnp.where(diag, jnp.dot(lag_ref[0, k], spread, preferred_element_type=F32), 0.0).astype(BF16)
                  for k in range(c)]
        for s in range(c):
            for i in range(c):
                toep_s[s * gh:(s + 1) * gh, i * gh:(i + 1) * gh] = blocks[i - s] if i >= s else zero
        rows = wcin_s.shape[0]
        row_g = (lax.broadcasted_iota(jnp.int32, (rows, gh), 0) % (rows // 2)) // SSM_STATE
        own = row_g == lax.broadcasted_iota(jnp.int32, (rows, gh), 1) // SSM_GROUP
        for i in range(c):
            wcin_s[:, i * gh:(i + 1) * gh] = jnp.where(
                own, jnp.dot(cin_ref[0, i], spread, preferred_element_type=F32), 0.0).astype(BF16)

    us = _chunk_steps(u_ref)
    cb = us[0].shape[0]
    u = jnp.concatenate(us, axis=1).astype(BF16)
    y = (jnp.dot(u, toep_s[...], preferred_element_type=F32)
         + jnp.dot(prev_ref[0], wcin_s[...], preferred_element_type=F32))
    for s in range(SSM_CHUNK):
        ys = y[:, s * 128:(s + 1) * 128] + d_ref[0] * us[s]
        y_ref[pl.ds(s, cb, stride=SSM_CHUNK), :] = 0.5 * ys * (1.0 + lax.erf(ys * (0.5 ** 0.5)))


def _ssm(u, u_lead, tables):
    lagc, wout_a, wout_b, cinc, spread, apr, api, dsk, n_steps = tables
    t = u.shape[0]
    nc = t // SSM_CHUNK
    cb = _tile(nc, 256) if nc % 128 == 0 else nc
    lanes = 128
    k = SSM_CHUNK * lanes
    states = 2 * SSM_TILE_GROUPS * SSM_STATE
    prev = pl.pallas_call(
        functools.partial(_ssm_state_kernel, n_steps=n_steps),
        grid=(SSM_TILES,),
        in_specs=[pl.BlockSpec((t, lanes), lambda g: (0, g)),
                  pl.BlockSpec((BLOCK, lanes), lambda g: (0, g)),
                  pl.BlockSpec((1, SSM_CHUNK, lanes, 2 * SSM_STATE), lambda g: (g, 0, 0, 0)),
                  pl.BlockSpec((1, SSM_CHUNK, lanes, 2 * SSM_STATE), lambda g: (g, 0, 0, 0)),
                  pl.BlockSpec((1, n_steps, states // 2), lambda g: (g, 0, 0)),
                  pl.BlockSpec((1, n_steps, states // 2), lambda g: (g, 0, 0))],
        out_specs=pl.BlockSpec((1, nc, states), lambda g: (g, 0, 0)),
        out_shape=jax.ShapeDtypeStruct((SSM_TILES, nc, states), BF16),
        scratch_shapes=[pltpu.VMEM((k, states), BF16)],
        compiler_params=_params(("parallel",), 48),
        name="s5_states",
    )(u, u_lead, wout_a, wout_b, apr, api)
    return pl.pallas_call(
        _ssm_out_kernel,
        grid=(SSM_TILES, nc // cb),
        in_specs=[pl.BlockSpec((cb * SSM_CHUNK, lanes), lambda g, c: (c, g)),
                  pl.BlockSpec((1, cb, states), lambda g, c: (g, c, 0)),
                  pl.BlockSpec((1, SSM_CHUNK, lanes, SSM_GROUP), lambda g, c: (g, 0, 0, 0)),
                  pl.BlockSpec((1, SSM_CHUNK, states, SSM_GROUP), lambda g, c: (g, 0, 0, 0)),
                  pl.BlockSpec((SSM_GROUP, lanes), lambda g, c: (0, 0)),
                  pl.BlockSpec((1, 1, lanes), lambda g, c: (g, 0, 0))],
        out_specs=pl.BlockSpec((cb * SSM_CHUNK, lanes), lambda g, c: (c, g)),
        out_shape=jax.ShapeDtypeStruct((t, SSM_WIDTH), F32),
        scratch_shapes=[pltpu.VMEM((k, k), BF16), pltpu.VMEM((states, k), BF16)],
        compiler_params=_params(("parallel", "arbitrary"), 48),
        name="s5_outputs",
    )(u, prev, lagc, cinc, spread, dsk)


def _glu_kernel(y_ref, w_ref, o_ref):
    y = y_ref[...]
    gate = jax.nn.sigmoid(jnp.dot(y.astype(BF16), w_ref[...].astype(BF16), preferred_element_type=F32))
    o_ref[...] = (y * gate).astype(BF16)


def _glu(y, w):
    m, d = y.shape
    tm = _tile(m, 1024)
    return pl.pallas_call(
        _glu_kernel,
        grid=(m // tm,),
        in_specs=[pl.BlockSpec((tm, d), lambda i: (i, 0)),
                  pl.BlockSpec((d, d), lambda i: (0, 0))],
        out_specs=pl.BlockSpec((tm, d), lambda i: (i, 0)),
        out_shape=jax.ShapeDtypeStruct((m, d), BF16),
        compiler_params=_params(("parallel",), 40),
        name="ssm_glu",
    )(y, w)


def _mix_kernel(a_ref, y_ref, wa_ref, ws_ref, ga_ref, gs_ref, o_ref):
    attn_d = jnp.dot(a_ref[...], wa_ref[...].astype(BF16), preferred_element_type=F32)
    ssm_d = jnp.dot(y_ref[...], ws_ref[...].astype(BF16), preferred_element_type=F32)
    mix = (jax.nn.sigmoid(ga_ref[...].astype(F32)) * attn_d
           + jax.nn.sigmoid(gs_ref[...].astype(F32)) * ssm_d)
    o_ref[...] = mix.astype(BF16)


def _mix(attn, y2, w_attn_o, w_ssm_o, z):
    m = attn.shape[0]
    tm, tn = _tile(m, 1024), 512
    ga_blk, gs_blk = COL_GA // tn, COL_GS // tn
    return pl.pallas_call(
        _mix_kernel,
        grid=(m // tm, D_MODEL // tn),
        in_specs=[pl.BlockSpec((tm, ATTN_WIDTH), lambda i, j: (i, 0)),
                  pl.BlockSpec((tm, SSM_WIDTH), lambda i, j: (i, 0)),
                  pl.BlockSpec((ATTN_WIDTH, tn), lambda i, j: (0, j)),
                  pl.BlockSpec((SSM_WIDTH, tn), lambda i, j: (0, j)),
                  pl.BlockSpec((tm, tn), lambda i, j: (i, j + ga_blk)),
                  pl.BlockSpec((tm, tn), lambda i, j: (i, j + gs_blk))],
        out_specs=pl.BlockSpec((tm, tn), lambda i, j: (i, j)),
        out_shape=jax.ShapeDtypeStruct((m, D_MODEL), BF16),
        compiler_params=_params(("parallel", "arbitrary"), 48),
        name="gated_merge",
    )(attn, y2, w_attn_o, w_ssm_o, z, z)


def _resid_kernel(a_ref, w_ref, h_ref, o_ref):
    o_ref[...] = DEEPNORM_ALPHA * h_ref[...] + jnp.dot(
        a_ref[...], w_ref[...].astype(BF16), preferred_element_type=F32)


def _out_proj(mix, w_out, h0):
    m = mix.shape[0]
    tm, tn = _tile(m, 1024), 512
    return pl.pallas_call(
        _resid_kernel,
        grid=(m // tm, D_MODEL // tn),
        in_specs=[pl.BlockSpec((tm, D_MODEL), lambda i, j: (i, 0)),
                  pl.BlockSpec((D_MODEL, tn), lambda i, j: (0, j)),
                  pl.BlockSpec((tm, tn), lambda i, j: (i, j))],
        out_specs=pl.BlockSpec((tm, tn), lambda i, j: (i, j)),
        out_shape=jax.ShapeDtypeStruct((m, D_MODEL), F32),
        compiler_params=_params(("parallel", "arbitrary"), 56),
        name="out_proj",
    )(mix, w_out, h0)


def _token_mixer(x, meta_tokens, ln0_g, ln0_b, w_in, attn_sinks, w_attn_o, ssm_tables_args,
                 w_glu, w_ssm_o, w_out, ln1_g, ln1_b):
    t = x.shape[0]
    lead = jnp.concatenate([jnp.zeros((N_PAD, D_MODEL), F32), meta_tokens.astype(F32)], axis=0)
    h0, h0b = _layer_norm(x, ln0_g, ln0_b)
    _, leadb = _layer_norm(lead, ln0_g, ln0_b)

    z = _project(h0b, w_in, 0, IN_WIDTH)
    z_lead = _project(leadb, w_in, COL_KV, 2 * KV_WIDTH + SSM_WIDTH)

    attn = _attention(z, z_lead, attn_sinks)

    lead_valid = (jnp.arange(BLOCK) >= N_PAD)[:, None]
    u_lead = jnp.where(lead_valid, z_lead[:, 2 * KV_WIDTH:].astype(F32), 0.0)
    tables = _ssm_tables(*ssm_tables_args, n_chunks=t // SSM_CHUNK)
    y = _ssm(z[:, COL_U:COL_GA].astype(F32), u_lead, tables)
    y2 = _glu(y, w_glu)

    mix = _mix(attn, y2, w_attn_o, w_ssm_o, z)
    r = _out_proj(mix, w_out, h0)
    return _layer_norm(r, ln1_g, ln1_b)


def _router_kernel(h_ref, wr_ref, bias_ref, idx_ref, wt_ref):
    tm = h_ref.shape[0]
    ng, ne = N_EXPERT_GROUPS, EXPERTS_PER_GROUP
    logits = lax.dot_general(wr_ref[...], h_ref[...], (((1,), (1,)), ((), ())),
                             precision=lax.Precision.HIGHEST, preferred_element_type=F32)
    scores = jax.nn.sigmoid(logits)
    sel = (scores + bias_ref[...]).reshape(ng, ne, tm)
    scores = scores.reshape(ng, ne, tm)
    e_in = lax.broadcasted_iota(jnp.int32, (ng, ne, tm), 1)
    e_id = lax.broadcasted_iota(jnp.int32, (ng, ne, tm), 0) * ne + e_in
    neg = -jnp.inf

    top1 = jnp.max(sel, axis=1, keepdims=True)
    first = jnp.min(jnp.where(sel == top1, e_in, ne), axis=1, keepdims=True)
    top2 = jnp.max(jnp.where(e_in == first, neg, sel), axis=1, keepdims=True)
    grp = (top1 + top2)[:, 0, :]

    g_id = lax.broadcasted_iota(jnp.int32, (ng, tm), 0)
    chosen = jnp.zeros((ng, tm), jnp.bool_)
    for _ in range(TOPK_GROUPS):
        best = jnp.max(grp, axis=0, keepdims=True)
        pick = g_id == jnp.min(jnp.where(grp == best, g_id, ng), axis=0, keepdims=True)
        chosen = chosen | pick
        grp = jnp.where(pick, neg, grp)

    cur = jnp.where(chosen[:, None, :], sel, NEG_INF)
    ids, wts = [], []
    for _ in range(TOP_K):
        best = jnp.max(jnp.max(cur, axis=1, keepdims=True), axis=0, keepdims=True)
        cand = jnp.where(cur == best, e_id, N_EXPERTS)
        win = jnp.min(jnp.min(cand, axis=1, keepdims=True), axis=0, keepdims=True)
        pick = e_id == win
        wsel = jnp.where(pick, scores, 0.0)
        ids.append(win[0])
        wts.append(jnp.sum(jnp.sum(wsel, axis=1, keepdims=True), axis=0, keepdims=True)[0])
        cur = jnp.where(pick, neg, cur)
    w = jnp.concatenate(wts, axis=0)
    idx_ref[...] = jnp.concatenate(ids, axis=0)
    wt_ref[...] = w / jnp.sum(w, axis=0, keepdims=True) * ROUTED_SCALE


def _router(h1, w_router, router_bias):
    t, d = h1.shape
    tm = _tile(t, 512)
    return pl.pallas_call(
        _router_kernel,
        grid=(t // tm,),
        in_specs=[pl.BlockSpec((tm, d), lambda i: (i, 0)),
                  pl.BlockSpec((N_EXPERTS, d), lambda i: (0, 0)),
                  pl.BlockSpec((N_EXPERTS, 1), lambda i: (0, 0))],
        out_specs=[pl.BlockSpec((TOP_K, tm), lambda i: (0, i)),
                   pl.BlockSpec((TOP_K, tm), lambda i: (0, i))],
        out_shape=[jax.ShapeDtypeStruct((TOP_K, t), jnp.int32),
                   jax.ShapeDtypeStruct((TOP_K, t), F32)],
        compiler_params=_params(("parallel",), 40),
        name="router",
    )(h1, w_router.T, router_bias.reshape(N_EXPERTS, 1))


def _dispatch(idx_t):
    t = idx_t.shape[1]
    n_assign = TOP_K * t
    n_blocks = -(-n_assign // ROW_BLOCK) + N_EXPERTS
    n_rows = n_blocks * ROW_BLOCK
    e_flat = idx_t.reshape(-1)
    onehot = (e_flat[:, None] == jnp.arange(N_EXPERTS, dtype=jnp.int32)[None, :]).astype(jnp.int32)
    csum = jnp.cumsum(onehot, axis=0)
    rank = jnp.take_along_axis(csum, e_flat[:, None], axis=1)[:, 0] - 1
    counts = csum[-1]
    padded = (counts + ROW_BLOCK - 1) // ROW_BLOCK * ROW_BLOCK
    ends = jnp.cumsum(padded)
    starts = ends - padded
    dest = starts[e_flat] + rank
    tok_flat = jnp.tile(jnp.arange(t, dtype=jnp.int32), TOP_K)
    row_tok = jnp.zeros((n_rows,), jnp.int32).at[dest].set(tok_flat)
    block_start = jnp.arange(n_blocks, dtype=jnp.int32) * ROW_BLOCK
    block_expert = jnp.minimum(
        jnp.sum((ends[None, :] <= block_start[:, None]).astype(jnp.int32), axis=1), N_EXPERTS - 1)
    n_used = (ends[-1] // ROW_BLOCK).astype(jnp.int32).reshape(1)
    has = counts > 0
    cand = jnp.where(has[None, :] & (jnp.arange(N_EXPERTS)[None, :] > jnp.arange(N_EXPERTS)[:, None]),
                     jnp.arange(N_EXPERTS)[None, :], N_EXPERTS)
    nxt = jnp.min(cand, axis=1)
    next_expert = jnp.where(nxt < N_EXPERTS, nxt, -1).astype(jnp.int32)
    return (row_tok.reshape(n_blocks, 1, ROW_BLOCK), block_expert, n_used, next_expert,
            dest.reshape(TOP_K, t))


WEIGHT_DMA_PRIORITY = 1


def _gather_rows(src_hbm, idx_smem, dst, sem, n, unrolled=False):
    def body(i, carry):
        pltpu.make_async_copy(src_hbm.at[pl.ds(idx_smem[0, i], 1)], dst.at[pl.ds(i, 1)], sem).start()
        return carry
    if unrolled:
        for i in range(n):
            body(i, 0)
    else:
        lax.fori_loop(0, n, body, 0, unroll=8)


def _wait_rows(src_hbm, dst, sem, n):
    pltpu.make_async_copy(src_hbm.at[pl.ds(0, n)], dst, sem).wait()


LANES = 128
SLAB = D_MODEL // LANES
SLAB_PITCH = SLAB + 4


def _gather_slabs(src_hbm, idx_smem, dst, sem, n, unrolled=False):
    def body(i, carry):
        src = pl.multiple_of(idx_smem[0, i] * SLAB, SLAB)
        pltpu.make_async_copy(src_hbm.at[pl.ds(src, SLAB)], dst.at[pl.ds(i * SLAB_PITCH, SLAB)], sem).start()
        return carry
    if unrolled:
        for i in range(n):
            body(i, 0)
    else:
        lax.fori_loop(0, n, body, 0, unroll=8)


def _wait_slabs(src_hbm, dst, sem, n):
    pltpu.make_async_copy(src_hbm.at[pl.ds(0, n * SLAB)], dst.at[pl.ds(0, n * SLAB)], sem).wait()


def _slab_rows(ref, n):
    return jnp.concatenate([ref[pl.ds(s, n, stride=SLAB_PITCH), :] for s in range(SLAB)], axis=1)


def _expert_up_kernel(be_ref, nu_ref, nx_ref, tok_hbm, h_hbm, w1_hbm, w3_hbm, o_ref,
                      idx_smem, xbuf, wstage, wcast, isem, xsem, wsem):
    b = pl.program_id(0)
    n_used = nu_ref[0]
    last = n_used - 1
    e = be_ref[b]
    slot = b % 2

    def idx_copy(blk, s):
        return pltpu.make_async_copy(tok_hbm.at[blk], idx_smem.at[s], isem.at[s])

    def weight_copies(ex):
        return (pltpu.make_async_copy(w1_hbm.at[ex], wstage.at[0], wsem.at[0]),
                pltpu.make_async_copy(w3_hbm.at[ex], wstage.at[1], wsem.at[1]))

    @pl.when(b == 0)
    def _():
        for c in weight_copies(e):
            c.start(priority=WEIGHT_DMA_PRIORITY)
        idx_copy(0, 0).start()
        idx_copy(0, 0).wait()
        _gather_slabs(h_hbm, idx_smem.at[0], xbuf.at[0], xsem.at[0], ROW_BLOCK)
        idx_copy(jnp.minimum(1, last), 1).start()

    @pl.when(b < n_used)
    def _():
        first_of_expert = jnp.logical_or(b == 0, be_ref[jnp.maximum(b - 1, 0)] != e)

        @pl.when(first_of_expert)
        def _():
            for c in weight_copies(e):
                c.wait()
            rows = wstage.shape[1] // 8
            for m in range(2):
                for r in range(8):
                    wcast[m, r * rows:(r + 1) * rows] = wstage[m, r * rows:(r + 1) * rows].astype(BF16)
            nxt = nx_ref[e]

            @pl.when(nxt >= 0)
            def _():
                for c in weight_copies(nxt):
                    c.start(priority=WEIGHT_DMA_PRIORITY)

        idx_copy(jnp.minimum(b + 1, last), 1 - slot).wait()
        _wait_slabs(h_hbm, xbuf.at[slot], xsem.at[slot], ROW_BLOCK)
        _gather_slabs(h_hbm, idx_smem.at[1 - slot], xbuf.at[1 - slot], xsem.at[1 - slot], ROW_BLOCK,
                      unrolled=True)
        idx_copy(jnp.minimum(b + 2, last), slot).start()
        x = _slab_rows(xbuf.at[slot], ROW_BLOCK).astype(BF16)
        a = jnp.dot(x, wcast[0], preferred_element_type=F32)
        g = jnp.dot(x, wcast[1], preferred_element_type=F32)
        o_ref[...] = (a * jax.nn.sigmoid(a) * g).astype(BF16)

        @pl.when(b == last)
        def _():
            _wait_slabs(h_hbm, xbuf.at[1 - slot], xsem.at[1 - slot], ROW_BLOCK)
            idx_copy(last, slot).wait()

    @pl.when(b >= n_used)
    def _():
        o_ref[...] = jnp.zeros_like(o_ref)


def _expert_up(h1, row_tok, block_expert, n_used, next_expert, w1, w3):
    n_blocks = row_tok.shape[0]
    d = w1.shape[1]
    grid_spec = pltpu.PrefetchScalarGridSpec(
        num_scalar_prefetch=3,
        grid=(n_blocks,),
        in_specs=[pl.BlockSpec(memory_space=pl.ANY)] * 4,
        out_specs=pl.BlockSpec((ROW_BLOCK, EXPERT_DIM), lambda b, *_: (b, 0)),
        scratch_shapes=[pltpu.SMEM((2, 1, ROW_BLOCK), jnp.int32),
                        pltpu.VMEM((2, ROW_BLOCK * SLAB_PITCH, LANES), F32),
                        pltpu.VMEM((2, d, EXPERT_DIM), F32),
                        pltpu.VMEM((2, d, EXPERT_DIM), BF16),
                        pltpu.SemaphoreType.DMA((2,)),
                        pltpu.SemaphoreType.DMA((2,)),
                        pltpu.SemaphoreType.DMA((2,))])
    return pl.pallas_call(
        _expert_up_kernel,
        grid_spec=grid_spec,
        out_shape=jax.ShapeDtypeStruct((n_blocks * ROW_BLOCK, EXPERT_DIM), BF16),
        compiler_params=_params(("arbitrary",), 58),
        name="expert_up",
    )(block_expert, n_used, next_expert, row_tok, h1, w1, w3)


def _expert_down_kernel(be_ref, nu_ref, nx_ref, a_ref, w2_hbm, o_ref, wstage, wcast, wsem):
    b = pl.program_id(0)
    n_used = nu_ref[0]
    e = be_ref[b]

    def weight_copy(ex):
        return pltpu.make_async_copy(w2_hbm.at[ex], wstage, wsem)

    @pl.when(b == 0)
    def _():
        weight_copy(e).start(priority=WEIGHT_DMA_PRIORITY)

    @pl.when(b < n_used)
    def _():
        first_of_expert = jnp.logical_or(b == 0, be_ref[jnp.maximum(b - 1, 0)] != e)

        @pl.when(first_of_expert)
        def _():
            weight_copy(e).wait()
            wcast[...] = wstage[...].astype(BF16)
            nxt = nx_ref[e]

            @pl.when(nxt >= 0)
            def _():
                weight_copy(nxt).start(priority=WEIGHT_DMA_PRIORITY)

        o_ref[...] = jnp.dot(a_ref[...], wcast[...], preferred_element_type=F32)

    @pl.when(b >= n_used)
    def _():
        o_ref[...] = jnp.zeros_like(o_ref)


def _expert_down(act, block_expert, n_used, next_expert, w2):
    n_rows = act.shape[0]
    n_blocks = n_rows // ROW_BLOCK
    d = w2.shape[2]
    grid_spec = pltpu.PrefetchScalarGridSpec(
        num_scalar_prefetch=3,
        grid=(n_blocks,),
        in_specs=[pl.BlockSpec((ROW_BLOCK, EXPERT_DIM), lambda b, *_: (b, 0)),
                  pl.BlockSpec(memory_space=pl.ANY)],
        out_specs=pl.BlockSpec((ROW_BLOCK, d), lambda b, *_: (b, 0)),
        scratch_shapes=[pltpu.VMEM((EXPERT_DIM, d), F32),
                        pltpu.VMEM((EXPERT_DIM, d), BF16),
                        pltpu.SemaphoreType.DMA(())])
    return pl.pallas_call(
        _expert_down_kernel,
        grid_spec=grid_spec,
        out_shape=jax.ShapeDtypeStruct((n_rows, d), F32),
        compiler_params=_params(("arbitrary",), 40),
        name="expert_down",
    )(block_expert, n_used, next_expert, act, w2)


def _shared_kernel(h_ref, w1_ref, w3_ref, w2_ref, o_ref):
    x = h_ref[...]
    a = jnp.dot(x, w1_ref[...], preferred_element_type=F32)
    g = jnp.dot(x, w3_ref[...], preferred_element_type=F32)
    act = (a * jax.nn.sigmoid(a) * g).astype(BF16)
    o_ref[...] = jnp.dot(act, w2_ref[...], preferred_element_type=F32)


def _shared_expert(h1b, w1, w3, w2):
    t, d = h1b.shape
    tm = _tile(t, 512)
    return pl.pallas_call(
        _shared_kernel,
        grid=(t // tm,),
        in_specs=[pl.BlockSpec((tm, d), lambda i: (i, 0)),
                  pl.BlockSpec((d, SHARED_DIM), lambda i: (0, 0)),
                  pl.BlockSpec((d, SHARED_DIM), lambda i: (0, 0)),
                  pl.BlockSpec((SHARED_DIM, d), lambda i: (0, 0))],
        out_specs=pl.BlockSpec((tm, d), lambda i: (i, 0)),
        out_shape=jax.ShapeDtypeStruct((t, d), F32),
        compiler_params=_params(("parallel",), 56),
        name="shared_expert",
    )(h1b, w1.astype(BF16), w3.astype(BF16), w2.astype(BF16))


COMBINE_TOKENS = 64


def _combine_kernel(dest_hbm, y_hbm, h_ref, s_ref, w_ref, g_ref, b_ref, o_ref,
                    idx_smem, ybuf, isem, ysem):
    i = pl.program_id(0)
    n = pl.num_programs(0)
    slot = i % 2
    rows = TOP_K * COMBINE_TOKENS

    def idx_copy(blk, s):
        return pltpu.make_async_copy(dest_hbm.at[blk], idx_smem.at[s], isem.at[s])

    last = n - 1

    @pl.when(i == 0)
    def _():
        idx_copy(0, 0).start()
        idx_copy(0, 0).wait()
        _gather_rows(y_hbm, idx_smem.at[0], ybuf.at[0], ysem.at[0], rows)
        idx_copy(jnp.minimum(1, last), 1).start()

    idx_copy(jnp.minimum(i + 1, last), 1 - slot).wait()
    _wait_rows(y_hbm, ybuf.at[slot], ysem.at[slot], rows)
    _gather_rows(y_hbm, idx_smem.at[1 - slot], ybuf.at[1 - slot], ysem.at[1 - slot], rows, unrolled=True)
    idx_copy(jnp.minimum(i + 2, last), slot).start()
    acc = DEEPNORM_ALPHA * h_ref[...] + s_ref[...]
    for j in range(TOP_K):
        acc = acc + w_ref[:, j:j + 1] * ybuf[slot, j * COMBINE_TOKENS:(j + 1) * COMBINE_TOKENS, :]
    o_ref[...] = _ln_rows(acc, g_ref[...], b_ref[...])

    @pl.when(i == last)
    def _():
        _wait_rows(y_hbm, ybuf.at[1 - slot], ysem.at[1 - slot], rows)
        idx_copy(last, slot).wait()


def _combine(yb, dest, w_t, h1, shared, g, b):
    t, d = h1.shape
    tm = COMBINE_TOKENS
    n_tiles = t // tm
    dest_tiles = dest.reshape(TOP_K, n_tiles, tm).transpose(1, 0, 2).reshape(n_tiles, 1, TOP_K * tm)
    return pl.pallas_call(
        _combine_kernel,
        grid=(n_tiles,),
        in_specs=[pl.BlockSpec(memory_space=pl.ANY),
                  pl.BlockSpec(memory_space=pl.ANY),
                  pl.BlockSpec((tm, d), lambda i: (i, 0)),
                  pl.BlockSpec((tm, d), lambda i: (i, 0)),
                  pl.BlockSpec((tm, TOP_K), lambda i: (i, 0)),
                  pl.BlockSpec((1, d), lambda i: (0, 0)),
                  pl.BlockSpec((1, d), lambda i: (0, 0))],
        out_specs=pl.BlockSpec((tm, d), lambda i: (i, 0)),
        out_shape=jax.ShapeDtypeStruct((t, d), F32),
        scratch_shapes=[pltpu.SMEM((2, 1, TOP_K * tm), jnp.int32),
                        pltpu.VMEM((2, TOP_K * tm, d), F32),
                        pltpu.SemaphoreType.DMA((2,)),
                        pltpu.SemaphoreType.DMA((2,))],
        compiler_params=_params(("arbitrary",), 40),
        name="combine_norm",
    )(dest_tiles, yb, h1, shared, w_t.T, g.reshape(1, d), b.reshape(1, d))


def _channel_mixer(h1, h1b, w_router, router_bias, expert_w1, expert_w3, expert_w2,
                   shared_w1, shared_w3, shared_w2, ln2_g, ln2_b):
    idx_t, w_t = _router(h1, w_router, router_bias)
    row_tok, block_expert, n_used, next_expert, dest = _dispatch(idx_t)
    shared = _shared_expert(h1b, shared_w1, shared_w3, shared_w2)
    h1_slabs = h1.reshape(h1.shape[0] * SLAB, LANES)
    act = _expert_up(h1_slabs, row_tok, block_expert, n_used, next_expert, expert_w1, expert_w3)
    yb = _expert_down(act, block_expert, n_used, next_expert, expert_w2)
    return _combine(yb, dest, w_t, h1, shared, ln2_g, ln2_b)


def kernel(x, meta_tokens, ln0_g, ln0_b, w_in, attn_sinks, w_attn_o, ssm_lam_re, ssm_lam_im, ssm_log_dt,
           ssm_b_re, ssm_b_im, ssm_c_re, ssm_c_im, ssm_d, w_glu, w_ssm_o, w_out, ln1_g, ln1_b, w_router,
           router_bias, expert_w1, expert_w3, expert_w2, shared_w1, shared_w3, shared_w2, ln2_g, ln2_b):
    assert x.shape[0] == 1 and w_in.shape[0] == 1
    ssm_args = (ssm_lam_re[0], ssm_lam_im[0], ssm_log_dt[0], ssm_b_re[0], ssm_b_im[0],
                ssm_c_re[0], ssm_c_im[0], ssm_d[0])
    h1, h1b = _token_mixer(x[0], meta_tokens, ln0_g, ln0_b, w_in[0], attn_sinks[0], w_attn_o[0],
                           ssm_args, w_glu[0], w_ssm_o[0], w_out[0], ln1_g[0], ln1_b[0])
    out = _channel_mixer(h1, h1b, w_router[0], router_bias[0], expert_w1[0], expert_w3[0],
                         expert_w2[0], shared_w1[0], shared_w3[0], shared_w2[0], ln2_g[0], ln2_b[0])
    return out[None]
```

```python
import functools
import math

import jax
import jax.numpy as jnp
import numpy as np
from jax import lax
from jax.experimental import pallas as pl
from jax.experimental.pallas import tpu as pltpu

F32 = jnp.float32
BF16 = jnp.bfloat16

D_MODEL = 4096
N_META = 16
BLOCK = 128
N_PAD = BLOCK - N_META

N_HEADS = 32
N_KV_HEADS = 4
HEAD_DIM = 64
HEADS_PER_KV = N_HEADS // N_KV_HEADS
ATTN_WIDTH = N_HEADS * HEAD_DIM
KV_WIDTH = N_KV_HEADS * HEAD_DIM

SSM_WIDTH = 1024
SSM_GROUP = 16
N_SSM_GROUPS = SSM_WIDTH // SSM_GROUP
SSM_STATE = 64
SSM_CHUNK = 16
SSM_TILE_GROUPS = 128 // SSM_GROUP
SSM_TILES = N_SSM_GROUPS // SSM_TILE_GROUPS

N_EXPERTS = 64
TOP_K = 8
N_EXPERT_GROUPS = 8
EXPERTS_PER_GROUP = N_EXPERTS // N_EXPERT_GROUPS
TOPK_GROUPS = 4
EXPERT_DIM = 512
SHARED_DIM = 512
ROUTED_SCALE = 2.5
ROW_BLOCK = 256

IN_WIDTH = ATTN_WIDTH + 2 * KV_WIDTH + SSM_WIDTH + 2 * D_MODEL
COL_KV = ATTN_WIDTH
COL_U = ATTN_WIDTH + 2 * KV_WIDTH
COL_GA = COL_U + SSM_WIDTH
COL_GS = COL_GA + D_MODEL

DEEPNORM_ALPHA = 2.0 ** 0.25
LN_EPS = 1e-5
NEG_INF = -1e30

MIB = 1024 * 1024


def _params(semantics, vmem_mib):
    return pltpu.CompilerParams(dimension_semantics=semantics, vmem_limit_bytes=vmem_mib * MIB)


def _tile(n, pref):
    t = min(n, pref)
    while n % t:
        t -= 128
    return t


def _ln_rows(x, g, b):
    mu = jnp.mean(x, axis=-1, keepdims=True)
    xc = x - mu
    var = jnp.mean(xc * xc, axis=-1, keepdims=True)
    return xc * lax.rsqrt(var + LN_EPS) * g + b


LANES = 128
SLAB = D_MODEL // LANES
SLAB_PITCH = SLAB + 4


def _store_slabs(ref, y):
    m = y.shape[0]
    for s in range(SLAB):
        ref[pl.ds(s, m, stride=SLAB_PITCH), :] = y[:, s * LANES:(s + 1) * LANES]
    for s in range(SLAB, SLAB_PITCH):
        ref[pl.ds(s, m, stride=SLAB_PITCH), :] = jnp.zeros((m, LANES), F32)


def _ln_kernel(x_ref, g_ref, b_ref, o_ref, ob_ref, *maybe_slab_ref):
    y = _ln_rows(x_ref[...], g_ref[...], b_ref[...])
    o_ref[...] = y
    ob_ref[...] = y.astype(BF16)
    for os_ref in maybe_slab_ref:
        _store_slabs(os_ref, y)


def _layer_norm(x, g, b, slabs=False):
    m, d = x.shape
    tm = _tile(m, 256)
    out_specs = [pl.BlockSpec((tm, d), lambda i: (i, 0)), pl.BlockSpec((tm, d), lambda i: (i, 0))]
    out_shape = [jax.ShapeDtypeStruct((m, d), F32), jax.ShapeDtypeStruct((m, d), BF16)]
    if slabs:
        out_specs.append(pl.BlockSpec((tm * SLAB_PITCH, LANES), lambda i: (i, 0)))
        out_shape.append(jax.ShapeDtypeStruct((m * SLAB_PITCH, LANES), F32))
    return pl.pallas_call(
        _ln_kernel,
        grid=(m // tm,),
        in_specs=[pl.BlockSpec((tm, d), lambda i: (i, 0)),
                  pl.BlockSpec((1, d), lambda i: (0, 0)),
                  pl.BlockSpec((1, d), lambda i: (0, 0))],
        out_specs=out_specs,
        out_shape=out_shape,
        compiler_params=_params(("parallel",), 40),
        name="layer_norm",
    )(x, g.reshape(1, d), b.reshape(1, d))


def _proj_kernel(a_ref, w_ref, o_ref):
    o_ref[...] = jnp.dot(a_ref[...], w_ref[...].astype(BF16),
                         preferred_element_type=F32).astype(o_ref.dtype)


def _project(a, w, col0, n_cols, tm_pref=1024, tn=512):
    m, k = a.shape
    tm = _tile(m, tm_pref)
    off = col0 // tn
    return pl.pallas_call(
        _proj_kernel,
        grid=(m // tm, n_cols // tn),
        in_specs=[pl.BlockSpec((tm, k), lambda i, j: (i, 0)),
                  pl.BlockSpec((k, tn), lambda i, j: (0, j + off))],
        out_specs=pl.BlockSpec((tm, tn), lambda i, j: (i, j)),
        out_shape=jax.ShapeDtypeStruct((m, n_cols), BF16),
        compiler_params=_params(("parallel", "arbitrary"), 56),
        name="in_proj",
    )(a, w)


def _attn_kernel(sink_ref, q_ref, kvc_ref, kvp_ref, kvm_ref, o_ref):
    first = pl.program_id(0) == 0
    kv_all = jnp.concatenate([kvp_ref[...], kvc_ref[...], kvm_ref[N_PAD:BLOCK, :]], axis=0)
    n_keys = 2 * BLOCK + N_META
    rows = HEADS_PER_KV * BLOCK

    lane = lax.broadcasted_iota(jnp.int32, (BLOCK, BLOCK), 1)
    lo_q = lane < HEAD_DIM
    lane_kv = lax.broadcasted_iota(jnp.int32, (n_keys, BLOCK), 1)
    lo_kv = lane_kv < HEAD_DIM

    qi = lax.broadcasted_iota(jnp.int32, (rows, n_keys), 0) & (BLOCK - 1)
    kj = lax.broadcasted_iota(jnp.int32, (rows, n_keys), 1)
    in_window = (kj > qi) & (kj <= qi + BLOCK) & ((kj >= BLOCK) | jnp.logical_not(first))
    visible = in_window | (kj >= 2 * BLOCK)
    local_head = lax.broadcasted_iota(jnp.int32, (rows, 1), 0) // BLOCK

    def dup_head(tile, odd):
        t = tile.astype(F32)
        r = pltpu.roll(t, HEAD_DIM, 1)
        keep = jnp.logical_not(lo_kv) if odd else lo_kv
        return jnp.where(keep, t, r).astype(BF16)

    for kh in range(N_KV_HEADS):
        t0 = (kh // 2) * BLOCK
        k2 = dup_head(kv_all[:, t0:t0 + BLOCK], kh % 2)
        v2 = dup_head(kv_all[:, KV_WIDTH + t0:KV_WIDTH + t0 + BLOCK], kh % 2)
        pieces = []
        for p in range(HEADS_PER_KV // 2):
            c0 = kh * HEADS_PER_KV * HEAD_DIM + p * BLOCK
            qp = q_ref[:, c0:c0 + BLOCK]
            zero = jnp.zeros_like(qp)
            pieces.append(jnp.where(lo_q, qp, zero))
            pieces.append(jnp.where(lo_q, zero, qp))
        qs = jnp.concatenate(pieces, axis=0)
        s = lax.dot_general(qs, k2, (((1,), (1,)), ((), ())), preferred_element_type=F32)
        s = jnp.where(visible, s * (HEAD_DIM ** -0.5), NEG_INF)
        sink = jnp.zeros((rows, 1), F32)
        for t in range(HEADS_PER_KV):
            sink = jnp.where(local_head == t, sink_ref[kh * HEADS_PER_KV + t], sink)
        m = jnp.maximum(jnp.max(s, axis=-1, keepdims=True), sink)
        e = jnp.exp(s - m)
        den = jnp.sum(e, axis=-1, keepdims=True) + jnp.exp(sink - m)
        o2 = jnp.dot(e.astype(BF16), v2, preferred_element_type=F32) / den
        for p in range(HEADS_PER_KV // 2):
            c0 = kh * HEADS_PER_KV * HEAD_DIM + p * BLOCK
            r0 = 2 * p * BLOCK
            o_ref[:, c0:c0 + BLOCK] = jnp.where(
                lo_q, o2[r0:r0 + BLOCK], o2[r0 + BLOCK:r0 + 2 * BLOCK]).astype(BF16)


def _attention(z, z_lead, sinks):
    t = z.shape[0]
    kv_blk = COL_KV // (2 * KV_WIDTH)
    return pl.pallas_call(
        _attn_kernel,
        grid=(t // BLOCK,),
        in_specs=[pl.BlockSpec(memory_space=pltpu.SMEM),
                  pl.BlockSpec((BLOCK, ATTN_WIDTH), lambda n: (n, 0)),
                  pl.BlockSpec((BLOCK, 2 * KV_WIDTH), lambda n: (n, kv_blk)),
                  pl.BlockSpec((BLOCK, 2 * KV_WIDTH), lambda n: (jnp.maximum(n - 1, 0), kv_blk)),
                  pl.BlockSpec((BLOCK, 2 * KV_WIDTH), lambda n: (0, 0))],
        out_specs=pl.BlockSpec((BLOCK, ATTN_WIDTH), lambda n: (n, 0)),
        out_shape=jax.ShapeDtypeStruct((t, ATTN_WIDTH), BF16),
        compiler_params=_params(("parallel",), 40),
        name="swa_attention",
    )(sinks.astype(F32), z, z, z, z_lead)


def _cmul(ar, ai, br, bi):
    return ar * br - ai * bi, ar * bi + ai * br


def _ssm_tables(lam_re, lam_im, log_dt, b_re, b_im, c_re, c_im, d_skip, n_chunks):
    hi = lax.Precision.HIGHEST
    g, p, h, c = N_SSM_GROUPS, SSM_STATE, SSM_GROUP, SSM_CHUNK
    dt = jnp.exp(log_dt)[:, None]
    mag = jnp.exp(dt * lam_re)
    are = mag * jnp.cos(dt * lam_im)
    aim = mag * jnp.sin(dt * lam_im)
    den = lam_re * lam_re + lam_im * lam_im
    num_re = are - 1.0
    coef_re = (num_re * lam_re + aim * lam_im) / den
    coef_im = (aim * lam_re - num_re * lam_im) / den
    bb_re = coef_re[..., None] * b_re - coef_im[..., None] * b_im
    bb_im = coef_re[..., None] * b_im + coef_im[..., None] * b_re

    pr, pi = [jnp.ones_like(are)], [jnp.zeros_like(are)]
    for _ in range(c):
        r, i = _cmul(pr[-1], pi[-1], are, aim)
        pr.append(r)
        pi.append(i)
    pr, pi = jnp.stack(pr), jnp.stack(pi)

    m_re = pr[:c, :, :, None] * bb_re[None] - pi[:c, :, :, None] * bb_im[None]
    m_im = pr[:c, :, :, None] * bb_im[None] + pi[:c, :, :, None] * bb_re[None]
    lag = (jnp.einsum('ghp,kgpj->kghj', c_re, m_re, precision=hi)
           - jnp.einsum('ghp,kgpj->kghj', c_im, m_im, precision=hi))
    nt, gt = SSM_TILES, SSM_TILE_GROUPS
    lagc = lag.reshape(c, nt, gt, h, h).transpose(1, 0, 2, 4, 3)
    lagc = lagc.reshape(nt, c, gt * h, h)
    rev = np.arange(c - 1, -1, -1)
    wo_re = pr[rev][:, :, :, None] * bb_re[None] - pi[rev][:, :, :, None] * bb_im[None]
    wo_im = pr[rev][:, :, :, None] * bb_im[None] + pi[rev][:, :, :, None] * bb_re[None]

    def lanes(first, second):
        w = jnp.stack([first, second]).reshape(2, c, nt, gt, p, h).transpose(2, 1, 3, 5, 0, 4)
        return w.reshape(nt, c, gt * h, 2 * p)

    wout_a, wout_b = lanes(wo_re, wo_im), lanes(wo_im, wo_re)
    cin_re = c_re[None] * pr[1:, :, None, :] - c_im[None] * pi[1:, :, None, :]
    cin_im = -(c_re[None] * pi[1:, :, None, :] + c_im[None] * pr[1:, :, None, :])
    cinc = jnp.stack([cin_re, cin_im]).reshape(2, c, nt, gt, h, p)
    cinc = cinc.transpose(2, 1, 0, 3, 5, 4).reshape(nt, c, 2 * gt * p, h)
    spread = jnp.tile(jnp.eye(h, dtype=BF16), (1, gt))

    n_steps = max(1, math.ceil(math.log2(n_chunks)))
    qr, qi = [pr[c]], [pi[c]]
    for _ in range(n_steps - 1):
        r, i = _cmul(qr[-1], qi[-1], qr[-1], qi[-1])
        qr.append(r)
        qi.append(i)
    apr = jnp.stack(qr, axis=0).reshape(n_steps, nt, gt * p).transpose(1, 0, 2)
    api = jnp.stack(qi, axis=0).reshape(n_steps, nt, gt * p).transpose(1, 0, 2)
    dsk = d_skip.reshape(nt, 1, gt * h)
    return (lagc.astype(BF16), wout_a.astype(BF16), wout_b.astype(BF16), cinc.astype(BF16), spread,
            apr, api, dsk, n_steps)


def _chunk_steps(u_ref):
    n = u_ref.shape[0] // SSM_CHUNK
    return [u_ref[pl.ds(s, n, stride=SSM_CHUNK), :] for s in range(SSM_CHUNK)]


def _chunk_lanes(u_ref):
    return jnp.concatenate(_chunk_steps(u_ref), axis=1)


def _ssm_state_kernel(u_ref, lead_ref, wa_ref, wb_ref, apr_ref, api_ref, prev_ref, wout_s, *, n_steps):
    p, gh = SSM_STATE, SSM_TILE_GROUPS * SSM_GROUP
    row_g = lax.broadcasted_iota(jnp.int32, (gh, 2 * p), 0) // SSM_GROUP
    lo = lax.broadcasted_iota(jnp.int32, (gh, 2 * p), 1) < p
    zero = jnp.zeros((gh, 2 * p), BF16)
    for s in range(SSM_CHUNK):
        a, b = wa_ref[0, s], wb_ref[0, s]
        for j in range(SSM_TILE_GROUPS // 2):
            even, odd = (row_g == 2 * j) & lo, (row_g == 2 * j + 1) & jnp.logical_not(lo)
            wout_s[s * gh:(s + 1) * gh, j * 2 * p:(j + 1) * 2 * p] = (
                jnp.where(even, a, jnp.where(odd, b, zero)))
            wout_s[s * gh:(s + 1) * gh, (SSM_TILE_GROUPS // 2 + j) * 2 * p:(SSM_TILE_GROUPS // 2 + j + 1) * 2 * p] = (
                jnp.where(even, b, jnp.where(odd, a, zero)))
    wout = wout_s[...]
    st = jnp.dot(_chunk_lanes(u_ref).astype(BF16), wout, preferred_element_type=F32)
    lead = jnp.dot(_chunk_lanes(lead_ref).astype(BF16), wout, preferred_element_type=F32)
    nc, half = st.shape[0], st.shape[1] // 2
    xr, xi = st[:, :half], st[:, half:]
    mr, mi = lead[-1:, :half], lead[-1:, half:]
    row = lax.broadcasted_iota(jnp.int32, (nc, half), 0)
    ar, ai = apr_ref[0, 0:1, :], api_ref[0, 0:1, :]
    xr = xr + jnp.where(row == 0, ar * mr - ai * mi, 0.0)
    xi = xi + jnp.where(row == 0, ar * mi + ai * mr, 0.0)

    def shift(x, d):
        return jnp.where(row >= d, pltpu.roll(x, d, 0), 0.0)

    for j in range(n_steps):
        ar, ai = apr_ref[0, j:j + 1, :], api_ref[0, j:j + 1, :]
        sr, si = shift(xr, 1 << j), shift(xi, 1 << j)
        xr, xi = xr + (ar * sr - ai * si), xi + (ar * si + ai * sr)
    pr = jnp.where(row == 0, mr, shift(xr, 1))
    pi = jnp.where(row == 0, mi, shift(xi, 1))
    prev_ref[0] = jnp.concatenate([pr, pi], axis=1).astype(BF16)


def _ssm_out_kernel(u_ref, prev_ref, lag_ref, cin_ref, spread_ref, d_ref, y_ref, toep_s, wcin_s):
    gh, c = SSM_TILE_GROUPS * SSM_GROUP, SSM_CHUNK

    @pl.when(pl.program_id(1) == 0)
    def _():
        spread = spread_ref[...]
        diag = (lax.broadcasted_iota(jnp.int32, (gh, gh), 0) // SSM_GROUP
                == lax.broadcasted_iota(jnp.int32, (gh, gh), 1) // SSM_GROUP)
        zero = jnp.zeros((gh, gh), BF16)
        blocks = [jnp.where(diag, jnp.dot(lag_ref[0, k], spread, preferred_element_type=F32), 0.0).astype(BF16)
                  for k in range(c)]
        for s in range(c):
            for i in range(c):
                toep_s[s * gh:(s + 1) * gh, i * gh:(i + 1) * gh] = blocks[i - s] if i >= s else zero
        rows = wcin_s.shape[0]
        row_g = (lax.broadcasted_iota(jnp.int32, (rows, gh), 0) % (rows // 2)) // SSM_STATE
        own = row_g == lax.broadcasted_iota(jnp.int32, (rows, gh), 1) // SSM_GROUP
        for i in range(c):
            wcin_s[:, i * gh:(i + 1) * gh] = jnp.where(
                own, jnp.dot(cin_ref[0, i], spread, preferred_element_type=F32), 0.0).astype(BF16)

    us = _chunk_steps(u_ref)
    cb = us[0].shape[0]
    u = jnp.concatenate(us, axis=1).astype(BF16)
    y = (jnp.dot(u, toep_s[...], preferred_element_type=F32)
         + jnp.dot(prev_ref[0], wcin_s[...], preferred_element_type=F32))
    for s in range(SSM_CHUNK):
        ys = y[:, s * 128:(s + 1) * 128] + d_ref[0] * us[s]
        y_ref[pl.ds(s, cb, stride=SSM_CHUNK), :] = 0.5 * ys * (1.0 + lax.erf(ys * (0.5 ** 0.5)))


def _ssm(u, u_lead, tables):
    lagc, wout_a, wout_b, cinc, spread, apr, api, dsk, n_steps = tables
    t = u.shape[0]
    nc = t // SSM_CHUNK
    cb = _tile(nc, 256) if nc % 128 == 0 else nc
    lanes = 128
    k = SSM_CHUNK * lanes
    states = 2 * SSM_TILE_GROUPS * SSM_STATE
    prev = pl.pallas_call(
        functools.partial(_ssm_state_kernel, n_steps=n_steps),
        grid=(SSM_TILES,),
        in_specs=[pl.BlockSpec((t, lanes), lambda g: (0, g)),
                  pl.BlockSpec((BLOCK, lanes), lambda g: (0, g)),
                  pl.BlockSpec((1, SSM_CHUNK, lanes, 2 * SSM_STATE), lambda g: (g, 0, 0, 0)),
                  pl.BlockSpec((1, SSM_CHUNK, lanes, 2 * SSM_STATE), lambda g: (g, 0, 0, 0)),
                  pl.BlockSpec((1, n_steps, states // 2), lambda g: (g, 0, 0)),
                  pl.BlockSpec((1, n_steps, states // 2), lambda g: (g, 0, 0))],
        out_specs=pl.BlockSpec((1, nc, states), lambda g: (g, 0, 0)),
        out_shape=jax.ShapeDtypeStruct((SSM_TILES, nc, states), BF16),
        scratch_shapes=[pltpu.VMEM((k, states), BF16)],
        compiler_params=_params(("parallel",), 48),
        name="s5_states",
    )(u, u_lead, wout_a, wout_b, apr, api)
    return pl.pallas_call(
        _ssm_out_kernel,
        grid=(SSM_TILES, nc // cb),
        in_specs=[pl.BlockSpec((cb * SSM_CHUNK, lanes), lambda g, c: (c, g)),
                  pl.BlockSpec((1, cb, states), lambda g, c: (g, c, 0)),
                  pl.BlockSpec((1, SSM_CHUNK, lanes, SSM_GROUP), lambda g, c: (g, 0, 0, 0)),
                  pl.BlockSpec((1, SSM_CHUNK, states, SSM_GROUP), lambda g, c: (g, 0, 0, 0)),
                  pl.BlockSpec((SSM_GROUP, lanes), lambda g, c: (0, 0)),
                  pl.BlockSpec((1, 1, lanes), lambda g, c: (g, 0, 0))],
        out_specs=pl.BlockSpec((cb * SSM_CHUNK, lanes), lambda g, c: (c, g)),
        out_shape=jax.ShapeDtypeStruct((t, SSM_WIDTH), F32),
        scratch_shapes=[pltpu.VMEM((k, k), BF16), pltpu.VMEM((states, k), BF16)],
        compiler_params=_params(("parallel", "arbitrary"), 48),
        name="s5_outputs",
    )(u, prev, lagc, cinc, spread, dsk)


def _glu_kernel(y_ref, w_ref, o_ref):
    y = y_ref[...]
    gate = jax.nn.sigmoid(jnp.dot(y.astype(BF16), w_ref[...].astype(BF16), preferred_element_type=F32))
    o_ref[...] = (y * gate).astype(BF16)


def _glu(y, w):
    m, d = y.shape
    tm = _tile(m, 1024)
    return pl.pallas_call(
        _glu_kernel,
        grid=(m // tm,),
        in_specs=[pl.BlockSpec((tm, d), lambda i: (i, 0)),
                  pl.BlockSpec((d, d), lambda i: (0, 0))],
        out_specs=pl.BlockSpec((tm, d), lambda i: (i, 0)),
        out_shape=jax.ShapeDtypeStruct((m, d), BF16),
        compiler_params=_params(("parallel",), 40),
        name="ssm_glu",
    )(y, w)


def _mix_kernel(a_ref, y_ref, wa_ref, ws_ref, ga_ref, gs_ref, o_ref):
    attn_d = jnp.dot(a_ref[...], wa_ref[...].astype(BF16), preferred_element_type=F32)
    ssm_d = jnp.dot(y_ref[...], ws_ref[...].astype(BF16), preferred_element_type=F32)
    mix = (jax.nn.sigmoid(ga_ref[...].astype(F32)) * attn_d
           + jax.nn.sigmoid(gs_ref[...].astype(F32)) * ssm_d)
    o_ref[...] = mix.astype(BF16)


def _mix(attn, y2, w_attn_o, w_ssm_o, z):
    m = attn.shape[0]
    tm, tn = _tile(m, 1024), 512
    ga_blk, gs_blk = COL_GA // tn, COL_GS // tn
    return pl.pallas_call(
        _mix_kernel,
        grid=(m // tm, D_MODEL // tn),
        in_specs=[pl.BlockSpec((tm, ATTN_WIDTH), lambda i, j: (i, 0)),
                  pl.BlockSpec((tm, SSM_WIDTH), lambda i, j: (i, 0)),
                  pl.BlockSpec((ATTN_WIDTH, tn), lambda i, j: (0, j)),
                  pl.BlockSpec((SSM_WIDTH, tn), lambda i, j: (0, j)),
                  pl.BlockSpec((tm, tn), lambda i, j: (i, j + ga_blk)),
                  pl.BlockSpec((tm, tn), lambda i, j: (i, j + gs_blk))],
        out_specs=pl.BlockSpec((tm, tn), lambda i, j: (i, j)),
        out_shape=jax.ShapeDtypeStruct((m, D_MODEL), BF16),
        compiler_params=_params(("parallel", "arbitrary"), 48),
        name="gated_merge",
    )(attn, y2, w_attn_o, w_ssm_o, z, z)


def _resid_kernel(a_ref, w_ref, h_ref, o_ref):
    o_ref[...] = DEEPNORM_ALPHA * h_ref[...] + jnp.dot(
        a_ref[...], w_ref[...].astype(BF16), preferred_element_type=F32)


def _out_proj(mix, w_out, h0):
    m = mix.shape[0]
    tm, tn = _tile(m, 1024), 512
    return pl.pallas_call(
        _resid_kernel,
        grid=(m // tm, D_MODEL // tn),
        in_specs=[pl.BlockSpec((tm, D_MODEL), lambda i, j: (i, 0)),
                  pl.BlockSpec((D_MODEL, tn), lambda i, j: (0, j)),
                  pl.BlockSpec((tm, tn), lambda i, j: (i, j))],
        out_specs=pl.BlockSpec((tm, tn), lambda i, j: (i, j)),
        out_shape=jax.ShapeDtypeStruct((m, D_MODEL), F32),
        compiler_params=_params(("parallel", "arbitrary"), 56),
        name="out_proj",
    )(mix, w_out, h0)


def _token_mixer(x, meta_tokens, ln0_g, ln0_b, w_in, attn_sinks, w_attn_o, ssm_tables_args,
                 w_glu, w_ssm_o, w_out, ln1_g, ln1_b):
    t = x.shape[0]
    lead = jnp.concatenate([jnp.zeros((N_PAD, D_MODEL), F32), meta_tokens.astype(F32)], axis=0)
    h0, h0b = _layer_norm(x, ln0_g, ln0_b)
    _, leadb = _layer_norm(lead, ln0_g, ln0_b)

    z = _project(h0b, w_in, 0, IN_WIDTH)
    z_lead = _project(leadb, w_in, COL_KV, 2 * KV_WIDTH + SSM_WIDTH)

    attn = _attention(z, z_lead, attn_sinks)

    lead_valid = (jnp.arange(BLOCK) >= N_PAD)[:, None]
    u_lead = jnp.where(lead_valid, z_lead[:, 2 * KV_WIDTH:].astype(F32), 0.0)
    tables = _ssm_tables(*ssm_tables_args, n_chunks=t // SSM_CHUNK)
    y = _ssm(z[:, COL_U:COL_GA].astype(F32), u_lead, tables)
    y2 = _glu(y, w_glu)

    mix = _mix(attn, y2, w_attn_o, w_ssm_o, z)
    r = _out_proj(mix, w_out, h0)
    return _layer_norm(r, ln1_g, ln1_b, slabs=True)


def _router_kernel(h_ref, wr_ref, bias_ref, idx_ref, wt_ref):
    tm = h_ref.shape[0]
    ng, ne = N_EXPERT_GROUPS, EXPERTS_PER_GROUP
    logits = lax.dot_general(wr_ref[...], h_ref[...], (((1,), (1,)), ((), ())),
                             precision=lax.Precision.HIGHEST, preferred_element_type=F32)
    scores = jax.nn.sigmoid(logits)
    sel = (scores + bias_ref[...]).reshape(ng, ne, tm)
    scores = scores.reshape(ng, ne, tm)
    e_in = lax.broadcasted_iota(jnp.int32, (ng, ne, tm), 1)
    e_id = lax.broadcasted_iota(jnp.int32, (ng, ne, tm), 0) * ne + e_in
    neg = -jnp.inf

    top1 = jnp.max(sel, axis=1, keepdims=True)
    first = jnp.min(jnp.where(sel == top1, e_in, ne), axis=1, keepdims=True)
    top2 = jnp.max(jnp.where(e_in == first, neg, sel), axis=1, keepdims=True)
    grp = (top1 + top2)[:, 0, :]

    g_id = lax.broadcasted_iota(jnp.int32, (ng, tm), 0)
    chosen = jnp.zeros((ng, tm), jnp.bool_)
    for _ in range(TOPK_GROUPS):
        best = jnp.max(grp, axis=0, keepdims=True)
        pick = g_id == jnp.min(jnp.where(grp == best, g_id, ng), axis=0, keepdims=True)
        chosen = chosen | pick
        grp = jnp.where(pick, neg, grp)

    cur = jnp.where(chosen[:, None, :], sel, NEG_INF)
    ids, wts = [], []
    for _ in range(TOP_K):
        best = jnp.max(jnp.max(cur, axis=1, keepdims=True), axis=0, keepdims=True)
        cand = jnp.where(cur == best, e_id, N_EXPERTS)
        win = jnp.min(jnp.min(cand, axis=1, keepdims=True), axis=0, keepdims=True)
        pick = e_id == win
        wsel = jnp.where(pick, scores, 0.0)
        ids.append(win[0])
        wts.append(jnp.sum(jnp.sum(wsel, axis=1, keepdims=True), axis=0, keepdims=True)[0])
        cur = jnp.where(pick, neg, cur)
    w = jnp.concatenate(wts, axis=0)
    idx_ref[...] = jnp.concatenate(ids, axis=0)
    wt_ref[...] = w / jnp.sum(w, axis=0, keepdims=True) * ROUTED_SCALE


def _router(h1, w_router, router_bias):
    t, d = h1.shape
    tm = _tile(t, 512)
    return pl.pallas_call(
        _router_kernel,
        grid=(t // tm,),
        in_specs=[pl.BlockSpec((tm, d), lambda i: (i, 0)),
                  pl.BlockSpec((N_EXPERTS, d), lambda i: (0, 0)),
                  pl.BlockSpec((N_EXPERTS, 1), lambda i: (0, 0))],
        out_specs=[pl.BlockSpec((TOP_K, tm), lambda i: (0, i)),
                   pl.BlockSpec((TOP_K, tm), lambda i: (0, i))],
        out_shape=[jax.ShapeDtypeStruct((TOP_K, t), jnp.int32),
                   jax.ShapeDtypeStruct((TOP_K, t), F32)],
        compiler_params=_params(("parallel",), 40),
        name="router",
    )(h1, w_router.T, router_bias.reshape(N_EXPERTS, 1))


def _dispatch(idx_t):
    t = idx_t.shape[1]
    n_assign = TOP_K * t
    n_blocks = -(-n_assign // ROW_BLOCK) + N_EXPERTS
    n_rows = n_blocks * ROW_BLOCK
    e_flat = idx_t.reshape(-1)
    onehot = (e_flat[:, None] == jnp.arange(N_EXPERTS, dtype=jnp.int32)[None, :]).astype(jnp.int32)
    csum = jnp.cumsum(onehot, axis=0)
    rank = jnp.take_along_axis(csum, e_flat[:, None], axis=1)[:, 0] - 1
    counts = csum[-1]
    padded = (counts + ROW_BLOCK - 1) // ROW_BLOCK * ROW_BLOCK
    ends = jnp.cumsum(padded)
    starts = ends - padded
    dest = starts[e_flat] + rank
    tok_flat = jnp.tile(jnp.arange(t, dtype=jnp.int32), TOP_K)
    row_tok = jnp.zeros((n_rows,), jnp.int32).at[dest].set(tok_flat)
    block_start = jnp.arange(n_blocks, dtype=jnp.int32) * ROW_BLOCK
    block_expert = jnp.minimum(
        jnp.sum((ends[None, :] <= block_start[:, None]).astype(jnp.int32), axis=1), N_EXPERTS - 1)
    n_used = (ends[-1] // ROW_BLOCK).astype(jnp.int32).reshape(1)
    has = counts > 0
    cand = jnp.where(has[None, :] & (jnp.arange(N_EXPERTS)[None, :] > jnp.arange(N_EXPERTS)[:, None]),
                     jnp.arange(N_EXPERTS)[None, :], N_EXPERTS)
    nxt = jnp.min(cand, axis=1)
    next_expert = jnp.where(nxt < N_EXPERTS, nxt, -1).astype(jnp.int32)
    return (row_tok.reshape(n_blocks, 1, ROW_BLOCK), block_expert, n_used, next_expert,
            dest.reshape(TOP_K, t))


WEIGHT_DMA_PRIORITY = 1


def _gather_rows(src_hbm, idx_smem, dst, sem, n, unrolled=False):
    def body(i, carry):
        pltpu.make_async_copy(src_hbm.at[pl.ds(idx_smem[0, i], 1)], dst.at[pl.ds(i, 1)], sem).start()
        return carry
    if unrolled:
        for i in range(n):
            body(i, 0)
    else:
        lax.fori_loop(0, n, body, 0, unroll=8)


def _wait_rows(src_hbm, dst, sem, n):
    pltpu.make_async_copy(src_hbm.at[pl.ds(0, n)], dst, sem).wait()


def _gather_slabs(src_hbm, idx_smem, dst, sem, n, unrolled=False, dst_pitch=None):
    dst_pitch = SLAB_PITCH if dst_pitch is None else dst_pitch

    def body(i, carry):
        src = idx_smem[0, i] * SLAB_PITCH
        pltpu.make_async_copy(src_hbm.at[pl.ds(src, SLAB)], dst.at[pl.ds(i * dst_pitch, SLAB)], sem).start()
        return carry
    if unrolled:
        for i in range(n):
            body(i, 0)
    else:
        lax.fori_loop(0, n, body, 0, unroll=8)


def _wait_slabs(src_hbm, dst, sem, n):
    pltpu.make_async_copy(src_hbm.at[pl.ds(0, n * SLAB)], dst.at[pl.ds(0, n * SLAB)], sem).wait()


def _slab_rows(ref, n, pitch=None):
    pitch = SLAB_PITCH if pitch is None else pitch
    return jnp.concatenate([ref[pl.ds(s, n, stride=pitch), :] for s in range(SLAB)], axis=1)


def _expert_up_kernel(be_ref, nu_ref, nx_ref, tok_hbm, h_hbm, w1_hbm, w3_hbm, o_ref,
                      idx_smem, xbuf, wstage, wcast, isem, xsem, wsem):
    b = pl.program_id(0)
    n_used = nu_ref[0]
    last = n_used - 1
    e = be_ref[b]
    slot = b % 2

    def idx_copy(blk, s):
        return pltpu.make_async_copy(tok_hbm.at[blk], idx_smem.at[s], isem.at[s])

    def weight_copies(ex):
        return (pltpu.make_async_copy(w1_hbm.at[ex], wstage.at[0], wsem.at[0]),
                pltpu.make_async_copy(w3_hbm.at[ex], wstage.at[1], wsem.at[1]))

    @pl.when(b == 0)
    def _():
        for c in weight_copies(e):
            c.start(priority=WEIGHT_DMA_PRIORITY)
        idx_copy(0, 0).start()
        idx_copy(0, 0).wait()
        _gather_slabs(h_hbm, idx_smem.at[0], xbuf.at[0], xsem.at[0], ROW_BLOCK)
        idx_copy(jnp.minimum(1, last), 1).start()

    @pl.when(b < n_used)
    def _():
        first_of_expert = jnp.logical_or(b == 0, be_ref[jnp.maximum(b - 1, 0)] != e)

        @pl.when(first_of_expert)
        def _():
            for c in weight_copies(e):
                c.wait()
            rows = wstage.shape[1] // 8
            for m in range(2):
                for r in range(8):
                    wcast[m, r * rows:(r + 1) * rows] = wstage[m, r * rows:(r + 1) * rows].astype(BF16)
            nxt = nx_ref[e]

            @pl.when(nxt >= 0)
            def _():
                for c in weight_copies(nxt):
                    c.start(priority=WEIGHT_DMA_PRIORITY)

        idx_copy(jnp.minimum(b + 1, last), 1 - slot).wait()
        _wait_slabs(h_hbm, xbuf.at[slot], xsem.at[slot], ROW_BLOCK)
        _gather_slabs(h_hbm, idx_smem.at[1 - slot], xbuf.at[1 - slot], xsem.at[1 - slot], ROW_BLOCK,
                      unrolled=True)
        idx_copy(jnp.minimum(b + 2, last), slot).start()
        x = _slab_rows(xbuf.at[slot], ROW_BLOCK).astype(BF16)
        a = jnp.dot(x, wcast[0], preferred_element_type=F32)
        g = jnp.dot(x, wcast[1], preferred_element_type=F32)
        o_ref[...] = (a * jax.nn.sigmoid(a) * g).astype(BF16)

        @pl.when(b == last)
        def _():
            _wait_slabs(h_hbm, xbuf.at[1 - slot], xsem.at[1 - slot], ROW_BLOCK)
            idx_copy(last, slot).wait()

    @pl.when(b >= n_used)
    def _():
        o_ref[...] = jnp.zeros_like(o_ref)


def _expert_up(h1, row_tok, block_expert, n_used, next_expert, w1, w3):
    n_blocks = row_tok.shape[0]
    d = w1.shape[1]
    grid_spec = pltpu.PrefetchScalarGridSpec(
        num_scalar_prefetch=3,
        grid=(n_blocks,),
        in_specs=[pl.BlockSpec(memory_space=pl.ANY)] * 4,
        out_specs=pl.BlockSpec((ROW_BLOCK, EXPERT_DIM), lambda b, *_: (b, 0)),
        scratch_shapes=[pltpu.SMEM((2, 1, ROW_BLOCK), jnp.int32),
                        pltpu.VMEM((2, ROW_BLOCK * SLAB_PITCH, LANES), F32),
                        pltpu.VMEM((2, d, EXPERT_DIM), F32),
                        pltpu.VMEM((2, d, EXPERT_DIM), BF16),
                        pltpu.SemaphoreType.DMA((2,)),
                        pltpu.SemaphoreType.DMA((2,)),
                        pltpu.SemaphoreType.DMA((2,))])
    return pl.pallas_call(
        _expert_up_kernel,
        grid_spec=grid_spec,
        out_shape=jax.ShapeDtypeStruct((n_blocks * ROW_BLOCK, EXPERT_DIM), BF16),
        compiler_params=_params(("arbitrary",), 58),
        name="expert_up",
    )(block_expert, n_used, next_expert, row_tok, h1, w1, w3)


def _expert_down_kernel(be_ref, nu_ref, nx_ref, a_ref, w2_hbm, o_ref, wstage, wcast, wsem):
    b = pl.program_id(0)
    n_used = nu_ref[0]
    e = be_ref[b]

    def weight_copy(ex):
        return pltpu.make_async_copy(w2_hbm.at[ex], wstage, wsem)

    @pl.when(b == 0)
    def _():
        weight_copy(e).start(priority=WEIGHT_DMA_PRIORITY)

    @pl.when(b < n_used)
    def _():
        first_of_expert = jnp.logical_or(b == 0, be_ref[jnp.maximum(b - 1, 0)] != e)

        @pl.when(first_of_expert)
        def _():
            weight_copy(e).wait()
            wcast[...] = wstage[...].astype(BF16)
            nxt = nx_ref[e]

            @pl.when(nxt >= 0)
            def _():
                weight_copy(nxt).start(priority=WEIGHT_DMA_PRIORITY)

        _store_slabs(o_ref, jnp.dot(a_ref[...], wcast[...], preferred_element_type=F32))

    @pl.when(b >= n_used)
    def _():
        o_ref[...] = jnp.zeros_like(o_ref)


def _expert_down(act, block_expert, n_used, next_expert, w2):
    n_rows = act.shape[0]
    n_blocks = n_rows // ROW_BLOCK
    d = w2.shape[2]
    grid_spec = pltpu.PrefetchScalarGridSpec(
        num_scalar_prefetch=3,
        grid=(n_blocks,),
        in_specs=[pl.BlockSpec((ROW_BLOCK, EXPERT_DIM), lambda b, *_: (b, 0)),
                  pl.BlockSpec(memory_space=pl.ANY)],
        out_specs=pl.BlockSpec((ROW_BLOCK * SLAB_PITCH, LANES), lambda b, *_: (b, 0)),
        scratch_shapes=[pltpu.VMEM((EXPERT_DIM, d), F32),
                        pltpu.VMEM((EXPERT_DIM, d), BF16),
                        pltpu.SemaphoreType.DMA(())])
    return pl.pallas_call(
        _expert_down_kernel,
        grid_spec=grid_spec,
        out_shape=jax.ShapeDtypeStruct((n_rows * SLAB_PITCH, LANES), F32),
        compiler_params=_params(("arbitrary",), 40),
        name="expert_down",
    )(block_expert, n_used, next_expert, act, w2)


def _shared_kernel(h_ref, w1_ref, w3_ref, w2_ref, o_ref):
    x = h_ref[...]
    a = jnp.dot(x, w1_ref[...], preferred_element_type=F32)
    g = jnp.dot(x, w3_ref[...], preferred_element_type=F32)
    act = (a * jax.nn.sigmoid(a) * g).astype(BF16)
    o_ref[...] = jnp.dot(act, w2_ref[...], preferred_element_type=F32)


def _shared_expert(h1b, w1, w3, w2):
    t, d = h1b.shape
    tm = _tile(t, 512)
    return pl.pallas_call(
        _shared_kernel,
        grid=(t // tm,),
        in_specs=[pl.BlockSpec((tm, d), lambda i: (i, 0)),
                  pl.BlockSpec((d, SHARED_DIM), lambda i: (0, 0)),
                  pl.BlockSpec((d, SHARED_DIM), lambda i: (0, 0)),
                  pl.BlockSpec((SHARED_DIM, d), lambda i: (0, 0))],
        out_specs=pl.BlockSpec((tm, d), lambda i: (i, 0)),
        out_shape=jax.ShapeDtypeStruct((t, d), F32),
        compiler_params=_params(("parallel",), 56),
        name="shared_expert",
    )(h1b, w1.astype(BF16), w3.astype(BF16), w2.astype(BF16))


COMBINE_TOKENS = 64
COMBINE_PITCH = SLAB + 8


def _combine_kernel(dest_hbm, y_hbm, h_ref, s_ref, w_ref, g_ref, b_ref, o_ref,
                    idx_smem, ybuf, stage, isem, ysem):
    i = pl.program_id(0)
    n = pl.num_programs(0)
    slot = i % 2
    rows = TOP_K * COMBINE_TOKENS

    def idx_copy(blk, s):
        return pltpu.make_async_copy(dest_hbm.at[blk], idx_smem.at[s], isem.at[s])

    last = n - 1

    pitch = COMBINE_PITCH
    per_j = COMBINE_TOKENS * pitch

    @pl.when(i == 0)
    def _():
        ybuf[...] = jnp.zeros_like(ybuf)
        idx_copy(0, 0).start()
        idx_copy(0, 0).wait()
        _gather_slabs(y_hbm, idx_smem.at[0], ybuf.at[0], ysem.at[0], rows, dst_pitch=pitch)
        idx_copy(jnp.minimum(1, last), 1).start()

    idx_copy(jnp.minimum(i + 1, last), 1 - slot).wait()
    _wait_slabs(y_hbm, ybuf.at[slot], ysem.at[slot], rows)
    _gather_slabs(y_hbm, idx_smem.at[1 - slot], ybuf.at[1 - slot], ysem.at[1 - slot], rows,
                  unrolled=True, dst_pitch=pitch)
    idx_copy(jnp.minimum(i + 2, last), slot).start()
    acc = None
    for j in range(TOP_K):
        wj = jnp.broadcast_to(w_ref[:, j:j + 1, :], (COMBINE_TOKENS, pitch, LANES)).reshape(per_j, LANES)
        term = wj * ybuf[slot, j * per_j:(j + 1) * per_j, :]
        acc = term if acc is None else acc + term
    stage[...] = acc
    routed = _slab_rows(stage, COMBINE_TOKENS, pitch)
    o_ref[...] = _ln_rows(DEEPNORM_ALPHA * h_ref[...] + s_ref[...] + routed, g_ref[...], b_ref[...])

    @pl.when(i == last)
    def _():
        _wait_slabs(y_hbm, ybuf.at[1 - slot], ysem.at[1 - slot], rows)
        idx_copy(last, slot).wait()


def _combine(yb, dest, w_t, h1, shared, g, b):
    t, d = h1.shape
    tm = COMBINE_TOKENS
    n_tiles = t // tm
    dest_tiles = dest.reshape(TOP_K, n_tiles, tm).transpose(1, 0, 2).reshape(n_tiles, 1, TOP_K * tm)
    w_lanes = jnp.broadcast_to(w_t.T[:, :, None], (t, TOP_K, LANES))
    return pl.pallas_call(
        _combine_kernel,
        grid=(n_tiles,),
        in_specs=[pl.BlockSpec(memory_space=pl.ANY),
                  pl.BlockSpec(memory_space=pl.ANY),
                  pl.BlockSpec((tm, d), lambda i: (i, 0)),
                  pl.BlockSpec((tm, d), lambda i: (i, 0)),
                  pl.BlockSpec((tm, TOP_K, LANES), lambda i: (i, 0, 0)),
                  pl.BlockSpec((1, d), lambda i: (0, 0)),
                  pl.BlockSpec((1, d), lambda i: (0, 0))],
        out_specs=pl.BlockSpec((tm, d), lambda i: (i, 0)),
        out_shape=jax.ShapeDtypeStruct((t, d), F32),
        scratch_shapes=[pltpu.SMEM((2, 1, TOP_K * tm), jnp.int32),
                        pltpu.VMEM((2, TOP_K * tm * COMBINE_PITCH, LANES), F32),
                        pltpu.VMEM((tm * COMBINE_PITCH, LANES), F32),
                        pltpu.SemaphoreType.DMA((2,)),
                        pltpu.SemaphoreType.DMA((2,))],
        compiler_params=_params(("arbitrary",), 48),
        name="combine_norm",
    )(dest_tiles, yb, h1, shared, w_lanes, g.reshape(1, d), b.reshape(1, d))


def _channel_mixer(h1, h1b, h1_slabs, w_router, router_bias, expert_w1, expert_w3, expert_w2,
                   shared_w1, shared_w3, shared_w2, ln2_g, ln2_b):
    idx_t, w_t = _router(h1, w_router, router_bias)
    row_tok, block_expert, n_used, next_expert, dest = _dispatch(idx_t)
    shared = _shared_expert(h1b, shared_w1, shared_w3, shared_w2)
    act = _expert_up(h1_slabs, row_tok, block_expert, n_used, next_expert, expert_w1, expert_w3)
    yb = _expert_down(act, block_expert, n_used, next_expert, expert_w2)
    return _combine(yb, dest, w_t, h1, shared, ln2_g, ln2_b)


def kernel(x, meta_tokens, ln0_g, ln0_b, w_in, attn_sinks, w_attn_o, ssm_lam_re, ssm_lam_im, ssm_log_dt,
           ssm_b_re, ssm_b_im, ssm_c_re, ssm_c_im, ssm_d, w_glu, w_ssm_o, w_out, ln1_g, ln1_b, w_router,
           router_bias, expert_w1, expert_w3, expert_w2, shared_w1, shared_w3, shared_w2, ln2_g, ln2_b):
    assert x.shape[0] == 1 and w_in.shape[0] == 1
    ssm_args = (ssm_lam_re[0], ssm_lam_im[0], ssm_log_dt[0], ssm_b_re[0], ssm_b_im[0],
                ssm_c_re[0], ssm_c_im[0], ssm_d[0])
    h1, h1b, h1_slabs = _token_mixer(x[0], meta_tokens, ln0_g, ln0_b, w_in[0], attn_sinks[0],
                                     w_attn_o[0], ssm_args, w_glu[0], w_ssm_o[0], w_out[0], ln1_g[0], ln1_b[0])
    out = _channel_mixer(h1, h1b, h1_slabs, w_router[0], router_bias[0], expert_w1[0], expert_w3[0],
                         expert_w2[0], shared_w1[0], shared_w3[0], shared_w2[0], ln2_g[0], ln2_b[0])
    return out[None]
```

```python
import functools
import math

import jax
import jax.numpy as jnp
import numpy as np
from jax import lax
from jax.experimental import pallas as pl
from jax.experimental.pallas import tpu as pltpu

F32 = jnp.float32
BF16 = jnp.bfloat16

D_MODEL = 4096
N_META = 16
BLOCK = 128
N_PAD = BLOCK - N_META

N_HEADS = 32
N_KV_HEADS = 4
HEAD_DIM = 64
HEADS_PER_KV = N_HEADS // N_KV_HEADS
ATTN_WIDTH = N_HEADS * HEAD_DIM
KV_WIDTH = N_KV_HEADS * HEAD_DIM

SSM_WIDTH = 1024
SSM_GROUP = 16
N_SSM_GROUPS = SSM_WIDTH // SSM_GROUP
SSM_STATE = 64
SSM_CHUNK = 16
SSM_TILE_GROUPS = 128 // SSM_GROUP
SSM_TILES = N_SSM_GROUPS // SSM_TILE_GROUPS

N_EXPERTS = 64
TOP_K = 8
N_EXPERT_GROUPS = 8
EXPERTS_PER_GROUP = N_EXPERTS // N_EXPERT_GROUPS
TOPK_GROUPS = 4
EXPERT_DIM = 512
SHARED_DIM = 512
ROUTED_SCALE = 2.5
ROW_BLOCK = 256

IN_WIDTH = ATTN_WIDTH + 2 * KV_WIDTH + SSM_WIDTH + 2 * D_MODEL
COL_KV = ATTN_WIDTH
COL_U = ATTN_WIDTH + 2 * KV_WIDTH
COL_GA = COL_U + SSM_WIDTH
COL_GS = COL_GA + D_MODEL

DEEPNORM_ALPHA = 2.0 ** 0.25
LN_EPS = 1e-5
NEG_INF = -1e30

MIB = 1024 * 1024


def _params(semantics, vmem_mib):
    return pltpu.CompilerParams(dimension_semantics=semantics, vmem_limit_bytes=vmem_mib * MIB)


def _tile(n, pref):
    t = min(n, pref)
    while n % t:
        t -= 128
    return t


def _ln_rows(x, g, b):
    mu = jnp.mean(x, axis=-1, keepdims=True)
    xc = x - mu
    var = jnp.mean(xc * xc, axis=-1, keepdims=True)
    return xc * lax.rsqrt(var + LN_EPS) * g + b


LANES = 128
SLAB = D_MODEL // LANES
SLAB_PITCH = SLAB + 4


def _store_slabs(ref, y):
    m = y.shape[0]
    for s in range(SLAB):
        ref[pl.ds(s, m, stride=SLAB_PITCH), :] = y[:, s * LANES:(s + 1) * LANES]
    for s in range(SLAB, SLAB_PITCH):
        ref[pl.ds(s, m, stride=SLAB_PITCH), :] = jnp.zeros((m, LANES), F32)


def _ln_kernel(x_ref, g_ref, b_ref, o_ref, ob_ref, *maybe_slab_ref):
    y = _ln_rows(x_ref[...], g_ref[...], b_ref[...])
    o_ref[...] = y
    ob_ref[...] = y.astype(BF16)
    for os_ref in maybe_slab_ref:
        _store_slabs(os_ref, y)


def _layer_norm(x, g, b, slabs=False):
    m, d = x.shape
    tm = _tile(m, 256)
    out_specs = [pl.BlockSpec((tm, d), lambda i: (i, 0)), pl.BlockSpec((tm, d), lambda i: (i, 0))]
    out_shape = [jax.ShapeDtypeStruct((m, d), F32), jax.ShapeDtypeStruct((m, d), BF16)]
    if slabs:
        out_specs.append(pl.BlockSpec((tm * SLAB_PITCH, LANES), lambda i: (i, 0)))
        out_shape.append(jax.ShapeDtypeStruct((m * SLAB_PITCH, LANES), F32))
    return pl.pallas_call(
        _ln_kernel,
        grid=(m // tm,),
        in_specs=[pl.BlockSpec((tm, d), lambda i: (i, 0)),
                  pl.BlockSpec((1, d), lambda i: (0, 0)),
                  pl.BlockSpec((1, d), lambda i: (0, 0))],
        out_specs=out_specs,
        out_shape=out_shape,
        compiler_params=_params(("parallel",), 40),
        name="layer_norm",
    )(x, g.reshape(1, d), b.reshape(1, d))


def _proj_kernel(a_ref, w_ref, o_ref):
    o_ref[...] = jnp.dot(a_ref[...], w_ref[...].astype(BF16),
                         preferred_element_type=F32).astype(o_ref.dtype)


def _project(a, w, col0, n_cols, tm_pref=1024, tn=512):
    m, k = a.shape
    tm = _tile(m, tm_pref)
    off = col0 // tn
    return pl.pallas_call(
        _proj_kernel,
        grid=(m // tm, n_cols // tn),
        in_specs=[pl.BlockSpec((tm, k), lambda i, j: (i, 0)),
                  pl.BlockSpec((k, tn), lambda i, j: (0, j + off))],
        out_specs=pl.BlockSpec((tm, tn), lambda i, j: (i, j)),
        out_shape=jax.ShapeDtypeStruct((m, n_cols), BF16),
        compiler_params=_params(("parallel", "arbitrary"), 56),
        name="in_proj",
    )(a, w)


def _attn_kernel(sink_ref, q_ref, kvc_ref, kvp_ref, kvm_ref, o_ref):
    first = pl.program_id(0) == 0
    kv_all = jnp.concatenate([kvp_ref[...], kvc_ref[...], kvm_ref[N_PAD:BLOCK, :]], axis=0)
    n_keys = 2 * BLOCK + N_META
    rows = HEADS_PER_KV * BLOCK

    lane = lax.broadcasted_iota(jnp.int32, (BLOCK, BLOCK), 1)
    lo_q = lane < HEAD_DIM
    lane_kv = lax.broadcasted_iota(jnp.int32, (n_keys, BLOCK), 1)
    lo_kv = lane_kv < HEAD_DIM

    qi = lax.broadcasted_iota(jnp.int32, (rows, BLOCK), 0) & (BLOCK - 1)
    kj = lax.broadcasted_iota(jnp.int32, (rows, BLOCK), 1)
    from_prev = kj > qi
    hidden = from_prev & first
    local_head = lax.broadcasted_iota(jnp.int32, (rows, 1), 0) // BLOCK

    def dup_head(tile, odd):
        t = tile.astype(F32)
        r = pltpu.roll(t, HEAD_DIM, 1)
        keep = jnp.logical_not(lo_kv) if odd else lo_kv
        return jnp.where(keep, t, r).astype(BF16)

    for kh in range(N_KV_HEADS):
        t0 = (kh // 2) * BLOCK
        k2 = dup_head(kv_all[:, t0:t0 + BLOCK], kh % 2)
        v2 = dup_head(kv_all[:, KV_WIDTH + t0:KV_WIDTH + t0 + BLOCK], kh % 2)
        pieces = []
        for p in range(HEADS_PER_KV // 2):
            c0 = kh * HEADS_PER_KV * HEAD_DIM + p * BLOCK
            qp = q_ref[:, c0:c0 + BLOCK] * jnp.asarray(HEAD_DIM ** -0.5, BF16)
            zero = jnp.zeros_like(qp)
            pieces.append(jnp.where(lo_q, qp, zero))
            pieces.append(jnp.where(lo_q, zero, qp))
        qs = jnp.concatenate(pieces, axis=0)
        s = lax.dot_general(qs, k2, (((1,), (1,)), ((), ())), preferred_element_type=F32)
        band = jnp.where(from_prev, s[:, :BLOCK], s[:, BLOCK:2 * BLOCK])
        band = jnp.where(hidden, NEG_INF, band)
        meta = s[:, 2 * BLOCK:]
        sink = jnp.zeros((rows, 1), F32)
        for t in range(HEADS_PER_KV):
            sink = jnp.where(local_head == t, sink_ref[kh * HEADS_PER_KV + t], sink)
        m = jnp.maximum(jnp.maximum(jnp.max(band, axis=-1, keepdims=True),
                                    jnp.max(meta, axis=-1, keepdims=True)), sink)
        e_band = jnp.exp(band - m)
        e_meta = jnp.exp(meta - m)
        den = (jnp.sum(e_band, axis=-1, keepdims=True) + jnp.sum(e_meta, axis=-1, keepdims=True)
               + jnp.exp(sink - m))
        e = jnp.concatenate([jnp.where(from_prev, e_band, 0.0), jnp.where(from_prev, 0.0, e_band), e_meta],
                            axis=1).astype(BF16)
        o2 = jnp.dot(e, v2, preferred_element_type=F32) / den
        for p in range(HEADS_PER_KV // 2):
            c0 = kh * HEADS_PER_KV * HEAD_DIM + p * BLOCK
            r0 = 2 * p * BLOCK
            o_ref[:, c0:c0 + BLOCK] = jnp.where(
                lo_q, o2[r0:r0 + BLOCK], o2[r0 + BLOCK:r0 + 2 * BLOCK]).astype(BF16)


def _attention(z, z_lead, sinks):
    t = z.shape[0]
    kv_blk = COL_KV // (2 * KV_WIDTH)
    return pl.pallas_call(
        _attn_kernel,
        grid=(t // BLOCK,),
        in_specs=[pl.BlockSpec(memory_space=pltpu.SMEM),
                  pl.BlockSpec((BLOCK, ATTN_WIDTH), lambda n: (n, 0)),
                  pl.BlockSpec((BLOCK, 2 * KV_WIDTH), lambda n: (n, kv_blk)),
                  pl.BlockSpec((BLOCK, 2 * KV_WIDTH), lambda n: (jnp.maximum(n - 1, 0), kv_blk)),
                  pl.BlockSpec((BLOCK, 2 * KV_WIDTH), lambda n: (0, 0))],
        out_specs=pl.BlockSpec((BLOCK, ATTN_WIDTH), lambda n: (n, 0)),
        out_shape=jax.ShapeDtypeStruct((t, ATTN_WIDTH), BF16),
        compiler_params=_params(("parallel",), 40),
        name="swa_attention",
    )(sinks.astype(F32), z, z, z, z_lead)


def _cmul(ar, ai, br, bi):
    return ar * br - ai * bi, ar * bi + ai * br


def _ssm_tables(lam_re, lam_im, log_dt, b_re, b_im, c_re, c_im, d_skip, n_chunks):
    hi = lax.Precision.HIGHEST
    g, p, h, c = N_SSM_GROUPS, SSM_STATE, SSM_GROUP, SSM_CHUNK
    dt = jnp.exp(log_dt)[:, None]
    mag = jnp.exp(dt * lam_re)
    are = mag * jnp.cos(dt * lam_im)
    aim = mag * jnp.sin(dt * lam_im)
    den = lam_re * lam_re + lam_im * lam_im
    num_re = are - 1.0
    coef_re = (num_re * lam_re + aim * lam_im) / den
    coef_im = (aim * lam_re - num_re * lam_im) / den
    bb_re = coef_re[..., None] * b_re - coef_im[..., None] * b_im
    bb_im = coef_re[..., None] * b_im + coef_im[..., None] * b_re

    pr, pi = [jnp.ones_like(are)], [jnp.zeros_like(are)]
    for _ in range(c):
        r, i = _cmul(pr[-1], pi[-1], are, aim)
        pr.append(r)
        pi.append(i)
    pr, pi = jnp.stack(pr), jnp.stack(pi)

    m_re = pr[:c, :, :, None] * bb_re[None] - pi[:c, :, :, None] * bb_im[None]
    m_im = pr[:c, :, :, None] * bb_im[None] + pi[:c, :, :, None] * bb_re[None]
    lag = (jnp.einsum('ghp,kgpj->kghj', c_re, m_re, precision=hi)
           - jnp.einsum('ghp,kgpj->kghj', c_im, m_im, precision=hi))
    nt, gt = SSM_TILES, SSM_TILE_GROUPS
    lagc = lag.reshape(c, nt, gt, h, h).transpose(1, 0, 2, 4, 3)
    lagc = lagc.reshape(nt, c, gt * h, h)
    rev = np.arange(c - 1, -1, -1)
    wo_re = pr[rev][:, :, :, None] * bb_re[None] - pi[rev][:, :, :, None] * bb_im[None]
    wo_im = pr[rev][:, :, :, None] * bb_im[None] + pi[rev][:, :, :, None] * bb_re[None]

    def lanes(first, second):
        w = jnp.stack([first, second]).reshape(2, c, nt, gt, p, h).transpose(2, 1, 3, 5, 0, 4)
        return w.reshape(nt, c, gt * h, 2 * p)

    wout_a, wout_b = lanes(wo_re, wo_im), lanes(wo_im, wo_re)
    cin_re = c_re[None] * pr[1:, :, None, :] - c_im[None] * pi[1:, :, None, :]
    cin_im = -(c_re[None] * pi[1:, :, None, :] + c_im[None] * pr[1:, :, None, :])
    cinc = jnp.stack([cin_re, cin_im]).reshape(2, c, nt, gt, h, p)
    cinc = cinc.transpose(2, 1, 0, 3, 5, 4).reshape(nt, c, 2 * gt * p, h)
    spread = jnp.tile(jnp.eye(h, dtype=BF16), (1, gt))

    n_steps = max(1, math.ceil(math.log2(n_chunks)))
    qr, qi = [pr[c]], [pi[c]]
    for _ in range(n_steps - 1):
        r, i = _cmul(qr[-1], qi[-1], qr[-1], qi[-1])
        qr.append(r)
        qi.append(i)
    apr = jnp.stack(qr, axis=0).reshape(n_steps, nt, gt * p).transpose(1, 0, 2)
    api = jnp.stack(qi, axis=0).reshape(n_steps, nt, gt * p).transpose(1, 0, 2)
    dsk = d_skip.reshape(nt, 1, gt * h)
    return (lagc.astype(BF16), wout_a.astype(BF16), wout_b.astype(BF16), cinc.astype(BF16), spread,
            apr, api, dsk, n_steps)


def _chunk_steps(u_ref):
    n = u_ref.shape[0] // SSM_CHUNK
    return [u_ref[pl.ds(s, n, stride=SSM_CHUNK), :] for s in range(SSM_CHUNK)]


def _chunk_lanes(u_ref):
    return jnp.concatenate(_chunk_steps(u_ref), axis=1)


def _ssm_state_kernel(u_ref, lead_ref, wa_ref, wb_ref, apr_ref, api_ref, prev_ref, wout_s, *, n_steps):
    p, gh = SSM_STATE, SSM_TILE_GROUPS * SSM_GROUP
    row_g = lax.broadcasted_iota(jnp.int32, (gh, 2 * p), 0) // SSM_GROUP
    lo = lax.broadcasted_iota(jnp.int32, (gh, 2 * p), 1) < p
    zero = jnp.zeros((gh, 2 * p), BF16)
    for s in range(SSM_CHUNK):
        a, b = wa_ref[0, s], wb_ref[0, s]
        for j in range(SSM_TILE_GROUPS // 2):
            even, odd = (row_g == 2 * j) & lo, (row_g == 2 * j + 1) & jnp.logical_not(lo)
            wout_s[s * gh:(s + 1) * gh, j * 2 * p:(j + 1) * 2 * p] = (
                jnp.where(even, a, jnp.where(odd, b, zero)))
            wout_s[s * gh:(s + 1) * gh, (SSM_TILE_GROUPS // 2 + j) * 2 * p:(SSM_TILE_GROUPS // 2 + j + 1) * 2 * p] = (
                jnp.where(even, b, jnp.where(odd, a, zero)))
    wout = wout_s[...]
    st = jnp.dot(_chunk_lanes(u_ref).astype(BF16), wout, preferred_element_type=F32)
    lead = jnp.dot(_chunk_lanes(lead_ref).astype(BF16), wout, preferred_element_type=F32)
    nc, half = st.shape[0], st.shape[1] // 2
    xr, xi = st[:, :half], st[:, half:]
    mr, mi = lead[-1:, :half], lead[-1:, half:]
    row = lax.broadcasted_iota(jnp.int32, (nc, half), 0)
    ar, ai = apr_ref[0, 0:1, :], api_ref[0, 0:1, :]
    xr = xr + jnp.where(row == 0, ar * mr - ai * mi, 0.0)
    xi = xi + jnp.where(row == 0, ar * mi + ai * mr, 0.0)

    def shift(x, d):
        return jnp.where(row >= d, pltpu.roll(x, d, 0), 0.0)

    for j in range(n_steps):
        ar, ai = apr_ref[0, j:j + 1, :], api_ref[0, j:j + 1, :]
        sr, si = shift(xr, 1 << j), shift(xi, 1 << j)
        xr, xi = xr + (ar * sr - ai * si), xi + (ar * si + ai * sr)
    pr = jnp.where(row == 0, mr, shift(xr, 1))
    pi = jnp.where(row == 0, mi, shift(xi, 1))
    prev_ref[0] = jnp.concatenate([pr, pi], axis=1).astype(BF16)


def _ssm_out_kernel(u_ref, prev_ref, lag_ref, cin_ref, spread_ref, d_ref, y_ref, toep_s, wcin_s):
    gh, c = SSM_TILE_GROUPS * SSM_GROUP, SSM_CHUNK

    @pl.when(pl.program_id(1) == 0)
    def _():
        spread = spread_ref[...]
        diag = (lax.broadcasted_iota(jnp.int32, (gh, gh), 0) // SSM_GROUP
                == lax.broadcasted_iota(jnp.int32, (gh, gh), 1) // SSM_GROUP)
        zero = jnp.zeros((gh, gh), BF16)
        blocks = [jnp.where(diag, jnp.dot(lag_ref[0, k], spread, preferred_element_type=F32), 0.0).astype(BF16)
                  for k in range(c)]
        for s in range(c):
            for i in range(c):
                toep_s[s * gh:(s + 1) * gh, i * gh:(i + 1) * gh] = blocks[i - s] if i >= s else zero
        rows = wcin_s.shape[0]
        row_g = (lax.broadcasted_iota(jnp.int32, (rows, gh), 0) % (rows // 2)) // SSM_STATE
        own = row_g == lax.broadcasted_iota(jnp.int32, (rows, gh), 1) // SSM_GROUP
        for i in range(c):
            wcin_s[:, i * gh:(i + 1) * gh] = jnp.where(
                own, jnp.dot(cin_ref[0, i], spread, preferred_element_type=F32), 0.0).astype(BF16)

    us = _chunk_steps(u_ref)
    cb = us[0].shape[0]
    u = jnp.concatenate(us, axis=1).astype(BF16)
    y = (jnp.dot(u, toep_s[...], preferred_element_type=F32)
         + jnp.dot(prev_ref[0], wcin_s[...], preferred_element_type=F32))
    for s in range(SSM_CHUNK):
        ys = y[:, s * 128:(s + 1) * 128] + d_ref[0] * us[s]
        y_ref[pl.ds(s, cb, stride=SSM_CHUNK), :] = 0.5 * ys * (1.0 + lax.erf(ys * (0.5 ** 0.5)))


def _ssm(u, u_lead, tables):
    lagc, wout_a, wout_b, cinc, spread, apr, api, dsk, n_steps = tables
    t = u.shape[0]
    nc = t // SSM_CHUNK
    cb = _tile(nc, 256) if nc % 128 == 0 else nc
    lanes = 128
    k = SSM_CHUNK * lanes
    states = 2 * SSM_TILE_GROUPS * SSM_STATE
    prev = pl.pallas_call(
        functools.partial(_ssm_state_kernel, n_steps=n_steps),
        grid=(SSM_TILES,),
        in_specs=[pl.BlockSpec((t, lanes), lambda g: (0, g)),
                  pl.BlockSpec((BLOCK, lanes), lambda g: (0, g)),
                  pl.BlockSpec((1, SSM_CHUNK, lanes, 2 * SSM_STATE), lambda g: (g, 0, 0, 0)),
                  pl.BlockSpec((1, SSM_CHUNK, lanes, 2 * SSM_STATE), lambda g: (g, 0, 0, 0)),
                  pl.BlockSpec((1, n_steps, states // 2), lambda g: (g, 0, 0)),
                  pl.BlockSpec((1, n_steps, states // 2), lambda g: (g, 0, 0))],
        out_specs=pl.BlockSpec((1, nc, states), lambda g: (g, 0, 0)),
        out_shape=jax.ShapeDtypeStruct((SSM_TILES, nc, states), BF16),
        scratch_shapes=[pltpu.VMEM((k, states), BF16)],
        compiler_params=_params(("parallel",), 48),
        name="s5_states",
    )(u, u_lead, wout_a, wout_b, apr, api)
    return pl.pallas_call(
        _ssm_out_kernel,
        grid=(SSM_TILES, nc // cb),
        in_specs=[pl.BlockSpec((cb * SSM_CHUNK, lanes), lambda g, c: (c, g)),
                  pl.BlockSpec((1, cb, states), lambda g, c: (g, c, 0)),
                  pl.BlockSpec((1, SSM_CHUNK, lanes, SSM_GROUP), lambda g, c: (g, 0, 0, 0)),
                  pl.BlockSpec((1, SSM_CHUNK, states, SSM_GROUP), lambda g, c: (g, 0, 0, 0)),
                  pl.BlockSpec((SSM_GROUP, lanes), lambda g, c: (0, 0)),
                  pl.BlockSpec((1, 1, lanes), lambda g, c: (g, 0, 0))],
        out_specs=pl.BlockSpec((cb * SSM_CHUNK, lanes), lambda g, c: (c, g)),
        out_shape=jax.ShapeDtypeStruct((t, SSM_WIDTH), F32),
        scratch_shapes=[pltpu.VMEM((k, k), BF16), pltpu.VMEM((states, k), BF16)],
        compiler_params=_params(("parallel", "arbitrary"), 48),
        name="s5_outputs",
    )(u, prev, lagc, cinc, spread, dsk)


def _glu_kernel(y_ref, w_ref, o_ref):
    y = y_ref[...]
    gate = jax.nn.sigmoid(jnp.dot(y.astype(BF16), w_ref[...].astype(BF16), preferred_element_type=F32))
    o_ref[...] = (y * gate).astype(BF16)


def _glu(y, w):
    m, d = y.shape
    tm = _tile(m, 1024)
    return pl.pallas_call(
        _glu_kernel,
        grid=(m // tm,),
        in_specs=[pl.BlockSpec((tm, d), lambda i: (i, 0)),
                  pl.BlockSpec((d, d), lambda i: (0, 0))],
        out_specs=pl.BlockSpec((tm, d), lambda i: (i, 0)),
        out_shape=jax.ShapeDtypeStruct((m, d), BF16),
        compiler_params=_params(("parallel",), 40),
        name="ssm_glu",
    )(y, w)


def _mix_kernel(a_ref, y_ref, wa_ref, ws_ref, ga_ref, gs_ref, o_ref):
    attn_d = jnp.dot(a_ref[...], wa_ref[...].astype(BF16), preferred_element_type=F32)
    ssm_d = jnp.dot(y_ref[...], ws_ref[...].astype(BF16), preferred_element_type=F32)
    mix = (jax.nn.sigmoid(ga_ref[...].astype(F32)) * attn_d
           + jax.nn.sigmoid(gs_ref[...].astype(F32)) * ssm_d)
    o_ref[...] = mix.astype(BF16)


def _mix(attn, y2, w_attn_o, w_ssm_o, z):
    m = attn.shape[0]
    tm, tn = _tile(m, 1024), 512
    ga_blk, gs_blk = COL_GA // tn, COL_GS // tn
    return pl.pallas_call(
        _mix_kernel,
        grid=(m // tm, D_MODEL // tn),
        in_specs=[pl.BlockSpec((tm, ATTN_WIDTH), lambda i, j: (i, 0)),
                  pl.BlockSpec((tm, SSM_WIDTH), lambda i, j: (i, 0)),
                  pl.BlockSpec((ATTN_WIDTH, tn), lambda i, j: (0, j)),
                  pl.BlockSpec((SSM_WIDTH, tn), lambda i, j: (0, j)),
                  pl.BlockSpec((tm, tn), lambda i, j: (i, j + ga_blk)),
                  pl.BlockSpec((tm, tn), lambda i, j: (i, j + gs_blk))],
        out_specs=pl.BlockSpec((tm, tn), lambda i, j: (i, j)),
        out_shape=jax.ShapeDtypeStruct((m, D_MODEL), BF16),
        compiler_params=_params(("parallel", "arbitrary"), 48),
        name="gated_merge",
    )(attn, y2, w_attn_o, w_ssm_o, z, z)


def _resid_kernel(a_ref, w_ref, h_ref, o_ref):
    o_ref[...] = DEEPNORM_ALPHA * h_ref[...] + jnp.dot(
        a_ref[...], w_ref[...].astype(BF16), preferred_element_type=F32)


def _out_proj(mix, w_out, h0):
    m = mix.shape[0]
    tm, tn = _tile(m, 1024), 512
    return pl.pallas_call(
        _resid_kernel,
        grid=(m // tm, D_MODEL // tn),
        in_specs=[pl.BlockSpec((tm, D_MODEL), lambda i, j: (i, 0)),
                  pl.BlockSpec((D_MODEL, tn), lambda i, j: (0, j)),
                  pl.BlockSpec((tm, tn), lambda i, j: (i, j))],
        out_specs=pl.BlockSpec((tm, tn), lambda i, j: (i, j)),
        out_shape=jax.ShapeDtypeStruct((m, D_MODEL), F32),
        compiler_params=_params(("parallel", "arbitrary"), 56),
        name="out_proj",
    )(mix, w_out, h0)


def _token_mixer(x, meta_tokens, ln0_g, ln0_b, w_in, attn_sinks, w_attn_o, ssm_tables_args,
                 w_glu, w_ssm_o, w_out, ln1_g, ln1_b):
    t = x.shape[0]
    lead = jnp.concatenate([jnp.zeros((N_PAD, D_MODEL), F32), meta_tokens.astype(F32)], axis=0)
    h0, h0b = _layer_norm(x, ln0_g, ln0_b)
    _, leadb = _layer_norm(lead, ln0_g, ln0_b)

    z = _project(h0b, w_in, 0, IN_WIDTH)
    z_lead = _project(leadb, w_in, COL_KV, 2 * KV_WIDTH + SSM_WIDTH)

    attn = _attention(z, z_lead, attn_sinks)

    lead_valid = (jnp.arange(BLOCK) >= N_PAD)[:, None]
    u_lead = jnp.where(lead_valid, z_lead[:, 2 * KV_WIDTH:].astype(F32), 0.0)
    tables = _ssm_tables(*ssm_tables_args, n_chunks=t // SSM_CHUNK)
    y = _ssm(z[:, COL_U:COL_GA].astype(F32), u_lead, tables)
    y2 = _glu(y, w_glu)

    mix = _mix(attn, y2, w_attn_o, w_ssm_o, z)
    r = _out_proj(mix, w_out, h0)
    return _layer_norm(r, ln1_g, ln1_b, slabs=True)


def _router_kernel(h_ref, wr_ref, bias_ref, idx_ref, wt_ref, pos_ref, cnt_ref):
    tm = h_ref.shape[0]

    @pl.when(pl.program_id(0) == 0)
    def _():
        cnt_ref[...] = jnp.zeros_like(cnt_ref)

    ng, ne = N_EXPERT_GROUPS, EXPERTS_PER_GROUP
    logits = lax.dot_general(wr_ref[...], h_ref[...], (((1,), (1,)), ((), ())),
                             precision=lax.Precision.HIGHEST, preferred_element_type=F32)
    scores = jax.nn.sigmoid(logits)
    sel = (scores + bias_ref[...]).reshape(ng, ne, tm)
    scores = scores.reshape(ng, ne, tm)
    e_in = lax.broadcasted_iota(jnp.int32, (ng, ne, tm), 1)
    e_id = lax.broadcasted_iota(jnp.int32, (ng, ne, tm), 0) * ne + e_in
    neg = -jnp.inf

    top1 = jnp.max(sel, axis=1, keepdims=True)
    first = jnp.min(jnp.where(sel == top1, e_in, ne), axis=1, keepdims=True)
    top2 = jnp.max(jnp.where(e_in == first, neg, sel), axis=1, keepdims=True)
    grp = (top1 + top2)[:, 0, :]

    g_id = lax.broadcasted_iota(jnp.int32, (ng, tm), 0)
    chosen = jnp.zeros((ng, tm), jnp.bool_)
    for _ in range(TOPK_GROUPS):
        best = jnp.max(grp, axis=0, keepdims=True)
        pick = g_id == jnp.min(jnp.where(grp == best, g_id, ng), axis=0, keepdims=True)
        chosen = chosen | pick
        grp = jnp.where(pick, neg, grp)

    cur = jnp.where(chosen[:, None, :], sel, NEG_INF)
    def pick_sum(pick, values):
        part = jnp.sum(jnp.where(pick, values, 0.0), axis=1, keepdims=True)
        return jnp.sum(part, axis=0, keepdims=True)[0]

    ids, wts, picks = [], [], []
    for _ in range(TOP_K):
        best = jnp.max(jnp.max(cur, axis=1, keepdims=True), axis=0, keepdims=True)
        cand = jnp.where(cur == best, e_id, N_EXPERTS)
        win = jnp.min(jnp.min(cand, axis=1, keepdims=True), axis=0, keepdims=True)
        pick = e_id == win
        ids.append(win[0])
        wts.append(pick_sum(pick, scores))
        picks.append(pick)
        cur = jnp.where(pick, neg, cur)
    w = jnp.concatenate(wts, axis=0)
    idx_ref[...] = jnp.concatenate(ids, axis=0)
    wt_ref[...] = w / jnp.sum(w, axis=0, keepdims=True) * ROUTED_SCALE

    taken = jnp.where(cur == neg, 1.0, 0.0).reshape(N_EXPERTS, tm)
    before = (lax.broadcasted_iota(jnp.int32, (tm, tm), 0)
              < lax.broadcasted_iota(jnp.int32, (tm, tm), 1))
    rank = jnp.dot(taken.astype(BF16), jnp.where(before, 1.0, 0.0).astype(BF16),
                   preferred_element_type=F32)
    pos = (cnt_ref[:, 0:1] + rank).reshape(ng, ne, tm)
    pos_ref[...] = jnp.concatenate([pick_sum(p, pos) for p in picks], axis=0).astype(jnp.int32)
    cnt_ref[...] = cnt_ref[...] + jnp.sum(taken, axis=1, keepdims=True)


def _router(h1, w_router, router_bias):
    t, d = h1.shape
    tm = _tile(t, 512)
    return pl.pallas_call(
        _router_kernel,
        grid=(t // tm,),
        in_specs=[pl.BlockSpec((tm, d), lambda i: (i, 0)),
                  pl.BlockSpec((N_EXPERTS, d), lambda i: (0, 0)),
                  pl.BlockSpec((N_EXPERTS, 1), lambda i: (0, 0))],
        out_specs=[pl.BlockSpec((TOP_K, tm), lambda i: (0, i)),
                   pl.BlockSpec((TOP_K, tm), lambda i: (0, i)),
                   pl.BlockSpec((TOP_K, tm), lambda i: (0, i)),
                   pl.BlockSpec((N_EXPERTS, 128), lambda i: (0, 0))],
        out_shape=[jax.ShapeDtypeStruct((TOP_K, t), jnp.int32),
                   jax.ShapeDtypeStruct((TOP_K, t), F32),
                   jax.ShapeDtypeStruct((TOP_K, t), jnp.int32),
                   jax.ShapeDtypeStruct((N_EXPERTS, 128), F32)],
        compiler_params=_params(("arbitrary",), 40),
        name="router",
    )(h1, w_router.T, router_bias.reshape(N_EXPERTS, 1))


def _dispatch(idx_t, pos_t, counts):
    t = idx_t.shape[1]
    n_assign = TOP_K * t
    n_blocks = -(-n_assign // ROW_BLOCK) + N_EXPERTS
    n_rows = n_blocks * ROW_BLOCK
    counts = counts[:, 0].astype(jnp.int32)
    padded = (counts + ROW_BLOCK - 1) // ROW_BLOCK * ROW_BLOCK
    ends = jnp.cumsum(padded)
    starts = ends - padded
    own = idx_t[:, :, None] == jnp.arange(N_EXPERTS, dtype=jnp.int32)[None, None, :]
    dest = (jnp.sum(jnp.where(own, starts[None, None, :], 0), axis=-1) + pos_t).reshape(-1)
    tok_flat = jnp.tile(jnp.arange(t, dtype=jnp.int32), TOP_K)
    row_tok = jnp.zeros((n_rows,), jnp.int32).at[dest].set(tok_flat)
    block_start = jnp.arange(n_blocks, dtype=jnp.int32) * ROW_BLOCK
    block_expert = jnp.minimum(
        jnp.sum((ends[None, :] <= block_start[:, None]).astype(jnp.int32), axis=1), N_EXPERTS - 1)
    n_used = (ends[-1] // ROW_BLOCK).astype(jnp.int32).reshape(1)
    has = counts > 0
    cand = jnp.where(has[None, :] & (jnp.arange(N_EXPERTS)[None, :] > jnp.arange(N_EXPERTS)[:, None]),
                     jnp.arange(N_EXPERTS)[None, :], N_EXPERTS)
    nxt = jnp.min(cand, axis=1)
    next_expert = jnp.where(nxt < N_EXPERTS, nxt, -1).astype(jnp.int32)
    return (row_tok.reshape(n_blocks, 1, ROW_BLOCK), block_expert, n_used, next_expert,
            dest.reshape(TOP_K, t))


WEIGHT_DMA_PRIORITY = 1


def _gather_rows(src_hbm, idx_smem, dst, sem, n, unrolled=False):
    def body(i, carry):
        pltpu.make_async_copy(src_hbm.at[pl.ds(idx_smem[0, i], 1)], dst.at[pl.ds(i, 1)], sem).start()
        return carry
    if unrolled:
        for i in range(n):
            body(i, 0)
    else:
        lax.fori_loop(0, n, body, 0, unroll=8)


def _wait_rows(src_hbm, dst, sem, n):
    pltpu.make_async_copy(src_hbm.at[pl.ds(0, n)], dst, sem).wait()


def _gather_slabs(src_hbm, idx_smem, dst, sem, n, unrolled=False, dst_pitch=None):
    dst_pitch = SLAB_PITCH if dst_pitch is None else dst_pitch

    def body(i, carry):
        src = idx_smem[0, i] * SLAB_PITCH
        pltpu.make_async_copy(src_hbm.at[pl.ds(src, SLAB)], dst.at[pl.ds(i * dst_pitch, SLAB)], sem).start()
        return carry
    if unrolled:
        for i in range(n):
            body(i, 0)
    else:
        lax.fori_loop(0, n, body, 0, unroll=8)


def _wait_slabs(src_hbm, dst, sem, n):
    pltpu.make_async_copy(src_hbm.at[pl.ds(0, n * SLAB)], dst.at[pl.ds(0, n * SLAB)], sem).wait()


def _slab_rows(ref, n, pitch=None):
    pitch = SLAB_PITCH if pitch is None else pitch
    return jnp.concatenate([ref[pl.ds(s, n, stride=pitch), :] for s in range(SLAB)], axis=1)


def _expert_up_kernel(be_ref, nu_ref, nx_ref, tok_hbm, h_hbm, w1_hbm, w3_hbm, o_ref,
                      idx_smem, xbuf, wstage, wcast, isem, xsem, wsem):
    b = pl.program_id(0)
    n_used = nu_ref[0]
    last = n_used - 1
    e = be_ref[b]
    slot = b % 2

    def idx_copy(blk, s):
        return pltpu.make_async_copy(tok_hbm.at[blk], idx_smem.at[s], isem.at[s])

    def weight_copies(ex):
        return (pltpu.make_async_copy(w1_hbm.at[ex], wstage.at[0], wsem.at[0]),
                pltpu.make_async_copy(w3_hbm.at[ex], wstage.at[1], wsem.at[1]))

    @pl.when(b == 0)
    def _():
        for c in weight_copies(e):
            c.start(priority=WEIGHT_DMA_PRIORITY)
        idx_copy(0, 0).start()
        idx_copy(0, 0).wait()
        _gather_slabs(h_hbm, idx_smem.at[0], xbuf.at[0], xsem.at[0], ROW_BLOCK)
        idx_copy(jnp.minimum(1, last), 1).start()

    @pl.when(b < n_used)
    def _():
        first_of_expert = jnp.logical_or(b == 0, be_ref[jnp.maximum(b - 1, 0)] != e)

        @pl.when(first_of_expert)
        def _():
            for c in weight_copies(e):
                c.wait()
            rows = wstage.shape[1] // 8
            for m in range(2):
                for r in range(8):
                    wcast[m, r * rows:(r + 1) * rows] = wstage[m, r * rows:(r + 1) * rows].astype(BF16)
            nxt = nx_ref[e]

            @pl.when(nxt >= 0)
            def _():
                for c in weight_copies(nxt):
                    c.start(priority=WEIGHT_DMA_PRIORITY)

        idx_copy(jnp.minimum(b + 1, last), 1 - slot).wait()
        _wait_slabs(h_hbm, xbuf.at[slot], xsem.at[slot], ROW_BLOCK)
        _gather_slabs(h_hbm, idx_smem.at[1 - slot], xbuf.at[1 - slot], xsem.at[1 - slot], ROW_BLOCK,
                      unrolled=True)
        idx_copy(jnp.minimum(b + 2, last), slot).start()
        x = _slab_rows(xbuf.at[slot], ROW_BLOCK).astype(BF16)
        a = jnp.dot(x, wcast[0], preferred_element_type=F32)
        g = jnp.dot(x, wcast[1], preferred_element_type=F32)
        o_ref[...] = (a * jax.nn.sigmoid(a) * g).astype(BF16)

        @pl.when(b == last)
        def _():
            _wait_slabs(h_hbm, xbuf.at[1 - slot], xsem.at[1 - slot], ROW_BLOCK)
            idx_copy(last, slot).wait()

    @pl.when(b >= n_used)
    def _():
        o_ref[...] = jnp.zeros_like(o_ref)


def _expert_up(h1, row_tok, block_expert, n_used, next_expert, w1, w3):
    n_blocks = row_tok.shape[0]
    d = w1.shape[1]
    grid_spec = pltpu.PrefetchScalarGridSpec(
        num_scalar_prefetch=3,
        grid=(n_blocks,),
        in_specs=[pl.BlockSpec(memory_space=pl.ANY)] * 4,
        out_specs=pl.BlockSpec((ROW_BLOCK, EXPERT_DIM), lambda b, *_: (b, 0)),
        scratch_shapes=[pltpu.SMEM((2, 1, ROW_BLOCK), jnp.int32),
                        pltpu.VMEM((2, ROW_BLOCK * SLAB_PITCH, LANES), F32),
                        pltpu.VMEM((2, d, EXPERT_DIM), F32),
                        pltpu.VMEM((2, d, EXPERT_DIM), BF16),
                        pltpu.SemaphoreType.DMA((2,)),
                        pltpu.SemaphoreType.DMA((2,)),
                        pltpu.SemaphoreType.DMA((2,))])
    return pl.pallas_call(
        _expert_up_kernel,
        grid_spec=grid_spec,
        out_shape=jax.ShapeDtypeStruct((n_blocks * ROW_BLOCK, EXPERT_DIM), BF16),
        compiler_params=_params(("arbitrary",), 58),
        name="expert_up",
    )(block_expert, n_used, next_expert, row_tok, h1, w1, w3)


def _expert_down_kernel(be_ref, nu_ref, nx_ref, a_ref, w2_hbm, o_ref, wstage, wcast, wsem):
    b = pl.program_id(0)
    n_used = nu_ref[0]
    e = be_ref[b]

    def weight_copy(ex):
        return pltpu.make_async_copy(w2_hbm.at[ex], wstage, wsem)

    @pl.when(b == 0)
    def _():
        weight_copy(e).start(priority=WEIGHT_DMA_PRIORITY)

    @pl.when(b < n_used)
    def _():
        first_of_expert = jnp.logical_or(b == 0, be_ref[jnp.maximum(b - 1, 0)] != e)

        @pl.when(first_of_expert)
        def _():
            weight_copy(e).wait()
            wcast[...] = wstage[...].astype(BF16)
            nxt = nx_ref[e]

            @pl.when(nxt >= 0)
            def _():
                weight_copy(nxt).start(priority=WEIGHT_DMA_PRIORITY)

        _store_slabs(o_ref, jnp.dot(a_ref[...], wcast[...], preferred_element_type=F32))

    @pl.when(b >= n_used)
    def _():
        o_ref[...] = jnp.zeros_like(o_ref)


def _expert_down(act, block_expert, n_used, next_expert, w2):
    n_rows = act.shape[0]
    n_blocks = n_rows // ROW_BLOCK
    d = w2.shape[2]
    grid_spec = pltpu.PrefetchScalarGridSpec(
        num_scalar_prefetch=3,
        grid=(n_blocks,),
        in_specs=[pl.BlockSpec((ROW_BLOCK, EXPERT_DIM), lambda b, *_: (b, 0)),
                  pl.BlockSpec(memory_space=pl.ANY)],
        out_specs=pl.BlockSpec((ROW_BLOCK * SLAB_PITCH, LANES), lambda b, *_: (b, 0)),
        scratch_shapes=[pltpu.VMEM((EXPERT_DIM, d), F32),
                        pltpu.VMEM((EXPERT_DIM, d), BF16),
                        pltpu.SemaphoreType.DMA(())])
    return pl.pallas_call(
        _expert_down_kernel,
        grid_spec=grid_spec,
        out_shape=jax.ShapeDtypeStruct((n_rows * SLAB_PITCH, LANES), F32),
        compiler_params=_params(("arbitrary",), 40),
        name="expert_down",
    )(block_expert, n_used, next_expert, act, w2)


def _shared_kernel(h_ref, w1_ref, w3_ref, w2_ref, o_ref):
    x = h_ref[...]
    a = jnp.dot(x, w1_ref[...], preferred_element_type=F32)
    g = jnp.dot(x, w3_ref[...], preferred_element_type=F32)
    act = (a * jax.nn.sigmoid(a) * g).astype(BF16)
    o_ref[...] = jnp.dot(act, w2_ref[...], preferred_element_type=F32)


def _shared_expert(h1b, w1, w3, w2):
    t, d = h1b.shape
    tm = _tile(t, 512)
    return pl.pallas_call(
        _shared_kernel,
        grid=(t // tm,),
        in_specs=[pl.BlockSpec((tm, d), lambda i: (i, 0)),
                  pl.BlockSpec((d, SHARED_DIM), lambda i: (0, 0)),
                  pl.BlockSpec((d, SHARED_DIM), lambda i: (0, 0)),
                  pl.BlockSpec((SHARED_DIM, d), lambda i: (0, 0))],
        out_specs=pl.BlockSpec((tm, d), lambda i: (i, 0)),
        out_shape=jax.ShapeDtypeStruct((t, d), F32),
        compiler_params=_params(("parallel",), 56),
        name="shared_expert",
    )(h1b, w1.astype(BF16), w3.astype(BF16), w2.astype(BF16))


COMBINE_TOKENS = 64
COMBINE_PITCH = SLAB + 8


def _combine_kernel(dest_hbm, y_hbm, h_ref, s_ref, w_ref, g_ref, b_ref, o_ref,
                    idx_smem, ybuf, stage, isem, ysem):
    i = pl.program_id(0)
    n = pl.num_programs(0)
    slot = i % 2
    rows = TOP_K * COMBINE_TOKENS

    def idx_copy(blk, s):
        return pltpu.make_async_copy(dest_hbm.at[blk], idx_smem.at[s], isem.at[s])

    last = n - 1

    pitch = COMBINE_PITCH
    per_j = COMBINE_TOKENS * pitch

    @pl.when(i == 0)
    def _():
        ybuf[...] = jnp.zeros_like(ybuf)
        idx_copy(0, 0).start()
        idx_copy(0, 0).wait()
        _gather_slabs(y_hbm, idx_smem.at[0], ybuf.at[0], ysem.at[0], rows, dst_pitch=pitch)
        idx_copy(jnp.minimum(1, last), 1).start()

    idx_copy(jnp.minimum(i + 1, last), 1 - slot).wait()
    _wait_slabs(y_hbm, ybuf.at[slot], ysem.at[slot], rows)
    _gather_slabs(y_hbm, idx_smem.at[1 - slot], ybuf.at[1 - slot], ysem.at[1 - slot], rows,
                  unrolled=True, dst_pitch=pitch)
    idx_copy(jnp.minimum(i + 2, last), slot).start()
    acc = None
    for j in range(TOP_K):
        wj = jnp.broadcast_to(w_ref[:, j:j + 1, :], (COMBINE_TOKENS, pitch, LANES)).reshape(per_j, LANES)
        term = wj * ybuf[slot, j * per_j:(j + 1) * per_j, :]
        acc = term if acc is None else acc + term
    stage[...] = acc
    routed = _slab_rows(stage, COMBINE_TOKENS, pitch)
    o_ref[...] = _ln_rows(DEEPNORM_ALPHA * h_ref[...] + s_ref[...] + routed, g_ref[...], b_ref[...])

    @pl.when(i == last)
    def _():
        _wait_slabs(y_hbm, ybuf.at[1 - slot], ysem.at[1 - slot], rows)
        idx_copy(last, slot).wait()


def _combine(yb, dest, w_t, h1, shared, g, b):
    t, d = h1.shape
    tm = COMBINE_TOKENS
    n_tiles = t // tm
    dest_tiles = dest.reshape(TOP_K, n_tiles, tm).transpose(1, 0, 2).reshape(n_tiles, 1, TOP_K * tm)
    w_lanes = jnp.broadcast_to(w_t.T[:, :, None], (t, TOP_K, LANES))
    return pl.pallas_call(
        _combine_kernel,
        grid=(n_tiles,),
        in_specs=[pl.BlockSpec(memory_space=pl.ANY),
                  pl.BlockSpec(memory_space=pl.ANY),
                  pl.BlockSpec((tm, d), lambda i: (i, 0)),
                  pl.BlockSpec((tm, d), lambda i: (i, 0)),
                  pl.BlockSpec((tm, TOP_K, LANES), lambda i: (i, 0, 0)),
                  pl.BlockSpec((1, d), lambda i: (0, 0)),
                  pl.BlockSpec((1, d), lambda i: (0, 0))],
        out_specs=pl.BlockSpec((tm, d), lambda i: (i, 0)),
        out_shape=jax.ShapeDtypeStruct((t, d), F32),
        scratch_shapes=[pltpu.SMEM((2, 1, TOP_K * tm), jnp.int32),
                        pltpu.VMEM((2, TOP_K * tm * COMBINE_PITCH, LANES), F32),
                        pltpu.VMEM((tm * COMBINE_PITCH, LANES), F32),
                        pltpu.SemaphoreType.DMA((2,)),
                        pltpu.SemaphoreType.DMA((2,))],
        compiler_params=_params(("arbitrary",), 48),
        name="combine_norm",
    )(dest_tiles, yb, h1, shared, w_lanes, g.reshape(1, d), b.reshape(1, d))


def _channel_mixer(h1, h1b, h1_slabs, w_router, router_bias, expert_w1, expert_w3, expert_w2,
                   shared_w1, shared_w3, shared_w2, ln2_g, ln2_b):
    idx_t, w_t, pos_t, counts = _router(h1, w_router, router_bias)
    row_tok, block_expert, n_used, next_expert, dest = _dispatch(idx_t, pos_t, counts)
    shared = _shared_expert(h1b, shared_w1, shared_w3, shared_w2)
    act = _expert_up(h1_slabs, row_tok, block_expert, n_used, next_expert, expert_w1, expert_w3)
    yb = _expert_down(act, block_expert, n_used, next_expert, expert_w2)
    return _combine(yb, dest, w_t, h1, shared, ln2_g, ln2_b)


def kernel(x, meta_tokens, ln0_g, ln0_b, w_in, attn_sinks, w_attn_o, ssm_lam_re, ssm_lam_im, ssm_log_dt,
           ssm_b_re, ssm_b_im, ssm_c_re, ssm_c_im, ssm_d, w_glu, w_ssm_o, w_out, ln1_g, ln1_b, w_router,
           router_bias, expert_w1, expert_w3, expert_w2, shared_w1, shared_w3, shared_w2, ln2_g, ln2_b):
    assert x.shape[0] == 1 and w_in.shape[0] == 1
    ssm_args = (ssm_lam_re[0], ssm_lam_im[0], ssm_log_dt[0], ssm_b_re[0], ssm_b_im[0],
                ssm_c_re[0], ssm_c_im[0], ssm_d[0])
    h1, h1b, h1_slabs = _token_mixer(x[0], meta_tokens, ln0_g, ln0_b, w_in[0], attn_sinks[0],
                                     w_attn_o[0], ssm_args, w_glu[0], w_ssm_o[0], w_out[0], ln1_g[0], ln1_b[0])
    out = _channel_mixer(h1, h1b, h1_slabs, w_router[0], router_bias[0], expert_w1[0], expert_w3[0],
                         expert_w2[0], shared_w1[0], shared_w3[0], shared_w2[0], ln2_g[0], ln2_b[0])
    return out[None]
```

```python
import functools
import math

import jax
import jax.numpy as jnp
import numpy as np
from jax import lax
from jax.experimental import pallas as pl
from jax.experimental.pallas import tpu as pltpu

F32 = jnp.float32
BF16 = jnp.bfloat16

D_MODEL = 4096
N_META = 16
BLOCK = 128
N_PAD = BLOCK - N_META

N_HEADS = 32
N_KV_HEADS = 4
HEAD_DIM = 64
HEADS_PER_KV = N_HEADS // N_KV_HEADS
ATTN_WIDTH = N_HEADS * HEAD_DIM
KV_WIDTH = N_KV_HEADS * HEAD_DIM

SSM_WIDTH = 1024
SSM_GROUP = 16
N_SSM_GROUPS = SSM_WIDTH // SSM_GROUP
SSM_STATE = 64
SSM_CHUNK = 16
SSM_TILE_GROUPS = 128 // SSM_GROUP
SSM_TILES = N_SSM_GROUPS // SSM_TILE_GROUPS

N_EXPERTS = 64
TOP_K = 8
N_EXPERT_GROUPS = 8
EXPERTS_PER_GROUP = N_EXPERTS // N_EXPERT_GROUPS
TOPK_GROUPS = 4
EXPERT_DIM = 512
SHARED_DIM = 512
ROUTED_SCALE = 2.5
ROW_BLOCK = 256

IN_WIDTH = ATTN_WIDTH + 2 * KV_WIDTH + SSM_WIDTH + 2 * D_MODEL
COL_KV = ATTN_WIDTH
COL_U = ATTN_WIDTH + 2 * KV_WIDTH
COL_GA = COL_U + SSM_WIDTH
COL_GS = COL_GA + D_MODEL

DEEPNORM_ALPHA = 2.0 ** 0.25
LN_EPS = 1e-5
NEG_INF = -1e30

MIB = 1024 * 1024


def _params(semantics, vmem_mib):
    return pltpu.CompilerParams(dimension_semantics=semantics, vmem_limit_bytes=vmem_mib * MIB)


def _tile(n, pref):
    t = min(n, pref)
    while n % t:
        t -= 128
    return t


def _ln_rows(x, g, b):
    mu = jnp.mean(x, axis=-1, keepdims=True)
    xc = x - mu
    var = jnp.mean(xc * xc, axis=-1, keepdims=True)
    return xc * lax.rsqrt(var + LN_EPS) * g + b


LANES = 128
SLAB = D_MODEL // LANES
SLAB_PITCH = SLAB + 4


def _store_slabs(ref, y):
    m = y.shape[0]
    for s in range(SLAB):
        ref[pl.ds(s, m, stride=SLAB_PITCH), :] = y[:, s * LANES:(s + 1) * LANES]
    for s in range(SLAB, SLAB_PITCH):
        ref[pl.ds(s, m, stride=SLAB_PITCH), :] = jnp.zeros((m, LANES), F32)


def _ln_kernel(x_ref, g_ref, b_ref, o_ref, ob_ref, *maybe_slab_ref):
    y = _ln_rows(x_ref[...], g_ref[...], b_ref[...])
    o_ref[...] = y
    ob_ref[...] = y.astype(BF16)
    for os_ref in maybe_slab_ref:
        _store_slabs(os_ref, y)


def _layer_norm(x, g, b, slabs=False):
    m, d = x.shape
    tm = _tile(m, 256)
    out_specs = [pl.BlockSpec((tm, d), lambda i: (i, 0)), pl.BlockSpec((tm, d), lambda i: (i, 0))]
    out_shape = [jax.ShapeDtypeStruct((m, d), F32), jax.ShapeDtypeStruct((m, d), BF16)]
    if slabs:
        out_specs.append(pl.BlockSpec((tm * SLAB_PITCH, LANES), lambda i: (i, 0)))
        out_shape.append(jax.ShapeDtypeStruct((m * SLAB_PITCH, LANES), F32))
    return pl.pallas_call(
        _ln_kernel,
        grid=(m // tm,),
        in_specs=[pl.BlockSpec((tm, d), lambda i: (i, 0)),
                  pl.BlockSpec((1, d), lambda i: (0, 0)),
                  pl.BlockSpec((1, d), lambda i: (0, 0))],
        out_specs=out_specs,
        out_shape=out_shape,
        compiler_params=_params(("parallel",), 40),
        name="layer_norm",
    )(x, g.reshape(1, d), b.reshape(1, d))


def _proj_kernel(a_ref, w_ref, o_ref):
    o_ref[...] = jnp.dot(a_ref[...], w_ref[...].astype(BF16),
                         preferred_element_type=F32).astype(o_ref.dtype)


def _project(a, w, col0, n_cols, tm_pref=1024, tn=512):
    m, k = a.shape
    tm = _tile(m, tm_pref)
    off = col0 // tn
    return pl.pallas_call(
        _proj_kernel,
        grid=(m // tm, n_cols // tn),
        in_specs=[pl.BlockSpec((tm, k), lambda i, j: (i, 0)),
                  pl.BlockSpec((k, tn), lambda i, j: (0, j + off))],
        out_specs=pl.BlockSpec((tm, tn), lambda i, j: (i, j)),
        out_shape=jax.ShapeDtypeStruct((m, n_cols), BF16),
        compiler_params=_params(("parallel", "arbitrary"), 56),
        name="in_proj",
    )(a, w)


def _attn_kernel(sink_ref, q_ref, kvc_ref, kvp_ref, kvm_ref, o_ref):
    first = pl.program_id(0) == 0
    kv_all = jnp.concatenate([kvp_ref[...], kvc_ref[...], kvm_ref[N_PAD:BLOCK, :]], axis=0)
    n_keys = 2 * BLOCK + N_META
    rows = HEADS_PER_KV * BLOCK

    lane = lax.broadcasted_iota(jnp.int32, (BLOCK, BLOCK), 1)
    lo_q = lane < HEAD_DIM
    lane_kv = lax.broadcasted_iota(jnp.int32, (n_keys, BLOCK), 1)
    lo_kv = lane_kv < HEAD_DIM

    qi = lax.broadcasted_iota(jnp.int32, (rows, BLOCK), 0) & (BLOCK - 1)
    kj = lax.broadcasted_iota(jnp.int32, (rows, BLOCK), 1)
    from_prev = kj > qi
    hidden = from_prev & first
    local_head = lax.broadcasted_iota(jnp.int32, (rows, 1), 0) // BLOCK

    def dup_head(tile, odd):
        t = tile.astype(F32)
        r = pltpu.roll(t, HEAD_DIM, 1)
        keep = jnp.logical_not(lo_kv) if odd else lo_kv
        return jnp.where(keep, t, r).astype(BF16)

    for kh in range(N_KV_HEADS):
        t0 = (kh // 2) * BLOCK
        k2 = dup_head(kv_all[:, t0:t0 + BLOCK], kh % 2)
        v2 = dup_head(kv_all[:, KV_WIDTH + t0:KV_WIDTH + t0 + BLOCK], kh % 2)
        pieces = []
        for p in range(HEADS_PER_KV // 2):
            c0 = kh * HEADS_PER_KV * HEAD_DIM + p * BLOCK
            qp = q_ref[:, c0:c0 + BLOCK] * jnp.asarray(HEAD_DIM ** -0.5, BF16)
            zero = jnp.zeros_like(qp)
            pieces.append(jnp.where(lo_q, qp, zero))
            pieces.append(jnp.where(lo_q, zero, qp))
        qs = jnp.concatenate(pieces, axis=0)
        s = lax.dot_general(qs, k2, (((1,), (1,)), ((), ())), preferred_element_type=F32)
        band = jnp.where(from_prev, s[:, :BLOCK], s[:, BLOCK:2 * BLOCK])
        band = jnp.where(hidden, NEG_INF, band)
        meta = s[:, 2 * BLOCK:]
        sink = jnp.zeros((rows, 1), F32)
        for t in range(HEADS_PER_KV):
            sink = jnp.where(local_head == t, sink_ref[kh * HEADS_PER_KV + t], sink)
        m = jnp.maximum(jnp.maximum(jnp.max(band, axis=-1, keepdims=True),
                                    jnp.max(meta, axis=-1, keepdims=True)), sink)
        e_band = jnp.exp(band - m)
        e_meta = jnp.exp(meta - m)
        den = (jnp.sum(e_band, axis=-1, keepdims=True) + jnp.sum(e_meta, axis=-1, keepdims=True)
               + jnp.exp(sink - m))
        e = jnp.concatenate([jnp.where(from_prev, e_band, 0.0), jnp.where(from_prev, 0.0, e_band), e_meta],
                            axis=1).astype(BF16)
        o2 = jnp.dot(e, v2, preferred_element_type=F32) / den
        for p in range(HEADS_PER_KV // 2):
            c0 = kh * HEADS_PER_KV * HEAD_DIM + p * BLOCK
            r0 = 2 * p * BLOCK
            o_ref[:, c0:c0 + BLOCK] = jnp.where(
                lo_q, o2[r0:r0 + BLOCK], o2[r0 + BLOCK:r0 + 2 * BLOCK]).astype(BF16)


def _attention(z, z_lead, sinks):
    t = z.shape[0]
    kv_blk = COL_KV // (2 * KV_WIDTH)
    return pl.pallas_call(
        _attn_kernel,
        grid=(t // BLOCK,),
        in_specs=[pl.BlockSpec(memory_space=pltpu.SMEM),
                  pl.BlockSpec((BLOCK, ATTN_WIDTH), lambda n: (n, 0)),
                  pl.BlockSpec((BLOCK, 2 * KV_WIDTH), lambda n: (n, kv_blk)),
                  pl.BlockSpec((BLOCK, 2 * KV_WIDTH), lambda n: (jnp.maximum(n - 1, 0), kv_blk)),
                  pl.BlockSpec((BLOCK, 2 * KV_WIDTH), lambda n: (0, 0))],
        out_specs=pl.BlockSpec((BLOCK, ATTN_WIDTH), lambda n: (n, 0)),
        out_shape=jax.ShapeDtypeStruct((t, ATTN_WIDTH), BF16),
        compiler_params=_params(("parallel",), 40),
        name="swa_attention",
    )(sinks.astype(F32), z, z, z, z_lead)


def _cmul(ar, ai, br, bi):
    return ar * br - ai * bi, ar * bi + ai * br


def _ssm_tables(lam_re, lam_im, log_dt, b_re, b_im, c_re, c_im, d_skip, n_chunks):
    hi = lax.Precision.HIGHEST
    g, p, h, c = N_SSM_GROUPS, SSM_STATE, SSM_GROUP, SSM_CHUNK
    dt = jnp.exp(log_dt)[:, None]
    mag = jnp.exp(dt * lam_re)
    are = mag * jnp.cos(dt * lam_im)
    aim = mag * jnp.sin(dt * lam_im)
    den = lam_re * lam_re + lam_im * lam_im
    num_re = are - 1.0
    coef_re = (num_re * lam_re + aim * lam_im) / den
    coef_im = (aim * lam_re - num_re * lam_im) / den
    bb_re = coef_re[..., None] * b_re - coef_im[..., None] * b_im
    bb_im = coef_re[..., None] * b_im + coef_im[..., None] * b_re

    pr, pi = [jnp.ones_like(are)], [jnp.zeros_like(are)]
    for _ in range(c):
        r, i = _cmul(pr[-1], pi[-1], are, aim)
        pr.append(r)
        pi.append(i)
    pr, pi = jnp.stack(pr), jnp.stack(pi)

    m_re = pr[:c, :, :, None] * bb_re[None] - pi[:c, :, :, None] * bb_im[None]
    m_im = pr[:c, :, :, None] * bb_im[None] + pi[:c, :, :, None] * bb_re[None]
    nt, gt = SSM_TILES, SSM_TILE_GROUPS
    lag = (jnp.einsum('ghp,kgpj->kgjh', c_re, m_re, precision=hi)
           - jnp.einsum('ghp,kgpj->kgjh', c_im, m_im, precision=hi))
    lagc = lag.reshape(c, nt, gt * h, h).transpose(1, 0, 2, 3)
    rev = np.arange(c - 1, -1, -1)
    bt_re, bt_im = bb_re.transpose(0, 2, 1), bb_im.transpose(0, 2, 1)
    wo_re = pr[rev][:, :, None, :] * bt_re[None] - pi[rev][:, :, None, :] * bt_im[None]
    wo_im = pr[rev][:, :, None, :] * bt_im[None] + pi[rev][:, :, None, :] * bt_re[None]

    def lanes(first, second):
        w = jnp.concatenate([first, second], axis=-1)
        return w.reshape(c, nt, gt * h, 2 * p).transpose(1, 0, 2, 3)

    wout_a, wout_b = lanes(wo_re, wo_im), lanes(wo_im, wo_re)
    ct_re, ct_im = c_re.transpose(0, 2, 1), c_im.transpose(0, 2, 1)
    cin_re = ct_re[None] * pr[1:, :, :, None] - ct_im[None] * pi[1:, :, :, None]
    cin_im = -(ct_re[None] * pi[1:, :, :, None] + ct_im[None] * pr[1:, :, :, None])
    cinc = jnp.stack([cin_re.reshape(c, nt, gt * p, h), cin_im.reshape(c, nt, gt * p, h)], axis=2)
    cinc = cinc.transpose(1, 0, 2, 3, 4).reshape(nt, c, 2 * gt * p, h)
    spread = jnp.tile(jnp.eye(h, dtype=BF16), (1, gt))

    n_steps = max(1, math.ceil(math.log2(n_chunks)))
    qr, qi = [pr[c]], [pi[c]]
    for _ in range(n_steps - 1):
        r, i = _cmul(qr[-1], qi[-1], qr[-1], qi[-1])
        qr.append(r)
        qi.append(i)
    apr = jnp.stack(qr, axis=0).reshape(n_steps, nt, gt * p).transpose(1, 0, 2)
    api = jnp.stack(qi, axis=0).reshape(n_steps, nt, gt * p).transpose(1, 0, 2)
    dsk = d_skip.reshape(nt, 1, gt * h)
    return (lagc.astype(BF16), wout_a.astype(BF16), wout_b.astype(BF16), cinc.astype(BF16), spread,
            apr, api, dsk, n_steps)


def _chunk_steps(u_ref):
    n = u_ref.shape[0] // SSM_CHUNK
    return [u_ref[pl.ds(s, n, stride=SSM_CHUNK), :] for s in range(SSM_CHUNK)]


def _chunk_lanes(u_ref):
    return jnp.concatenate(_chunk_steps(u_ref), axis=1)


def _ssm_state_kernel(u_ref, lead_ref, wa_ref, wb_ref, apr_ref, api_ref, prev_ref, wout_s, *, n_steps):
    p, gh = SSM_STATE, SSM_TILE_GROUPS * SSM_GROUP
    row_g = lax.broadcasted_iota(jnp.int32, (gh, 2 * p), 0) // SSM_GROUP
    lo = lax.broadcasted_iota(jnp.int32, (gh, 2 * p), 1) < p
    zero = jnp.zeros((gh, 2 * p), BF16)
    for s in range(SSM_CHUNK):
        a, b = wa_ref[0, s], wb_ref[0, s]
        for j in range(SSM_TILE_GROUPS // 2):
            even, odd = (row_g == 2 * j) & lo, (row_g == 2 * j + 1) & jnp.logical_not(lo)
            wout_s[s * gh:(s + 1) * gh, j * 2 * p:(j + 1) * 2 * p] = (
                jnp.where(even, a, jnp.where(odd, b, zero)))
            wout_s[s * gh:(s + 1) * gh, (SSM_TILE_GROUPS // 2 + j) * 2 * p:(SSM_TILE_GROUPS // 2 + j + 1) * 2 * p] = (
                jnp.where(even, b, jnp.where(odd, a, zero)))
    wout = wout_s[...]
    st = jnp.dot(_chunk_lanes(u_ref).astype(BF16), wout, preferred_element_type=F32)
    lead = jnp.dot(_chunk_lanes(lead_ref).astype(BF16), wout, preferred_element_type=F32)
    nc, half = st.shape[0], st.shape[1] // 2
    xr, xi = st[:, :half], st[:, half:]
    mr, mi = lead[-1:, :half], lead[-1:, half:]
    row = lax.broadcasted_iota(jnp.int32, (nc, half), 0)
    ar, ai = apr_ref[0, 0:1, :], api_ref[0, 0:1, :]
    xr = xr + jnp.where(row == 0, ar * mr - ai * mi, 0.0)
    xi = xi + jnp.where(row == 0, ar * mi + ai * mr, 0.0)

    def shift(x, d):
        return jnp.where(row >= d, pltpu.roll(x, d, 0), 0.0)

    for j in range(n_steps):
        ar, ai = apr_ref[0, j:j + 1, :], api_ref[0, j:j + 1, :]
        sr, si = shift(xr, 1 << j), shift(xi, 1 << j)
        xr, xi = xr + (ar * sr - ai * si), xi + (ar * si + ai * sr)
    pr = jnp.where(row == 0, mr, shift(xr, 1))
    pi = jnp.where(row == 0, mi, shift(xi, 1))
    prev_ref[0] = jnp.concatenate([pr, pi], axis=1).astype(BF16)


def _ssm_out_kernel(u_ref, prev_ref, lag_ref, cin_ref, spread_ref, d_ref, y_ref, toep_s, wcin_s):
    gh, c = SSM_TILE_GROUPS * SSM_GROUP, SSM_CHUNK

    @pl.when(pl.program_id(1) == 0)
    def _():
        spread = spread_ref[...]
        diag = (lax.broadcasted_iota(jnp.int32, (gh, gh), 0) // SSM_GROUP
                == lax.broadcasted_iota(jnp.int32, (gh, gh), 1) // SSM_GROUP)
        zero = jnp.zeros((gh, gh), BF16)
        blocks = [jnp.where(diag, jnp.dot(lag_ref[0, k], spread, preferred_element_type=F32), 0.0).astype(BF16)
                  for k in range(c)]
        for s in range(c):
            for i in range(c):
                toep_s[s * gh:(s + 1) * gh, i * gh:(i + 1) * gh] = blocks[i - s] if i >= s else zero
        rows = wcin_s.shape[0]
        row_g = (lax.broadcasted_iota(jnp.int32, (rows, gh), 0) % (rows // 2)) // SSM_STATE
        own = row_g == lax.broadcasted_iota(jnp.int32, (rows, gh), 1) // SSM_GROUP
        for i in range(c):
            wcin_s[:, i * gh:(i + 1) * gh] = jnp.where(
                own, jnp.dot(cin_ref[0, i], spread, preferred_element_type=F32), 0.0).astype(BF16)

    us = _chunk_steps(u_ref)
    cb = us[0].shape[0]
    u = jnp.concatenate(us, axis=1).astype(BF16)
    y = (jnp.dot(u, toep_s[...], preferred_element_type=F32)
         + jnp.dot(prev_ref[0], wcin_s[...], preferred_element_type=F32))
    for s in range(SSM_CHUNK):
        ys = y[:, s * 128:(s + 1) * 128] + d_ref[0] * us[s]
        y_ref[pl.ds(s, cb, stride=SSM_CHUNK), :] = 0.5 * ys * (1.0 + lax.erf(ys * (0.5 ** 0.5)))


def _ssm(u, u_lead, tables):
    lagc, wout_a, wout_b, cinc, spread, apr, api, dsk, n_steps = tables
    t = u.shape[0]
    nc = t // SSM_CHUNK
    cb = _tile(nc, 256) if nc % 128 == 0 else nc
    lanes = 128
    k = SSM_CHUNK * lanes
    states = 2 * SSM_TILE_GROUPS * SSM_STATE
    prev = pl.pallas_call(
        functools.partial(_ssm_state_kernel, n_steps=n_steps),
        grid=(SSM_TILES,),
        in_specs=[pl.BlockSpec((t, lanes), lambda g: (0, g)),
                  pl.BlockSpec((BLOCK, lanes), lambda g: (0, g)),
                  pl.BlockSpec((1, SSM_CHUNK, lanes, 2 * SSM_STATE), lambda g: (g, 0, 0, 0)),
                  pl.BlockSpec((1, SSM_CHUNK, lanes, 2 * SSM_STATE), lambda g: (g, 0, 0, 0)),
                  pl.BlockSpec((1, n_steps, states // 2), lambda g: (g, 0, 0)),
                  pl.BlockSpec((1, n_steps, states // 2), lambda g: (g, 0, 0))],
        out_specs=pl.BlockSpec((1, nc, states), lambda g: (g, 0, 0)),
        out_shape=jax.ShapeDtypeStruct((SSM_TILES, nc, states), BF16),
        scratch_shapes=[pltpu.VMEM((k, states), BF16)],
        compiler_params=_params(("parallel",), 48),
        name="s5_states",
    )(u, u_lead, wout_a, wout_b, apr, api)
    return pl.pallas_call(
        _ssm_out_kernel,
        grid=(SSM_TILES, nc // cb),
        in_specs=[pl.BlockSpec((cb * SSM_CHUNK, lanes), lambda g, c: (c, g)),
                  pl.BlockSpec((1, cb, states), lambda g, c: (g, c, 0)),
                  pl.BlockSpec((1, SSM_CHUNK, lanes, SSM_GROUP), lambda g, c: (g, 0, 0, 0)),
                  pl.BlockSpec((1, SSM_CHUNK, states, SSM_GROUP), lambda g, c: (g, 0, 0, 0)),
                  pl.BlockSpec((SSM_GROUP, lanes), lambda g, c: (0, 0)),
                  pl.BlockSpec((1, 1, lanes), lambda g, c: (g, 0, 0))],
        out_specs=pl.BlockSpec((cb * SSM_CHUNK, lanes), lambda g, c: (c, g)),
        out_shape=jax.ShapeDtypeStruct((t, SSM_WIDTH), F32),
        scratch_shapes=[pltpu.VMEM((k, k), BF16), pltpu.VMEM((states, k), BF16)],
        compiler_params=_params(("parallel", "arbitrary"), 48),
        name="s5_outputs",
    )(u, prev, lagc, cinc, spread, dsk)


def _glu_kernel(y_ref, w_ref, o_ref):
    y = y_ref[...]
    gate = jax.nn.sigmoid(jnp.dot(y.astype(BF16), w_ref[...].astype(BF16), preferred_element_type=F32))
    o_ref[...] = (y * gate).astype(BF16)


def _glu(y, w):
    m, d = y.shape
    tm = _tile(m, 1024)
    return pl.pallas_call(
        _glu_kernel,
        grid=(m // tm,),
        in_specs=[pl.BlockSpec((tm, d), lambda i: (i, 0)),
                  pl.BlockSpec((d, d), lambda i: (0, 0))],
        out_specs=pl.BlockSpec((tm, d), lambda i: (i, 0)),
        out_shape=jax.ShapeDtypeStruct((m, d), BF16),
        compiler_params=_params(("parallel",), 40),
        name="ssm_glu",
    )(y, w)


def _mix_kernel(a_ref, y_ref, wa_ref, ws_ref, ga_ref, gs_ref, o_ref):
    attn_d = jnp.dot(a_ref[...], wa_ref[...].astype(BF16), preferred_element_type=F32)
    ssm_d = jnp.dot(y_ref[...], ws_ref[...].astype(BF16), preferred_element_type=F32)
    mix = (jax.nn.sigmoid(ga_ref[...].astype(F32)) * attn_d
           + jax.nn.sigmoid(gs_ref[...].astype(F32)) * ssm_d)
    o_ref[...] = mix.astype(BF16)


def _mix(attn, y2, w_attn_o, w_ssm_o, z):
    m = attn.shape[0]
    tm, tn = _tile(m, 1024), 512
    ga_blk, gs_blk = COL_GA // tn, COL_GS // tn
    return pl.pallas_call(
        _mix_kernel,
        grid=(m // tm, D_MODEL // tn),
        in_specs=[pl.BlockSpec((tm, ATTN_WIDTH), lambda i, j: (i, 0)),
                  pl.BlockSpec((tm, SSM_WIDTH), lambda i, j: (i, 0)),
                  pl.BlockSpec((ATTN_WIDTH, tn), lambda i, j: (0, j)),
                  pl.BlockSpec((SSM_WIDTH, tn), lambda i, j: (0, j)),
                  pl.BlockSpec((tm, tn), lambda i, j: (i, j + ga_blk)),
                  pl.BlockSpec((tm, tn), lambda i, j: (i, j + gs_blk))],
        out_specs=pl.BlockSpec((tm, tn), lambda i, j: (i, j)),
        out_shape=jax.ShapeDtypeStruct((m, D_MODEL), BF16),
        compiler_params=_params(("parallel", "arbitrary"), 48),
        name="gated_merge",
    )(attn, y2, w_attn_o, w_ssm_o, z, z)


def _resid_kernel(a_ref, w_ref, h_ref, o_ref):
    o_ref[...] = DEEPNORM_ALPHA * h_ref[...] + jnp.dot(
        a_ref[...], w_ref[...].astype(BF16), preferred_element_type=F32)


def _out_proj(mix, w_out, h0):
    m = mix.shape[0]
    tm, tn = _tile(m, 1024), 512
    return pl.pallas_call(
        _resid_kernel,
        grid=(m // tm, D_MODEL // tn),
        in_specs=[pl.BlockSpec((tm, D_MODEL), lambda i, j: (i, 0)),
                  pl.BlockSpec((D_MODEL, tn), lambda i, j: (0, j)),
                  pl.BlockSpec((tm, tn), lambda i, j: (i, j))],
        out_specs=pl.BlockSpec((tm, tn), lambda i, j: (i, j)),
        out_shape=jax.ShapeDtypeStruct((m, D_MODEL), F32),
        compiler_params=_params(("parallel", "arbitrary"), 56),
        name="out_proj",
    )(mix, w_out, h0)


def _token_mixer(x, meta_tokens, ln0_g, ln0_b, w_in, attn_sinks, w_attn_o, ssm_tables_args,
                 w_glu, w_ssm_o, w_out, ln1_g, ln1_b):
    t = x.shape[0]
    lead = jnp.concatenate([jnp.zeros((N_PAD, D_MODEL), F32), meta_tokens.astype(F32)], axis=0)
    h0, h0b = _layer_norm(x, ln0_g, ln0_b)
    _, leadb = _layer_norm(lead, ln0_g, ln0_b)

    z = _project(h0b, w_in, 0, IN_WIDTH)
    z_lead = _project(leadb, w_in, COL_KV, 2 * KV_WIDTH + SSM_WIDTH)

    attn = _attention(z, z_lead, attn_sinks)

    lead_valid = (jnp.arange(BLOCK) >= N_PAD)[:, None]
    u_lead = jnp.where(lead_valid, z_lead[:, 2 * KV_WIDTH:].astype(F32), 0.0)
    tables = _ssm_tables(*ssm_tables_args, n_chunks=t // SSM_CHUNK)
    y = _ssm(z[:, COL_U:COL_GA].astype(F32), u_lead, tables)
    y2 = _glu(y, w_glu)

    mix = _mix(attn, y2, w_attn_o, w_ssm_o, z)
    r = _out_proj(mix, w_out, h0)
    return _layer_norm(r, ln1_g, ln1_b, slabs=True)


def _router_kernel(h_ref, wr_ref, bias_ref, idx_ref, wt_ref, pos_ref, cnt_ref):
    tm = h_ref.shape[0]

    @pl.when(pl.program_id(0) == 0)
    def _():
        cnt_ref[...] = jnp.zeros_like(cnt_ref)

    ng, ne = N_EXPERT_GROUPS, EXPERTS_PER_GROUP
    logits = lax.dot_general(wr_ref[...], h_ref[...], (((1,), (1,)), ((), ())),
                             precision=lax.Precision.HIGHEST, preferred_element_type=F32)
    scores = jax.nn.sigmoid(logits)
    sel = (scores + bias_ref[...]).reshape(ng, ne, tm)
    scores = scores.reshape(ng, ne, tm)
    e_in = lax.broadcasted_iota(jnp.int32, (ng, ne, tm), 1)
    e_id = lax.broadcasted_iota(jnp.int32, (ng, ne, tm), 0) * ne + e_in
    neg = -jnp.inf

    top1 = jnp.max(sel, axis=1, keepdims=True)
    first = jnp.min(jnp.where(sel == top1, e_in, ne), axis=1, keepdims=True)
    top2 = jnp.max(jnp.where(e_in == first, neg, sel), axis=1, keepdims=True)
    grp = (top1 + top2)[:, 0, :]

    g_id = lax.broadcasted_iota(jnp.int32, (ng, tm), 0)
    chosen = jnp.zeros((ng, tm), jnp.bool_)
    for _ in range(TOPK_GROUPS):
        best = jnp.max(grp, axis=0, keepdims=True)
        pick = g_id == jnp.min(jnp.where(grp == best, g_id, ng), axis=0, keepdims=True)
        chosen = chosen | pick
        grp = jnp.where(pick, neg, grp)

    cur = jnp.where(chosen[:, None, :], sel, NEG_INF)
    def pick_sum(pick, values):
        part = jnp.sum(jnp.where(pick, values, 0.0), axis=1, keepdims=True)
        return jnp.sum(part, axis=0, keepdims=True)[0]

    ids, wts, picks = [], [], []
    for _ in range(TOP_K):
        best = jnp.max(jnp.max(cur, axis=1, keepdims=True), axis=0, keepdims=True)
        cand = jnp.where(cur == best, e_id, N_EXPERTS)
        win = jnp.min(jnp.min(cand, axis=1, keepdims=True), axis=0, keepdims=True)
        pick = e_id == win
        ids.append(win[0])
        wts.append(pick_sum(pick, scores))
        picks.append(pick)
        cur = jnp.where(pick, neg, cur)
    w = jnp.concatenate(wts, axis=0)
    idx_ref[...] = jnp.concatenate(ids, axis=0)
    wt_ref[...] = w / jnp.sum(w, axis=0, keepdims=True) * ROUTED_SCALE

    taken = jnp.where(cur == neg, 1.0, 0.0).reshape(N_EXPERTS, tm)
    before = (lax.broadcasted_iota(jnp.int32, (tm, tm), 0)
              < lax.broadcasted_iota(jnp.int32, (tm, tm), 1))
    rank = jnp.dot(taken.astype(BF16), jnp.where(before, 1.0, 0.0).astype(BF16),
                   preferred_element_type=F32)
    pos = (cnt_ref[:, 0:1] + rank).reshape(ng, ne, tm)
    pos_ref[...] = jnp.concatenate([pick_sum(p, pos) for p in picks], axis=0).astype(jnp.int32)
    cnt_ref[...] = cnt_ref[...] + jnp.sum(taken, axis=1, keepdims=True)


def _router(h1, w_router, router_bias):
    t, d = h1.shape
    tm = _tile(t, 512)
    return pl.pallas_call(
        _router_kernel,
        grid=(t // tm,),
        in_specs=[pl.BlockSpec((tm, d), lambda i: (i, 0)),
                  pl.BlockSpec((N_EXPERTS, d), lambda i: (0, 0)),
                  pl.BlockSpec((N_EXPERTS, 1), lambda i: (0, 0))],
        out_specs=[pl.BlockSpec((TOP_K, tm), lambda i: (0, i)),
                   pl.BlockSpec((TOP_K, tm), lambda i: (0, i)),
                   pl.BlockSpec((TOP_K, tm), lambda i: (0, i)),
                   pl.BlockSpec((N_EXPERTS, 128), lambda i: (0, 0))],
        out_shape=[jax.ShapeDtypeStruct((TOP_K, t), jnp.int32),
                   jax.ShapeDtypeStruct((TOP_K, t), F32),
                   jax.ShapeDtypeStruct((TOP_K, t), jnp.int32),
                   jax.ShapeDtypeStruct((N_EXPERTS, 128), F32)],
        compiler_params=_params(("arbitrary",), 40),
        name="router",
    )(h1, w_router.T, router_bias.reshape(N_EXPERTS, 1))


def _dispatch(idx_t, pos_t, counts):
    t = idx_t.shape[1]
    n_assign = TOP_K * t
    n_blocks = -(-n_assign // ROW_BLOCK) + N_EXPERTS
    n_rows = n_blocks * ROW_BLOCK
    counts = counts[:, 0].astype(jnp.int32)
    padded = (counts + ROW_BLOCK - 1) // ROW_BLOCK * ROW_BLOCK
    ends = jnp.cumsum(padded)
    starts = ends - padded
    own = idx_t[:, :, None] == jnp.arange(N_EXPERTS, dtype=jnp.int32)[None, None, :]
    dest = (jnp.sum(jnp.where(own, starts[None, None, :], 0), axis=-1) + pos_t).reshape(-1)
    tok_flat = jnp.tile(jnp.arange(t, dtype=jnp.int32), TOP_K)
    row_tok = jnp.zeros((n_rows,), jnp.int32).at[dest].set(tok_flat)
    block_start = jnp.arange(n_blocks, dtype=jnp.int32) * ROW_BLOCK
    block_expert = jnp.minimum(
        jnp.sum((ends[None, :] <= block_start[:, None]).astype(jnp.int32), axis=1), N_EXPERTS - 1)
    n_used = (ends[-1] // ROW_BLOCK).astype(jnp.int32).reshape(1)
    has = counts > 0
    cand = jnp.where(has[None, :] & (jnp.arange(N_EXPERTS)[None, :] > jnp.arange(N_EXPERTS)[:, None]),
                     jnp.arange(N_EXPERTS)[None, :], N_EXPERTS)
    nxt = jnp.min(cand, axis=1)
    next_expert = jnp.where(nxt < N_EXPERTS, nxt, -1).astype(jnp.int32)
    return (row_tok.reshape(n_blocks, 1, ROW_BLOCK), block_expert, n_used, next_expert,
            dest.reshape(TOP_K, t))


WEIGHT_DMA_PRIORITY = 1
GATHER_SLOTS = 3


def _gather_rows(src_hbm, idx_smem, dst, sem, n, unrolled=False):
    def body(i, carry):
        pltpu.make_async_copy(src_hbm.at[pl.ds(idx_smem[0, i], 1)], dst.at[pl.ds(i, 1)], sem).start()
        return carry
    if unrolled:
        for i in range(n):
            body(i, 0)
    else:
        lax.fori_loop(0, n, body, 0, unroll=8)


def _wait_rows(src_hbm, dst, sem, n):
    pltpu.make_async_copy(src_hbm.at[pl.ds(0, n)], dst, sem).wait()


def _gather_slabs(src_hbm, idx_smem, dst, sem, n, unrolled=False, dst_pitch=None):
    dst_pitch = SLAB_PITCH if dst_pitch is None else dst_pitch

    def body(i, carry):
        src = idx_smem[0, i] * SLAB_PITCH
        pltpu.make_async_copy(src_hbm.at[pl.ds(src, SLAB)], dst.at[pl.ds(i * dst_pitch, SLAB)], sem).start()
        return carry
    if unrolled:
        for i in range(n):
            body(i, 0)
    else:
        lax.fori_loop(0, n, body, 0, unroll=8)


def _wait_slabs(src_hbm, dst, sem, n):
    pltpu.make_async_copy(src_hbm.at[pl.ds(0, n * SLAB)], dst.at[pl.ds(0, n * SLAB)], sem).wait()


def _slab_rows(ref, n, pitch=None):
    pitch = SLAB_PITCH if pitch is None else pitch
    return jnp.concatenate([ref[pl.ds(s, n, stride=pitch), :] for s in range(SLAB)], axis=1)


def _expert_up_kernel(be_ref, nu_ref, nx_ref, tok_hbm, h_hbm, w1_hbm, w3_hbm, o_ref,
                      idx_smem, xbuf, wstage, wcast, isem, xsem, wsem):
    b = pl.program_id(0)
    n_used = nu_ref[0]
    last = n_used - 1
    e = be_ref[b]
    ns = GATHER_SLOTS
    slot = b % ns
    ahead = (b + ns - 1) % ns

    def idx_copy(blk, s):
        return pltpu.make_async_copy(tok_hbm.at[jnp.minimum(blk, last)], idx_smem.at[s], isem.at[s])

    def weight_copies(ex):
        return (pltpu.make_async_copy(w1_hbm.at[ex], wstage.at[0], wsem.at[0]),
                pltpu.make_async_copy(w3_hbm.at[ex], wstage.at[1], wsem.at[1]))

    @pl.when(b == 0)
    def _():
        for c in weight_copies(e):
            c.start(priority=WEIGHT_DMA_PRIORITY)
        for k in range(ns - 1):
            idx_copy(k, k).start()
            idx_copy(k, k).wait()
            _gather_slabs(h_hbm, idx_smem.at[k], xbuf.at[k], xsem.at[k], ROW_BLOCK)
        idx_copy(ns - 1, ns - 1).start()

    @pl.when(b < n_used)
    def _():
        first_of_expert = jnp.logical_or(b == 0, be_ref[jnp.maximum(b - 1, 0)] != e)

        @pl.when(first_of_expert)
        def _():
            for c in weight_copies(e):
                c.wait()
            rows = wstage.shape[1] // 8
            for m in range(2):
                for r in range(8):
                    wcast[m, r * rows:(r + 1) * rows] = wstage[m, r * rows:(r + 1) * rows].astype(BF16)
            nxt = nx_ref[e]

            @pl.when(nxt >= 0)
            def _():
                for c in weight_copies(nxt):
                    c.start(priority=WEIGHT_DMA_PRIORITY)

        idx_copy(b + ns - 1, ahead).wait()
        _wait_slabs(h_hbm, xbuf.at[slot], xsem.at[slot], ROW_BLOCK)
        _gather_slabs(h_hbm, idx_smem.at[ahead], xbuf.at[ahead], xsem.at[ahead], ROW_BLOCK, unrolled=True)
        idx_copy(b + ns, slot).start()
        x = _slab_rows(xbuf.at[slot], ROW_BLOCK).astype(BF16)
        a = jnp.dot(x, wcast[0], preferred_element_type=F32)
        g = jnp.dot(x, wcast[1], preferred_element_type=F32)
        o_ref[...] = (a * jax.nn.sigmoid(a) * g).astype(BF16)

        @pl.when(b == last)
        def _():
            for k in range(1, ns):
                s = (b + k) % ns
                _wait_slabs(h_hbm, xbuf.at[s], xsem.at[s], ROW_BLOCK)
            idx_copy(last, slot).wait()

    @pl.when(b >= n_used)
    def _():
        o_ref[...] = jnp.zeros_like(o_ref)


def _expert_up(h1, row_tok, block_expert, n_used, next_expert, w1, w3):
    n_blocks = row_tok.shape[0]
    d = w1.shape[1]
    grid_spec = pltpu.PrefetchScalarGridSpec(
        num_scalar_prefetch=3,
        grid=(n_blocks,),
        in_specs=[pl.BlockSpec(memory_space=pl.ANY)] * 4,
        out_specs=pl.BlockSpec((ROW_BLOCK, EXPERT_DIM), lambda b, *_: (b, 0)),
        scratch_shapes=[pltpu.SMEM((GATHER_SLOTS, 1, ROW_BLOCK), jnp.int32),
                        pltpu.VMEM((GATHER_SLOTS, ROW_BLOCK * SLAB_PITCH, LANES), F32),
                        pltpu.VMEM((2, d, EXPERT_DIM), F32),
                        pltpu.VMEM((2, d, EXPERT_DIM), BF16),
                        pltpu.SemaphoreType.DMA((GATHER_SLOTS,)),
                        pltpu.SemaphoreType.DMA((GATHER_SLOTS,)),
                        pltpu.SemaphoreType.DMA((2,))])
    return pl.pallas_call(
        _expert_up_kernel,
        grid_spec=grid_spec,
        out_shape=jax.ShapeDtypeStruct((n_blocks * ROW_BLOCK, EXPERT_DIM), BF16),
        compiler_params=_params(("arbitrary",), 58),
        name="expert_up",
    )(block_expert, n_used, next_expert, row_tok, h1, w1, w3)


def _expert_down_kernel(be_ref, nu_ref, nx_ref, a_ref, w2_hbm, o_ref, wstage, wcast, wsem):
    b = pl.program_id(0)
    n_used = nu_ref[0]
    e = be_ref[b]

    def weight_copy(ex):
        return pltpu.make_async_copy(w2_hbm.at[ex], wstage, wsem)

    @pl.when(b == 0)
    def _():
        weight_copy(e).start(priority=WEIGHT_DMA_PRIORITY)

    @pl.when(b < n_used)
    def _():
        first_of_expert = jnp.logical_or(b == 0, be_ref[jnp.maximum(b - 1, 0)] != e)

        @pl.when(first_of_expert)
        def _():
            weight_copy(e).wait()
            wcast[...] = wstage[...].astype(BF16)
            nxt = nx_ref[e]

            @pl.when(nxt >= 0)
            def _():
                weight_copy(nxt).start(priority=WEIGHT_DMA_PRIORITY)

        _store_slabs(o_ref, jnp.dot(a_ref[...], wcast[...], preferred_element_type=F32))

    @pl.when(b >= n_used)
    def _():
        o_ref[...] = jnp.zeros_like(o_ref)


def _expert_down(act, block_expert, n_used, next_expert, w2):
    n_rows = act.shape[0]
    n_blocks = n_rows // ROW_BLOCK
    d = w2.shape[2]
    grid_spec = pltpu.PrefetchScalarGridSpec(
        num_scalar_prefetch=3,
        grid=(n_blocks,),
        in_specs=[pl.BlockSpec((ROW_BLOCK, EXPERT_DIM), lambda b, *_: (b, 0)),
                  pl.BlockSpec(memory_space=pl.ANY)],
        out_specs=pl.BlockSpec((ROW_BLOCK * SLAB_PITCH, LANES), lambda b, *_: (b, 0)),
        scratch_shapes=[pltpu.VMEM((EXPERT_DIM, d), F32),
                        pltpu.VMEM((EXPERT_DIM, d), BF16),
                        pltpu.SemaphoreType.DMA(())])
    return pl.pallas_call(
        _expert_down_kernel,
        grid_spec=grid_spec,
        out_shape=jax.ShapeDtypeStruct((n_rows * SLAB_PITCH, LANES), F32),
        compiler_params=_params(("arbitrary",), 40),
        name="expert_down",
    )(block_expert, n_used, next_expert, act, w2)


def _shared_kernel(h_ref, w1_ref, w3_ref, w2_ref, o_ref):
    x = h_ref[...]
    a = jnp.dot(x, w1_ref[...], preferred_element_type=F32)
    g = jnp.dot(x, w3_ref[...], preferred_element_type=F32)
    act = (a * jax.nn.sigmoid(a) * g).astype(BF16)
    o_ref[...] = jnp.dot(act, w2_ref[...], preferred_element_type=F32)


def _shared_expert(h1b, w1, w3, w2):
    t, d = h1b.shape
    tm = _tile(t, 512)
    return pl.pallas_call(
        _shared_kernel,
        grid=(t // tm,),
        in_specs=[pl.BlockSpec((tm, d), lambda i: (i, 0)),
                  pl.BlockSpec((d, SHARED_DIM), lambda i: (0, 0)),
                  pl.BlockSpec((d, SHARED_DIM), lambda i: (0, 0)),
                  pl.BlockSpec((SHARED_DIM, d), lambda i: (0, 0))],
        out_specs=pl.BlockSpec((tm, d), lambda i: (i, 0)),
        out_shape=jax.ShapeDtypeStruct((t, d), F32),
        compiler_params=_params(("parallel",), 56),
        name="shared_expert",
    )(h1b, w1.astype(BF16), w3.astype(BF16), w2.astype(BF16))


COMBINE_TOKENS = 64
COMBINE_PITCH = SLAB + 8


def _combine_kernel(dest_hbm, y_hbm, h_ref, s_ref, w_ref, g_ref, b_ref, o_ref,
                    idx_smem, ybuf, stage, isem, ysem):
    i = pl.program_id(0)
    n = pl.num_programs(0)
    slot = i % 2
    rows = TOP_K * COMBINE_TOKENS

    def idx_copy(blk, s):
        return pltpu.make_async_copy(dest_hbm.at[blk], idx_smem.at[s], isem.at[s])

    last = n - 1

    pitch = COMBINE_PITCH
    per_j = COMBINE_TOKENS * pitch

    @pl.when(i == 0)
    def _():
        ybuf[...] = jnp.zeros_like(ybuf)
        idx_copy(0, 0).start()
        idx_copy(0, 0).wait()
        _gather_slabs(y_hbm, idx_smem.at[0], ybuf.at[0], ysem.at[0], rows, dst_pitch=pitch)
        idx_copy(jnp.minimum(1, last), 1).start()

    idx_copy(jnp.minimum(i + 1, last), 1 - slot).wait()
    _wait_slabs(y_hbm, ybuf.at[slot], ysem.at[slot], rows)
    _gather_slabs(y_hbm, idx_smem.at[1 - slot], ybuf.at[1 - slot], ysem.at[1 - slot], rows,
                  unrolled=True, dst_pitch=pitch)
    idx_copy(jnp.minimum(i + 2, last), slot).start()
    acc = None
    for j in range(TOP_K):
        wj = jnp.broadcast_to(w_ref[:, j:j + 1, :], (COMBINE_TOKENS, pitch, LANES)).reshape(per_j, LANES)
        term = wj * ybuf[slot, j * per_j:(j + 1) * per_j, :]
        acc = term if acc is None else acc + term
    stage[...] = acc
    routed = _slab_rows(stage, COMBINE_TOKENS, pitch)
    o_ref[...] = _ln_rows(DEEPNORM_ALPHA * h_ref[...] + s_ref[...] + routed, g_ref[...], b_ref[...])

    @pl.when(i == last)
    def _():
        _wait_slabs(y_hbm, ybuf.at[1 - slot], ysem.at[1 - slot], rows)
        idx_copy(last, slot).wait()


def _combine(yb, dest, w_t, h1, shared, g, b):
    t, d = h1.shape
    tm = COMBINE_TOKENS
    n_tiles = t // tm
    dest_tiles = dest.reshape(TOP_K, n_tiles, tm).transpose(1, 0, 2).reshape(n_tiles, 1, TOP_K * tm)
    w_lanes = jnp.broadcast_to(w_t.T[:, :, None], (t, TOP_K, LANES))
    return pl.pallas_call(
        _combine_kernel,
        grid=(n_tiles,),
        in_specs=[pl.BlockSpec(memory_space=pl.ANY),
                  pl.BlockSpec(memory_space=pl.ANY),
                  pl.BlockSpec((tm, d), lambda i: (i, 0)),
                  pl.BlockSpec((tm, d), lambda i: (i, 0)),
                  pl.BlockSpec((tm, TOP_K, LANES), lambda i: (i, 0, 0)),
                  pl.BlockSpec((1, d), lambda i: (0, 0)),
                  pl.BlockSpec((1, d), lambda i: (0, 0))],
        out_specs=pl.BlockSpec((tm, d), lambda i: (i, 0)),
        out_shape=jax.ShapeDtypeStruct((t, d), F32),
        scratch_shapes=[pltpu.SMEM((2, 1, TOP_K * tm), jnp.int32),
                        pltpu.VMEM((2, TOP_K * tm * COMBINE_PITCH, LANES), F32),
                        pltpu.VMEM((tm * COMBINE_PITCH, LANES), F32),
                        pltpu.SemaphoreType.DMA((2,)),
                        pltpu.SemaphoreType.DMA((2,))],
        compiler_params=_params(("arbitrary",), 48),
        name="combine_norm",
    )(dest_tiles, yb, h1, shared, w_lanes, g.reshape(1, d), b.reshape(1, d))


def _channel_mixer(h1, h1b, h1_slabs, w_router, router_bias, expert_w1, expert_w3, expert_w2,
                   shared_w1, shared_w3, shared_w2, ln2_g, ln2_b):
    idx_t, w_t, pos_t, counts = _router(h1, w_router, router_bias)
    row_tok, block_expert, n_used, next_expert, dest = _dispatch(idx_t, pos_t, counts)
    shared = _shared_expert(h1b, shared_w1, shared_w3, shared_w2)
    act = _expert_up(h1_slabs, row_tok, block_expert, n_used, next_expert, expert_w1, expert_w3)
    yb = _expert_down(act, block_expert, n_used, next_expert, expert_w2)
    return _combine(yb, dest, w_t, h1, shared, ln2_g, ln2_b)


def kernel(x, meta_tokens, ln0_g, ln0_b, w_in, attn_sinks, w_attn_o, ssm_lam_re, ssm_lam_im, ssm_log_dt,
           ssm_b_re, ssm_b_im, ssm_c_re, ssm_c_im, ssm_d, w_glu, w_ssm_o, w_out, ln1_g, ln1_b, w_router,
           router_bias, expert_w1, expert_w3, expert_w2, shared_w1, shared_w3, shared_w2, ln2_g, ln2_b):
    assert x.shape[0] == 1 and w_in.shape[0] == 1
    ssm_args = (ssm_lam_re[0], ssm_lam_im[0], ssm_log_dt[0], ssm_b_re[0], ssm_b_im[0],
                ssm_c_re[0], ssm_c_im[0], ssm_d[0])
    h1, h1b, h1_slabs = _token_mixer(x[0], meta_tokens, ln0_g, ln0_b, w_in[0], attn_sinks[0],
                                     w_attn_o[0], ssm_args, w_glu[0], w_ssm_o[0], w_out[0], ln1_g[0], ln1_b[0])
    out = _channel_mixer(h1, h1b, h1_slabs, w_router[0], router_bias[0], expert_w1[0], expert_w3[0],
                         expert_w2[0], shared_w1[0], shared_w3[0], shared_w2[0], ln2_g[0], ln2_b[0])
    return out[None]
```

```python
import functools
import math

import jax
import jax.numpy as jnp
import numpy as np
from jax import lax
from jax.experimental import pallas as pl
from jax.experimental.pallas import tpu as pltpu

F32 = jnp.float32
BF16 = jnp.bfloat16

D_MODEL = 4096
N_META = 16
BLOCK = 128
N_PAD = BLOCK - N_META

N_HEADS = 32
N_KV_HEADS = 4
HEAD_DIM = 64
HEADS_PER_KV = N_HEADS // N_KV_HEADS
ATTN_WIDTH = N_HEADS * HEAD_DIM
KV_WIDTH = N_KV_HEADS * HEAD_DIM

SSM_WIDTH = 1024
SSM_GROUP = 16
N_SSM_GROUPS = SSM_WIDTH // SSM_GROUP
SSM_STATE = 64
SSM_CHUNK = 16
SSM_TILE_GROUPS = 128 // SSM_GROUP
SSM_TILES = N_SSM_GROUPS // SSM_TILE_GROUPS

N_EXPERTS = 64
TOP_K = 8
N_EXPERT_GROUPS = 8
EXPERTS_PER_GROUP = N_EXPERTS // N_EXPERT_GROUPS
TOPK_GROUPS = 4
EXPERT_DIM = 512
SHARED_DIM = 512
ROUTED_SCALE = 2.5
ROW_BLOCK = 256

IN_WIDTH = ATTN_WIDTH + 2 * KV_WIDTH + SSM_WIDTH + 2 * D_MODEL
COL_KV = ATTN_WIDTH
COL_U = ATTN_WIDTH + 2 * KV_WIDTH
COL_GA = COL_U + SSM_WIDTH
COL_GS = COL_GA + D_MODEL

DEEPNORM_ALPHA = 2.0 ** 0.25
LN_EPS = 1e-5
NEG_INF = -1e30

MIB = 1024 * 1024


def _params(semantics, vmem_mib):
    return pltpu.CompilerParams(dimension_semantics=semantics, vmem_limit_bytes=vmem_mib * MIB)


def _tile(n, pref):
    t = min(n, pref)
    while n % t:
        t -= 128
    return t


def _ln_rows(x, g, b):
    mu = jnp.mean(x, axis=-1, keepdims=True)
    xc = x - mu
    var = jnp.mean(xc * xc, axis=-1, keepdims=True)
    return xc * lax.rsqrt(var + LN_EPS) * g + b


LANES = 128
SLAB = D_MODEL // LANES
SLAB_PITCH = SLAB + 4


def _store_slabs(ref, y):
    m = y.shape[0]
    for s in range(SLAB):
        ref[pl.ds(s, m, stride=SLAB_PITCH), :] = y[:, s * LANES:(s + 1) * LANES]
    for s in range(SLAB, SLAB_PITCH):
        ref[pl.ds(s, m, stride=SLAB_PITCH), :] = jnp.zeros((m, LANES), F32)


def _ln_kernel(x_ref, g_ref, b_ref, o_ref, ob_ref, *maybe_slab_ref):
    y = _ln_rows(x_ref[...], g_ref[...], b_ref[...])
    o_ref[...] = y
    ob_ref[...] = y.astype(BF16)
    for os_ref in maybe_slab_ref:
        _store_slabs(os_ref, y)


def _layer_norm(x, g, b, slabs=False):
    m, d = x.shape
    tm = _tile(m, 256)
    out_specs = [pl.BlockSpec((tm, d), lambda i: (i, 0)), pl.BlockSpec((tm, d), lambda i: (i, 0))]
    out_shape = [jax.ShapeDtypeStruct((m, d), F32), jax.ShapeDtypeStruct((m, d), BF16)]
    if slabs:
        out_specs.append(pl.BlockSpec((tm * SLAB_PITCH, LANES), lambda i: (i, 0)))
        out_shape.append(jax.ShapeDtypeStruct((m * SLAB_PITCH, LANES), F32))
    return pl.pallas_call(
        _ln_kernel,
        grid=(m // tm,),
        in_specs=[pl.BlockSpec((tm, d), lambda i: (i, 0)),
                  pl.BlockSpec((1, d), lambda i: (0, 0)),
                  pl.BlockSpec((1, d), lambda i: (0, 0))],
        out_specs=out_specs,
        out_shape=out_shape,
        compiler_params=_params(("parallel",), 40),
        name="layer_norm",
    )(x, g.reshape(1, d), b.reshape(1, d))


def _proj_kernel(a_ref, w_ref, o_ref):
    o_ref[...] = jnp.dot(a_ref[...], w_ref[...].astype(BF16),
                         preferred_element_type=F32).astype(o_ref.dtype)


def _project(a, w, col0, n_cols, tm_pref=1024, tn=512):
    m, k = a.shape
    tm = _tile(m, tm_pref)
    off = col0 // tn
    return pl.pallas_call(
        _proj_kernel,
        grid=(m // tm, n_cols // tn),
        in_specs=[pl.BlockSpec((tm, k), lambda i, j: (i, 0)),
                  pl.BlockSpec((k, tn), lambda i, j: (0, j + off))],
        out_specs=pl.BlockSpec((tm, tn), lambda i, j: (i, j)),
        out_shape=jax.ShapeDtypeStruct((m, n_cols), BF16),
        compiler_params=_params(("parallel", "arbitrary"), 56),
        name="in_proj",
    )(a, w)


def _attn_kernel(sink_ref, q_ref, kvc_ref, kvp_ref, kvm_ref, o_ref):
    first = pl.program_id(0) == 0
    kv_all = jnp.concatenate([kvp_ref[...], kvc_ref[...], kvm_ref[N_PAD:BLOCK, :]], axis=0)
    n_keys = 2 * BLOCK + N_META
    rows = HEADS_PER_KV * BLOCK

    lane = lax.broadcasted_iota(jnp.int32, (BLOCK, BLOCK), 1)
    lo_q = lane < HEAD_DIM
    lane_kv = lax.broadcasted_iota(jnp.int32, (n_keys, BLOCK), 1)
    lo_kv = lane_kv < HEAD_DIM

    qi = lax.broadcasted_iota(jnp.int32, (rows, BLOCK), 0) & (BLOCK - 1)
    kj = lax.broadcasted_iota(jnp.int32, (rows, BLOCK), 1)
    from_prev = kj > qi
    hidden = from_prev & first
    local_head = lax.broadcasted_iota(jnp.int32, (rows, 1), 0) // BLOCK

    def dup_head(tile, odd):
        t = tile.astype(F32)
        r = pltpu.roll(t, HEAD_DIM, 1)
        keep = jnp.logical_not(lo_kv) if odd else lo_kv
        return jnp.where(keep, t, r).astype(BF16)

    for kh in range(N_KV_HEADS):
        t0 = (kh // 2) * BLOCK
        k2 = dup_head(kv_all[:, t0:t0 + BLOCK], kh % 2)
        v2 = dup_head(kv_all[:, KV_WIDTH + t0:KV_WIDTH + t0 + BLOCK], kh % 2)
        pieces = []
        for p in range(HEADS_PER_KV // 2):
            c0 = kh * HEADS_PER_KV * HEAD_DIM + p * BLOCK
            qp = q_ref[:, c0:c0 + BLOCK] * jnp.asarray(HEAD_DIM ** -0.5, BF16)
            zero = jnp.zeros_like(qp)
            pieces.append(jnp.where(lo_q, qp, zero))
            pieces.append(jnp.where(lo_q, zero, qp))
        qs = jnp.concatenate(pieces, axis=0)
        s = lax.dot_general(qs, k2, (((1,), (1,)), ((), ())), preferred_element_type=F32)
        band = jnp.where(from_prev, s[:, :BLOCK], s[:, BLOCK:2 * BLOCK])
        band = jnp.where(hidden, NEG_INF, band)
        meta = s[:, 2 * BLOCK:]
        sink = jnp.zeros((rows, 1), F32)
        for t in range(HEADS_PER_KV):
            sink = jnp.where(local_head == t, sink_ref[kh * HEADS_PER_KV + t], sink)
        m = jnp.maximum(jnp.maximum(jnp.max(band, axis=-1, keepdims=True),
                                    jnp.max(meta, axis=-1, keepdims=True)), sink)
        e_band = jnp.exp(band - m)
        e_meta = jnp.exp(meta - m)
        den = (jnp.sum(e_band, axis=-1, keepdims=True) + jnp.sum(e_meta, axis=-1, keepdims=True)
               + jnp.exp(sink - m))
        e = jnp.concatenate([jnp.where(from_prev, e_band, 0.0), jnp.where(from_prev, 0.0, e_band), e_meta],
                            axis=1).astype(BF16)
        o2 = jnp.dot(e, v2, preferred_element_type=F32) / den
        for p in range(HEADS_PER_KV // 2):
            c0 = kh * HEADS_PER_KV * HEAD_DIM + p * BLOCK
            r0 = 2 * p * BLOCK
            o_ref[:, c0:c0 + BLOCK] = jnp.where(
                lo_q, o2[r0:r0 + BLOCK], o2[r0 + BLOCK:r0 + 2 * BLOCK]).astype(BF16)


def _attention(z, z_lead, sinks):
    t = z.shape[0]
    kv_blk = COL_KV // (2 * KV_WIDTH)
    return pl.pallas_call(
        _attn_kernel,
        grid=(t // BLOCK,),
        in_specs=[pl.BlockSpec(memory_space=pltpu.SMEM),
                  pl.BlockSpec((BLOCK, ATTN_WIDTH), lambda n: (n, 0)),
                  pl.BlockSpec((BLOCK, 2 * KV_WIDTH), lambda n: (n, kv_blk)),
                  pl.BlockSpec((BLOCK, 2 * KV_WIDTH), lambda n: (jnp.maximum(n - 1, 0), kv_blk)),
                  pl.BlockSpec((BLOCK, 2 * KV_WIDTH), lambda n: (0, 0))],
        out_specs=pl.BlockSpec((BLOCK, ATTN_WIDTH), lambda n: (n, 0)),
        out_shape=jax.ShapeDtypeStruct((t, ATTN_WIDTH), BF16),
        compiler_params=_params(("parallel",), 40),
        name="swa_attention",
    )(sinks.astype(F32), z, z, z, z_lead)


def _cmul(ar, ai, br, bi):
    return ar * br - ai * bi, ar * bi + ai * br


def _ssm_tables(lam_re, lam_im, log_dt, b_re, b_im, c_re, c_im, d_skip, n_chunks):
    hi = lax.Precision.HIGHEST
    g, p, h, c = N_SSM_GROUPS, SSM_STATE, SSM_GROUP, SSM_CHUNK
    dt = jnp.exp(log_dt)[:, None]
    mag = jnp.exp(dt * lam_re)
    are = mag * jnp.cos(dt * lam_im)
    aim = mag * jnp.sin(dt * lam_im)
    den = lam_re * lam_re + lam_im * lam_im
    num_re = are - 1.0
    coef_re = (num_re * lam_re + aim * lam_im) / den
    coef_im = (aim * lam_re - num_re * lam_im) / den
    bb_re = coef_re[..., None] * b_re - coef_im[..., None] * b_im
    bb_im = coef_re[..., None] * b_im + coef_im[..., None] * b_re

    pr, pi = [jnp.ones_like(are)], [jnp.zeros_like(are)]
    for _ in range(c):
        r, i = _cmul(pr[-1], pi[-1], are, aim)
        pr.append(r)
        pi.append(i)
    pr, pi = jnp.stack(pr), jnp.stack(pi)

    m_re = pr[:c, :, :, None] * bb_re[None] - pi[:c, :, :, None] * bb_im[None]
    m_im = pr[:c, :, :, None] * bb_im[None] + pi[:c, :, :, None] * bb_re[None]
    nt, gt = SSM_TILES, SSM_TILE_GROUPS
    lag = (jnp.einsum('ghp,kgpj->kgjh', c_re, m_re, precision=hi)
           - jnp.einsum('ghp,kgpj->kgjh', c_im, m_im, precision=hi))
    lagc = lag.reshape(c, nt, gt * h, h).transpose(1, 0, 2, 3)
    rev = np.arange(c - 1, -1, -1)
    bt_re, bt_im = bb_re.transpose(0, 2, 1), bb_im.transpose(0, 2, 1)
    wo_re = pr[rev][:, :, None, :] * bt_re[None] - pi[rev][:, :, None, :] * bt_im[None]
    wo_im = pr[rev][:, :, None, :] * bt_im[None] + pi[rev][:, :, None, :] * bt_re[None]

    def lanes(first, second):
        w = jnp.concatenate([first, second], axis=-1)
        return w.reshape(c, nt, gt * h, 2 * p).transpose(1, 0, 2, 3)

    wout_a, wout_b = lanes(wo_re, wo_im), lanes(wo_im, wo_re)
    ct_re, ct_im = c_re.transpose(0, 2, 1), c_im.transpose(0, 2, 1)
    cin_re = ct_re[None] * pr[1:, :, :, None] - ct_im[None] * pi[1:, :, :, None]
    cin_im = -(ct_re[None] * pi[1:, :, :, None] + ct_im[None] * pr[1:, :, :, None])
    cinc = jnp.stack([cin_re.reshape(c, nt, gt * p, h), cin_im.reshape(c, nt, gt * p, h)], axis=2)
    cinc = cinc.transpose(1, 0, 2, 3, 4).reshape(nt, c, 2 * gt * p, h)
    spread = jnp.tile(jnp.eye(h, dtype=BF16), (1, gt))

    n_steps = max(1, math.ceil(math.log2(n_chunks)))
    qr, qi = [pr[c]], [pi[c]]
    for _ in range(n_steps - 1):
        r, i = _cmul(qr[-1], qi[-1], qr[-1], qi[-1])
        qr.append(r)
        qi.append(i)
    apr = jnp.stack(qr, axis=0).reshape(n_steps, nt, gt * p).transpose(1, 0, 2)
    api = jnp.stack(qi, axis=0).reshape(n_steps, nt, gt * p).transpose(1, 0, 2)
    dsk = d_skip.reshape(nt, 1, gt * h)
    return (lagc.astype(BF16), wout_a.astype(BF16), wout_b.astype(BF16), cinc.astype(BF16), spread,
            apr, api, dsk, n_steps)


def _chunk_steps(u_ref):
    n = u_ref.shape[0] // SSM_CHUNK
    return [u_ref[pl.ds(s, n, stride=SSM_CHUNK), :] for s in range(SSM_CHUNK)]


def _chunk_lanes(u_ref):
    return jnp.concatenate(_chunk_steps(u_ref), axis=1)


def _ssm_state_kernel(u_ref, lead_ref, wa_ref, wb_ref, apr_ref, api_ref, prev_ref, wout_s, *, n_steps):
    p, gh = SSM_STATE, SSM_TILE_GROUPS * SSM_GROUP
    row_g = lax.broadcasted_iota(jnp.int32, (gh, 2 * p), 0) // SSM_GROUP
    lo = lax.broadcasted_iota(jnp.int32, (gh, 2 * p), 1) < p
    zero = jnp.zeros((gh, 2 * p), BF16)
    for s in range(SSM_CHUNK):
        a, b = wa_ref[0, s], wb_ref[0, s]
        for j in range(SSM_TILE_GROUPS // 2):
            even, odd = (row_g == 2 * j) & lo, (row_g == 2 * j + 1) & jnp.logical_not(lo)
            wout_s[s * gh:(s + 1) * gh, j * 2 * p:(j + 1) * 2 * p] = (
                jnp.where(even, a, jnp.where(odd, b, zero)))
            wout_s[s * gh:(s + 1) * gh, (SSM_TILE_GROUPS // 2 + j) * 2 * p:(SSM_TILE_GROUPS // 2 + j + 1) * 2 * p] = (
                jnp.where(even, b, jnp.where(odd, a, zero)))
    wout = wout_s[...]
    st = jnp.dot(_chunk_lanes(u_ref).astype(BF16), wout, preferred_element_type=F32)
    lead = jnp.dot(_chunk_lanes(lead_ref).astype(BF16), wout, preferred_element_type=F32)
    nc, half = st.shape[0], st.shape[1] // 2
    xr, xi = st[:, :half], st[:, half:]
    mr, mi = lead[-1:, :half], lead[-1:, half:]
    row = lax.broadcasted_iota(jnp.int32, (nc, half), 0)
    ar, ai = apr_ref[0, 0:1, :], api_ref[0, 0:1, :]
    xr = xr + jnp.where(row == 0, ar * mr - ai * mi, 0.0)
    xi = xi + jnp.where(row == 0, ar * mi + ai * mr, 0.0)

    def shift(x, d):
        return jnp.where(row >= d, pltpu.roll(x, d, 0), 0.0)

    for j in range(n_steps):
        ar, ai = apr_ref[0, j:j + 1, :], api_ref[0, j:j + 1, :]
        sr, si = shift(xr, 1 << j), shift(xi, 1 << j)
        xr, xi = xr + (ar * sr - ai * si), xi + (ar * si + ai * sr)
    pr = jnp.where(row == 0, mr, shift(xr, 1))
    pi = jnp.where(row == 0, mi, shift(xi, 1))
    prev_ref[0] = jnp.concatenate([pr, pi], axis=1).astype(BF16)


def _ssm_out_kernel(u_ref, prev_ref, lag_ref, cin_ref, spread_ref, d_ref, y_ref, toep_s, wcin_s):
    gh, c = SSM_TILE_GROUPS * SSM_GROUP, SSM_CHUNK

    @pl.when(pl.program_id(1) == 0)
    def _():
        spread = spread_ref[...]
        diag = (lax.broadcasted_iota(jnp.int32, (gh, gh), 0) // SSM_GROUP
                == lax.broadcasted_iota(jnp.int32, (gh, gh), 1) // SSM_GROUP)
        zero = jnp.zeros((gh, gh), BF16)
        blocks = [jnp.where(diag, jnp.dot(lag_ref[0, k], spread, preferred_element_type=F32), 0.0).astype(BF16)
                  for k in range(c)]
        for s in range(c):
            for i in range(c):
                toep_s[s * gh:(s + 1) * gh, i * gh:(i + 1) * gh] = blocks[i - s] if i >= s else zero
        rows = wcin_s.shape[0]
        row_g = (lax.broadcasted_iota(jnp.int32, (rows, gh), 0) % (rows // 2)) // SSM_STATE
        own = row_g == lax.broadcasted_iota(jnp.int32, (rows, gh), 1) // SSM_GROUP
        for i in range(c):
            wcin_s[:, i * gh:(i + 1) * gh] = jnp.where(
                own, jnp.dot(cin_ref[0, i], spread, preferred_element_type=F32), 0.0).astype(BF16)

    us = _chunk_steps(u_ref)
    cb = us[0].shape[0]
    u = jnp.concatenate(us, axis=1).astype(BF16)
    y = (jnp.dot(u, toep_s[...], preferred_element_type=F32)
         + jnp.dot(prev_ref[0], wcin_s[...], preferred_element_type=F32))
    for s in range(SSM_CHUNK):
        ys = y[:, s * LANES:(s + 1) * LANES] + d_ref[0] * us[s]
        y_ref[pl.ds(s, cb, stride=SSM_CHUNK), :] = 0.5 * ys * (1.0 + lax.erf(ys * (0.5 ** 0.5)))


def _ssm(u, u_lead, tables):
    lagc, wout_a, wout_b, cinc, spread, apr, api, dsk, n_steps = tables
    t = u.shape[0]
    nc = t // SSM_CHUNK
    cb = _tile(nc, 256) if nc % 128 == 0 else nc
    lanes = LANES
    k = SSM_CHUNK * lanes
    states = 2 * SSM_TILE_GROUPS * SSM_STATE
    prev = pl.pallas_call(
        functools.partial(_ssm_state_kernel, n_steps=n_steps),
        grid=(SSM_TILES,),
        in_specs=[pl.BlockSpec((t, lanes), lambda g: (0, g)),
                  pl.BlockSpec((BLOCK, lanes), lambda g: (0, g)),
                  pl.BlockSpec((1, SSM_CHUNK, lanes, 2 * SSM_STATE), lambda g: (g, 0, 0, 0)),
                  pl.BlockSpec((1, SSM_CHUNK, lanes, 2 * SSM_STATE), lambda g: (g, 0, 0, 0)),
                  pl.BlockSpec((1, n_steps, states // 2), lambda g: (g, 0, 0)),
                  pl.BlockSpec((1, n_steps, states // 2), lambda g: (g, 0, 0))],
        out_specs=pl.BlockSpec((1, nc, states), lambda g: (g, 0, 0)),
        out_shape=jax.ShapeDtypeStruct((SSM_TILES, nc, states), BF16),
        scratch_shapes=[pltpu.VMEM((k, states), BF16)],
        compiler_params=_params(("parallel",), 48),
        name="s5_states",
    )(u, u_lead, wout_a, wout_b, apr, api)
    return pl.pallas_call(
        _ssm_out_kernel,
        grid=(SSM_TILES, nc // cb),
        in_specs=[pl.BlockSpec((cb * SSM_CHUNK, lanes), lambda g, c: (c, g)),
                  pl.BlockSpec((1, cb, states), lambda g, c: (g, c, 0)),
                  pl.BlockSpec((1, SSM_CHUNK, lanes, SSM_GROUP), lambda g, c: (g, 0, 0, 0)),
                  pl.BlockSpec((1, SSM_CHUNK, states, SSM_GROUP), lambda g, c: (g, 0, 0, 0)),
                  pl.BlockSpec((SSM_GROUP, lanes), lambda g, c: (0, 0)),
                  pl.BlockSpec((1, 1, lanes), lambda g, c: (g, 0, 0))],
        out_specs=pl.BlockSpec((cb * SSM_CHUNK, lanes), lambda g, c: (c, g)),
        out_shape=jax.ShapeDtypeStruct((t, SSM_WIDTH), F32),
        scratch_shapes=[pltpu.VMEM((k, k), BF16), pltpu.VMEM((states, k), BF16)],
        compiler_params=_params(("parallel", "arbitrary"), 48),
        name="s5_outputs",
    )(u, prev, lagc, cinc, spread, dsk)


def _glu_kernel(y_ref, w_ref, o_ref):
    y = y_ref[...]
    gate = jax.nn.sigmoid(jnp.dot(y.astype(BF16), w_ref[...].astype(BF16), preferred_element_type=F32))
    o_ref[...] = (y * gate).astype(BF16)


def _glu(y, w):
    m, d = y.shape
    tm = _tile(m, 1024)
    return pl.pallas_call(
        _glu_kernel,
        grid=(m // tm,),
        in_specs=[pl.BlockSpec((tm, d), lambda i: (i, 0)),
                  pl.BlockSpec((d, d), lambda i: (0, 0))],
        out_specs=pl.BlockSpec((tm, d), lambda i: (i, 0)),
        out_shape=jax.ShapeDtypeStruct((m, d), BF16),
        compiler_params=_params(("parallel",), 40),
        name="ssm_glu",
    )(y, w)


def _mix_kernel(a_ref, y_ref, wa_ref, ws_ref, ga_ref, gs_ref, o_ref):
    attn_d = jnp.dot(a_ref[...], wa_ref[...].astype(BF16), preferred_element_type=F32)
    ssm_d = jnp.dot(y_ref[...], ws_ref[...].astype(BF16), preferred_element_type=F32)
    mix = (jax.nn.sigmoid(ga_ref[...].astype(F32)) * attn_d
           + jax.nn.sigmoid(gs_ref[...].astype(F32)) * ssm_d)
    o_ref[...] = mix.astype(BF16)


def _mix(attn, y2, w_attn_o, w_ssm_o, z):
    m = attn.shape[0]
    tm, tn = _tile(m, 1024), 512
    ga_blk, gs_blk = COL_GA // tn, COL_GS // tn
    return pl.pallas_call(
        _mix_kernel,
        grid=(m // tm, D_MODEL // tn),
        in_specs=[pl.BlockSpec((tm, ATTN_WIDTH), lambda i, j: (i, 0)),
                  pl.BlockSpec((tm, SSM_WIDTH), lambda i, j: (i, 0)),
                  pl.BlockSpec((ATTN_WIDTH, tn), lambda i, j: (0, j)),
                  pl.BlockSpec((SSM_WIDTH, tn), lambda i, j: (0, j)),
                  pl.BlockSpec((tm, tn), lambda i, j: (i, j + ga_blk)),
                  pl.BlockSpec((tm, tn), lambda i, j: (i, j + gs_blk))],
        out_specs=pl.BlockSpec((tm, tn), lambda i, j: (i, j)),
        out_shape=jax.ShapeDtypeStruct((m, D_MODEL), BF16),
        compiler_params=_params(("parallel", "arbitrary"), 48),
        name="gated_merge",
    )(attn, y2, w_attn_o, w_ssm_o, z, z)


def _resid_kernel(a_ref, w_ref, h_ref, o_ref):
    o_ref[...] = DEEPNORM_ALPHA * h_ref[...] + jnp.dot(
        a_ref[...], w_ref[...].astype(BF16), preferred_element_type=F32)


def _out_proj(mix, w_out, h0):
    m = mix.shape[0]
    tm, tn = _tile(m, 1024), 512
    return pl.pallas_call(
        _resid_kernel,
        grid=(m // tm, D_MODEL // tn),
        in_specs=[pl.BlockSpec((tm, D_MODEL), lambda i, j: (i, 0)),
                  pl.BlockSpec((D_MODEL, tn), lambda i, j: (0, j)),
                  pl.BlockSpec((tm, tn), lambda i, j: (i, j))],
        out_specs=pl.BlockSpec((tm, tn), lambda i, j: (i, j)),
        out_shape=jax.ShapeDtypeStruct((m, D_MODEL), F32),
        compiler_params=_params(("parallel", "arbitrary"), 56),
        name="out_proj",
    )(mix, w_out, h0)


def _token_mixer(x, meta_tokens, ln0_g, ln0_b, w_in, attn_sinks, w_attn_o, ssm_tables_args,
                 w_glu, w_ssm_o, w_out, ln1_g, ln1_b):
    t = x.shape[0]
    lead = jnp.concatenate([jnp.zeros((N_PAD, D_MODEL), F32), meta_tokens.astype(F32)], axis=0)
    h0, h0b = _layer_norm(x, ln0_g, ln0_b)
    _, leadb = _layer_norm(lead, ln0_g, ln0_b)

    z = _project(h0b, w_in, 0, IN_WIDTH)
    z_lead = _project(leadb, w_in, COL_KV, 2 * KV_WIDTH + SSM_WIDTH)

    attn = _attention(z, z_lead, attn_sinks)

    lead_valid = (jnp.arange(BLOCK) >= N_PAD)[:, None]
    u_lead = jnp.where(lead_valid, z_lead[:, 2 * KV_WIDTH:].astype(F32), 0.0)
    tables = _ssm_tables(*ssm_tables_args, n_chunks=t // SSM_CHUNK)
    y = _ssm(z[:, COL_U:COL_GA].astype(F32), u_lead, tables)
    y2 = _glu(y, w_glu)

    mix = _mix(attn, y2, w_attn_o, w_ssm_o, z)
    r = _out_proj(mix, w_out, h0)
    return _layer_norm(r, ln1_g, ln1_b, slabs=True)


def _router_kernel(h_ref, wr_ref, bias_ref, idx_ref, wt_ref, pos_ref, cnt_ref):
    tm = h_ref.shape[0]

    @pl.when(pl.program_id(0) == 0)
    def _():
        cnt_ref[...] = jnp.zeros_like(cnt_ref)

    ng, ne = N_EXPERT_GROUPS, EXPERTS_PER_GROUP
    logits = lax.dot_general(wr_ref[...], h_ref[...], (((1,), (1,)), ((), ())),
                             precision=lax.Precision.HIGHEST, preferred_element_type=F32)
    scores = jax.nn.sigmoid(logits)
    sel = (scores + bias_ref[...]).reshape(ng, ne, tm)
    scores = scores.reshape(ng, ne, tm)
    e_in = lax.broadcasted_iota(jnp.int32, (ng, ne, tm), 1)
    e_id = lax.broadcasted_iota(jnp.int32, (ng, ne, tm), 0) * ne + e_in
    neg = -jnp.inf

    top1 = jnp.max(sel, axis=1, keepdims=True)
    first = jnp.min(jnp.where(sel == top1, e_in, ne), axis=1, keepdims=True)
    top2 = jnp.max(jnp.where(e_in == first, neg, sel), axis=1, keepdims=True)
    grp = (top1 + top2)[:, 0, :]

    g_id = lax.broadcasted_iota(jnp.int32, (ng, tm), 0)
    chosen = jnp.zeros((ng, tm), jnp.bool_)
    for _ in range(TOPK_GROUPS):
        best = jnp.max(grp, axis=0, keepdims=True)
        pick = g_id == jnp.min(jnp.where(grp == best, g_id, ng), axis=0, keepdims=True)
        chosen = chosen | pick
        grp = jnp.where(pick, neg, grp)

    cur = jnp.where(chosen[:, None, :], sel, NEG_INF)
    def pick_sum(pick, values):
        part = jnp.sum(jnp.where(pick, values, 0.0), axis=1, keepdims=True)
        return jnp.sum(part, axis=0, keepdims=True)[0]

    ids, wts, picks = [], [], []
    for _ in range(TOP_K):
        best = jnp.max(jnp.max(cur, axis=1, keepdims=True), axis=0, keepdims=True)
        cand = jnp.where(cur == best, e_id, N_EXPERTS)
        win = jnp.min(jnp.min(cand, axis=1, keepdims=True), axis=0, keepdims=True)
        pick = e_id == win
        ids.append(win[0])
        wts.append(pick_sum(pick, scores))
        picks.append(pick)
        cur = jnp.where(pick, neg, cur)
    w = jnp.concatenate(wts, axis=0)
    idx_ref[...] = jnp.concatenate(ids, axis=0)
    wt_ref[...] = w / jnp.sum(w, axis=0, keepdims=True) * ROUTED_SCALE

    taken = jnp.where(cur == neg, 1.0, 0.0).reshape(N_EXPERTS, tm)
    before = (lax.broadcasted_iota(jnp.int32, (tm, tm), 0)
              < lax.broadcasted_iota(jnp.int32, (tm, tm), 1))
    rank = jnp.dot(taken.astype(BF16), jnp.where(before, 1.0, 0.0).astype(BF16),
                   preferred_element_type=F32)
    pos = (cnt_ref[:, 0:1] + rank).reshape(ng, ne, tm)
    pos_ref[...] = jnp.concatenate([pick_sum(p, pos) for p in picks], axis=0).astype(jnp.int32)
    cnt_ref[...] = cnt_ref[...] + jnp.sum(taken, axis=1, keepdims=True)


def _router(h1, w_router, router_bias):
    t, d = h1.shape
    tm = _tile(t, 512)
    return pl.pallas_call(
        _router_kernel,
        grid=(t // tm,),
        in_specs=[pl.BlockSpec((tm, d), lambda i: (i, 0)),
                  pl.BlockSpec((N_EXPERTS, d), lambda i: (0, 0)),
                  pl.BlockSpec((N_EXPERTS, 1), lambda i: (0, 0))],
        out_specs=[pl.BlockSpec((TOP_K, tm), lambda i: (0, i)),
                   pl.BlockSpec((TOP_K, tm), lambda i: (0, i)),
                   pl.BlockSpec((TOP_K, tm), lambda i: (0, i)),
                   pl.BlockSpec((N_EXPERTS, LANES), lambda i: (0, 0))],
        out_shape=[jax.ShapeDtypeStruct((TOP_K, t), jnp.int32),
                   jax.ShapeDtypeStruct((TOP_K, t), F32),
                   jax.ShapeDtypeStruct((TOP_K, t), jnp.int32),
                   jax.ShapeDtypeStruct((N_EXPERTS, LANES), F32)],
        compiler_params=_params(("arbitrary",), 40),
        name="router",
    )(h1, w_router.T, router_bias.reshape(N_EXPERTS, 1))


def _dispatch(idx_t, pos_t, counts):
    t = idx_t.shape[1]
    n_assign = TOP_K * t
    n_blocks = -(-n_assign // ROW_BLOCK) + N_EXPERTS
    n_rows = n_blocks * ROW_BLOCK
    counts = counts[:, 0].astype(jnp.int32)
    padded = (counts + ROW_BLOCK - 1) // ROW_BLOCK * ROW_BLOCK
    ends = jnp.cumsum(padded)
    starts = ends - padded
    own = idx_t[:, :, None] == jnp.arange(N_EXPERTS, dtype=jnp.int32)[None, None, :]
    dest = (jnp.sum(jnp.where(own, starts[None, None, :], 0), axis=-1) + pos_t).reshape(-1)
    tok_flat = jnp.tile(jnp.arange(t, dtype=jnp.int32), TOP_K)
    row_tok = jnp.zeros((n_rows,), jnp.int32).at[dest].set(
        tok_flat, unique_indices=True, mode='promise_in_bounds')
    block_start = jnp.arange(n_blocks, dtype=jnp.int32) * ROW_BLOCK
    block_expert = jnp.minimum(
        jnp.sum((ends[None, :] <= block_start[:, None]).astype(jnp.int32), axis=1), N_EXPERTS - 1)
    n_used = (ends[-1] // ROW_BLOCK).astype(jnp.int32).reshape(1)
    has = counts > 0
    cand = jnp.where(has[None, :] & (jnp.arange(N_EXPERTS)[None, :] > jnp.arange(N_EXPERTS)[:, None]),
                     jnp.arange(N_EXPERTS)[None, :], N_EXPERTS)
    nxt = jnp.min(cand, axis=1)
    next_expert = jnp.where(nxt < N_EXPERTS, nxt, -1).astype(jnp.int32)
    return (row_tok.reshape(n_blocks, 1, ROW_BLOCK), block_expert, n_used, next_expert,
            dest.reshape(TOP_K, t))


WEIGHT_DMA_PRIORITY = 1
GATHER_SLOTS = 4


def _gather_slabs(src_hbm, idx_smem, dst, sem, n, unrolled=False, dst_pitch=None):
    dst_pitch = SLAB_PITCH if dst_pitch is None else dst_pitch

    def body(i, carry):
        src = idx_smem[0, i] * SLAB_PITCH
        pltpu.make_async_copy(src_hbm.at[pl.ds(src, SLAB)], dst.at[pl.ds(i * dst_pitch, SLAB)], sem).start()
        return carry
    if unrolled:
        for i in range(n):
            body(i, 0)
    else:
        lax.fori_loop(0, n, body, 0, unroll=8)


def _wait_slabs(src_hbm, dst, sem, n):
    pltpu.make_async_copy(src_hbm.at[pl.ds(0, n * SLAB)], dst.at[pl.ds(0, n * SLAB)], sem).wait()


def _slab_rows(ref, n, pitch=None):
    pitch = SLAB_PITCH if pitch is None else pitch
    return jnp.concatenate([ref[pl.ds(s, n, stride=pitch), :] for s in range(SLAB)], axis=1)


def _expert_up_kernel(be_ref, nu_ref, nx_ref, tok_hbm, h_hbm, w1_hbm, w3_hbm, o_ref,
                      idx_smem, xbuf, wstage, wcast, isem, xsem, wsem):
    b = pl.program_id(0)
    n_used = nu_ref[0]
    last = n_used - 1
    e = be_ref[b]
    ns = GATHER_SLOTS
    slot = b % ns
    ahead = (b + ns - 1) % ns

    def idx_copy(blk, s):
        return pltpu.make_async_copy(tok_hbm.at[jnp.minimum(blk, last)], idx_smem.at[s], isem.at[s])

    def weight_copies(ex):
        return (pltpu.make_async_copy(w1_hbm.at[ex], wstage.at[0], wsem.at[0]),
                pltpu.make_async_copy(w3_hbm.at[ex], wstage.at[1], wsem.at[1]))

    @pl.when(b == 0)
    def _():
        for c in weight_copies(e):
            c.start(priority=WEIGHT_DMA_PRIORITY)
        for k in range(ns - 1):
            idx_copy(k, k).start()
            idx_copy(k, k).wait()
            _gather_slabs(h_hbm, idx_smem.at[k], xbuf.at[k], xsem.at[k], ROW_BLOCK)
        idx_copy(ns - 1, ns - 1).start()

    @pl.when(b < n_used)
    def _():
        first_of_expert = jnp.logical_or(b == 0, be_ref[jnp.maximum(b - 1, 0)] != e)

        @pl.when(first_of_expert)
        def _():
            for c in weight_copies(e):
                c.wait()
            rows = wstage.shape[1] // 8
            for m in range(2):
                for r in range(8):
                    wcast[m, r * rows:(r + 1) * rows] = wstage[m, r * rows:(r + 1) * rows].astype(BF16)
            nxt = nx_ref[e]

            @pl.when(nxt >= 0)
            def _():
                for c in weight_copies(nxt):
                    c.start(priority=WEIGHT_DMA_PRIORITY)

        idx_copy(b + ns - 1, ahead).wait()
        _wait_slabs(h_hbm, xbuf.at[slot], xsem.at[slot], ROW_BLOCK)
        _gather_slabs(h_hbm, idx_smem.at[ahead], xbuf.at[ahead], xsem.at[ahead], ROW_BLOCK, unrolled=True)
        idx_copy(b + ns, slot).start()
        x = _slab_rows(xbuf.at[slot], ROW_BLOCK).astype(BF16)
        a = jnp.dot(x, wcast[0], preferred_element_type=F32)
        g = jnp.dot(x, wcast[1], preferred_element_type=F32)
        o_ref[...] = (a * jax.nn.sigmoid(a) * g).astype(BF16)

        @pl.when(b == last)
        def _():
            for k in range(1, ns):
                s = (b + k) % ns
                _wait_slabs(h_hbm, xbuf.at[s], xsem.at[s], ROW_BLOCK)
            idx_copy(last, slot).wait()

    @pl.when(b >= n_used)
    def _():
        o_ref[...] = jnp.zeros_like(o_ref)


def _expert_up(h1, row_tok, block_expert, n_used, next_expert, w1, w3):
    n_blocks = row_tok.shape[0]
    d = w1.shape[1]
    grid_spec = pltpu.PrefetchScalarGridSpec(
        num_scalar_prefetch=3,
        grid=(n_blocks,),
        in_specs=[pl.BlockSpec(memory_space=pl.ANY)] * 4,
        out_specs=pl.BlockSpec((ROW_BLOCK, EXPERT_DIM), lambda b, *_: (b, 0)),
        scratch_shapes=[pltpu.SMEM((GATHER_SLOTS, 1, ROW_BLOCK), jnp.int32),
                        pltpu.VMEM((GATHER_SLOTS, ROW_BLOCK * SLAB_PITCH, LANES), F32),
                        pltpu.VMEM((2, d, EXPERT_DIM), F32),
                        pltpu.VMEM((2, d, EXPERT_DIM), BF16),
                        pltpu.SemaphoreType.DMA((GATHER_SLOTS,)),
                        pltpu.SemaphoreType.DMA((GATHER_SLOTS,)),
                        pltpu.SemaphoreType.DMA((2,))])
    return pl.pallas_call(
        _expert_up_kernel,
        grid_spec=grid_spec,
        out_shape=jax.ShapeDtypeStruct((n_blocks * ROW_BLOCK, EXPERT_DIM), BF16),
        compiler_params=_params(("arbitrary",), 58),
        name="expert_up",
    )(block_expert, n_used, next_expert, row_tok, h1, w1, w3)


def _expert_down_kernel(be_ref, nu_ref, nx_ref, a_ref, w2_hbm, o_ref, wstage, wcast, wsem):
    b = pl.program_id(0)
    n_used = nu_ref[0]
    e = be_ref[b]

    def weight_copy(ex):
        return pltpu.make_async_copy(w2_hbm.at[ex], wstage, wsem)

    @pl.when(b == 0)
    def _():
        weight_copy(e).start(priority=WEIGHT_DMA_PRIORITY)

    @pl.when(b < n_used)
    def _():
        first_of_expert = jnp.logical_or(b == 0, be_ref[jnp.maximum(b - 1, 0)] != e)

        @pl.when(first_of_expert)
        def _():
            weight_copy(e).wait()
            wcast[...] = wstage[...].astype(BF16)
            nxt = nx_ref[e]

            @pl.when(nxt >= 0)
            def _():
                weight_copy(nxt).start(priority=WEIGHT_DMA_PRIORITY)

        _store_slabs(o_ref, jnp.dot(a_ref[...], wcast[...], preferred_element_type=F32))

    @pl.when(b >= n_used)
    def _():
        o_ref[...] = jnp.zeros_like(o_ref)


def _expert_down(act, block_expert, n_used, next_expert, w2):
    n_rows = act.shape[0]
    n_blocks = n_rows // ROW_BLOCK
    d = w2.shape[2]
    grid_spec = pltpu.PrefetchScalarGridSpec(
        num_scalar_prefetch=3,
        grid=(n_blocks,),
        in_specs=[pl.BlockSpec((ROW_BLOCK, EXPERT_DIM), lambda b, *_: (b, 0)),
                  pl.BlockSpec(memory_space=pl.ANY)],
        out_specs=pl.BlockSpec((ROW_BLOCK * SLAB_PITCH, LANES), lambda b, *_: (b, 0)),
        scratch_shapes=[pltpu.VMEM((EXPERT_DIM, d), F32),
                        pltpu.VMEM((EXPERT_DIM, d), BF16),
                        pltpu.SemaphoreType.DMA(())])
    return pl.pallas_call(
        _expert_down_kernel,
        grid_spec=grid_spec,
        out_shape=jax.ShapeDtypeStruct((n_rows * SLAB_PITCH, LANES), F32),
        compiler_params=_params(("arbitrary",), 40),
        name="expert_down",
    )(block_expert, n_used, next_expert, act, w2)


def _shared_kernel(h_ref, w1_ref, w3_ref, w2_ref, o_ref):
    x = h_ref[...]
    a = jnp.dot(x, w1_ref[...], preferred_element_type=F32)
    g = jnp.dot(x, w3_ref[...], preferred_element_type=F32)
    act = (a * jax.nn.sigmoid(a) * g).astype(BF16)
    o_ref[...] = jnp.dot(act, w2_ref[...], preferred_element_type=F32)


def _shared_expert(h1b, w1, w3, w2):
    t, d = h1b.shape
    tm = _tile(t, 512)
    return pl.pallas_call(
        _shared_kernel,
        grid=(t // tm,),
        in_specs=[pl.BlockSpec((tm, d), lambda i: (i, 0)),
                  pl.BlockSpec((d, SHARED_DIM), lambda i: (0, 0)),
                  pl.BlockSpec((d, SHARED_DIM), lambda i: (0, 0)),
                  pl.BlockSpec((SHARED_DIM, d), lambda i: (0, 0))],
        out_specs=pl.BlockSpec((tm, d), lambda i: (i, 0)),
        out_shape=jax.ShapeDtypeStruct((t, d), F32),
        compiler_params=_params(("parallel",), 56),
        name="shared_expert",
    )(h1b, w1.astype(BF16), w3.astype(BF16), w2.astype(BF16))


COMBINE_TOKENS = 64
COMBINE_PITCH = SLAB + 8


def _combine_kernel(dest_hbm, y_hbm, h_ref, s_ref, w_ref, g_ref, b_ref, o_ref,
                    idx_smem, ybuf, stage, isem, ysem):
    i = pl.program_id(0)
    n = pl.num_programs(0)
    slot = i % 2
    rows = TOP_K * COMBINE_TOKENS

    def idx_copy(blk, s):
        return pltpu.make_async_copy(dest_hbm.at[blk], idx_smem.at[s], isem.at[s])

    last = n - 1

    pitch = COMBINE_PITCH
    per_j = COMBINE_TOKENS * pitch

    @pl.when(i == 0)
    def _():
        ybuf[...] = jnp.zeros_like(ybuf)
        idx_copy(0, 0).start()
        idx_copy(0, 0).wait()
        _gather_slabs(y_hbm, idx_smem.at[0], ybuf.at[0], ysem.at[0], rows, dst_pitch=pitch)
        idx_copy(jnp.minimum(1, last), 1).start()

    idx_copy(jnp.minimum(i + 1, last), 1 - slot).wait()
    _wait_slabs(y_hbm, ybuf.at[slot], ysem.at[slot], rows)
    _gather_slabs(y_hbm, idx_smem.at[1 - slot], ybuf.at[1 - slot], ysem.at[1 - slot], rows,
                  unrolled=True, dst_pitch=pitch)
    idx_copy(jnp.minimum(i + 2, last), slot).start()
    acc = None
    for j in range(TOP_K):
        wj = jnp.broadcast_to(w_ref[:, j:j + 1, :], (COMBINE_TOKENS, pitch, LANES)).reshape(per_j, LANES)
        term = wj * ybuf[slot, j * per_j:(j + 1) * per_j, :]
        acc = term if acc is None else acc + term
    stage[...] = acc
    routed = _slab_rows(stage, COMBINE_TOKENS, pitch)
    o_ref[...] = _ln_rows(DEEPNORM_ALPHA * h_ref[...] + s_ref[...] + routed, g_ref[...], b_ref[...])

    @pl.when(i == last)
    def _():
        _wait_slabs(y_hbm, ybuf.at[1 - slot], ysem.at[1 - slot], rows)
        idx_copy(last, slot).wait()


def _combine(yb, dest, w_t, h1, shared, g, b):
    t, d = h1.shape
    tm = COMBINE_TOKENS
    n_tiles = t // tm
    dest_tiles = dest.reshape(TOP_K, n_tiles, tm).transpose(1, 0, 2).reshape(n_tiles, 1, TOP_K * tm)
    w_lanes = jnp.broadcast_to(w_t.T[:, :, None], (t, TOP_K, LANES))
    return pl.pallas_call(
        _combine_kernel,
        grid=(n_tiles,),
        in_specs=[pl.BlockSpec(memory_space=pl.ANY),
                  pl.BlockSpec(memory_space=pl.ANY),
                  pl.BlockSpec((tm, d), lambda i: (i, 0)),
                  pl.BlockSpec((tm, d), lambda i: (i, 0)),
                  pl.BlockSpec((tm, TOP_K, LANES), lambda i: (i, 0, 0)),
                  pl.BlockSpec((1, d), lambda i: (0, 0)),
                  pl.BlockSpec((1, d), lambda i: (0, 0))],
        out_specs=pl.BlockSpec((tm, d), lambda i: (i, 0)),
        out_shape=jax.ShapeDtypeStruct((t, d), F32),
        scratch_shapes=[pltpu.SMEM((2, 1, TOP_K * tm), jnp.int32),
                        pltpu.VMEM((2, TOP_K * tm * COMBINE_PITCH, LANES), F32),
                        pltpu.VMEM((tm * COMBINE_PITCH, LANES), F32),
                        pltpu.SemaphoreType.DMA((2,)),
                        pltpu.SemaphoreType.DMA((2,))],
        compiler_params=_params(("arbitrary",), 48),
        name="combine_norm",
    )(dest_tiles, yb, h1, shared, w_lanes, g.reshape(1, d), b.reshape(1, d))


def _channel_mixer(h1, h1b, h1_slabs, w_router, router_bias, expert_w1, expert_w3, expert_w2,
                   shared_w1, shared_w3, shared_w2, ln2_g, ln2_b):
    idx_t, w_t, pos_t, counts = _router(h1, w_router, router_bias)
    row_tok, block_expert, n_used, next_expert, dest = _dispatch(idx_t, pos_t, counts)
    shared = _shared_expert(h1b, shared_w1, shared_w3, shared_w2)
    act = _expert_up(h1_slabs, row_tok, block_expert, n_used, next_expert, expert_w1, expert_w3)
    yb = _expert_down(act, block_expert, n_used, next_expert, expert_w2)
    return _combine(yb, dest, w_t, h1, shared, ln2_g, ln2_b)


def kernel(x, meta_tokens, ln0_g, ln0_b, w_in, attn_sinks, w_attn_o, ssm_lam_re, ssm_lam_im, ssm_log_dt,
           ssm_b_re, ssm_b_im, ssm_c_re, ssm_c_im, ssm_d, w_glu, w_ssm_o, w_out, ln1_g, ln1_b, w_router,
           router_bias, expert_w1, expert_w3, expert_w2, shared_w1, shared_w3, shared_w2, ln2_g, ln2_b):
    assert x.shape[0] == 1 and w_in.shape[0] == 1
    ssm_args = (ssm_lam_re[0], ssm_lam_im[0], ssm_log_dt[0], ssm_b_re[0], ssm_b_im[0],
                ssm_c_re[0], ssm_c_im[0], ssm_d[0])
    h1, h1b, h1_slabs = _token_mixer(x[0], meta_tokens, ln0_g, ln0_b, w_in[0], attn_sinks[0],
                                     w_attn_o[0], ssm_args, w_glu[0], w_ssm_o[0], w_out[0], ln1_g[0], ln1_b[0])
    out = _channel_mixer(h1, h1b, h1_slabs, w_router[0], router_bias[0], expert_w1[0], expert_w3[0],
                         expert_w2[0], shared_w1[0], shared_w3[0], shared_w2[0], ln2_g[0], ln2_b[0])
    return out[None]
```

```python
import functools
import math

import jax
import jax.numpy as jnp
import numpy as np
from jax import lax
from jax.experimental import pallas as pl
from jax.experimental.pallas import tpu as pltpu

F32 = jnp.float32
BF16 = jnp.bfloat16

D_MODEL = 4096
N_META = 16
BLOCK = 128
N_PAD = BLOCK - N_META

N_HEADS = 32
N_KV_HEADS = 4
HEAD_DIM = 64
HEADS_PER_KV = N_HEADS // N_KV_HEADS
ATTN_WIDTH = N_HEADS * HEAD_DIM
KV_WIDTH = N_KV_HEADS * HEAD_DIM

SSM_WIDTH = 1024
SSM_GROUP = 16
N_SSM_GROUPS = SSM_WIDTH // SSM_GROUP
SSM_STATE = 64
SSM_CHUNK = 16
SSM_TILE_GROUPS = 128 // SSM_GROUP
SSM_TILES = N_SSM_GROUPS // SSM_TILE_GROUPS

N_EXPERTS = 64
TOP_K = 8
N_EXPERT_GROUPS = 8
EXPERTS_PER_GROUP = N_EXPERTS // N_EXPERT_GROUPS
TOPK_GROUPS = 4
EXPERT_DIM = 512
SHARED_DIM = 512
ROUTED_SCALE = 2.5
ROW_BLOCK = 256

IN_WIDTH = ATTN_WIDTH + 2 * KV_WIDTH + SSM_WIDTH + 2 * D_MODEL
COL_KV = ATTN_WIDTH
COL_U = ATTN_WIDTH + 2 * KV_WIDTH
COL_GA = COL_U + SSM_WIDTH
COL_GS = COL_GA + D_MODEL

DEEPNORM_ALPHA = 2.0 ** 0.25
LN_EPS = 1e-5
NEG_INF = -1e30

MIB = 1024 * 1024


def _params(semantics, vmem_mib):
    return pltpu.CompilerParams(dimension_semantics=semantics, vmem_limit_bytes=vmem_mib * MIB)


def _tile(n, pref):
    t = min(n, pref)
    while n % t:
        t -= 128
    return t


def _ln_rows(x, g, b):
    mu = jnp.mean(x, axis=-1, keepdims=True)
    xc = x - mu
    var = jnp.mean(xc * xc, axis=-1, keepdims=True)
    return xc * lax.rsqrt(var + LN_EPS) * g + b


LANES = 128
SLAB = D_MODEL // LANES
SLAB_PITCH = SLAB + 4


def _store_slabs(ref, y):
    m = y.shape[0]
    for s in range(SLAB):
        ref[pl.ds(s, m, stride=SLAB_PITCH), :] = y[:, s * LANES:(s + 1) * LANES]
    for s in range(SLAB, SLAB_PITCH):
        ref[pl.ds(s, m, stride=SLAB_PITCH), :] = jnp.zeros((m, LANES), F32)


def _ln_kernel(x_ref, g_ref, b_ref, o_ref, ob_ref, *maybe_slab_ref):
    y = _ln_rows(x_ref[...], g_ref[...], b_ref[...])
    o_ref[...] = y
    ob_ref[...] = y.astype(BF16)
    for os_ref in maybe_slab_ref:
        _store_slabs(os_ref, y)


def _layer_norm(x, g, b, slabs=False):
    m, d = x.shape
    tm = _tile(m, 256)
    out_specs = [pl.BlockSpec((tm, d), lambda i: (i, 0)), pl.BlockSpec((tm, d), lambda i: (i, 0))]
    out_shape = [jax.ShapeDtypeStruct((m, d), F32), jax.ShapeDtypeStruct((m, d), BF16)]
    if slabs:
        out_specs.append(pl.BlockSpec((tm * SLAB_PITCH, LANES), lambda i: (i, 0)))
        out_shape.append(jax.ShapeDtypeStruct((m * SLAB_PITCH, LANES), F32))
    return pl.pallas_call(
        _ln_kernel,
        grid=(m // tm,),
        in_specs=[pl.BlockSpec((tm, d), lambda i: (i, 0)),
                  pl.BlockSpec((1, d), lambda i: (0, 0)),
                  pl.BlockSpec((1, d), lambda i: (0, 0))],
        out_specs=out_specs,
        out_shape=out_shape,
        compiler_params=_params(("parallel",), 40),
        name="layer_norm",
    )(x, g.reshape(1, d), b.reshape(1, d))


def _proj_kernel(a_ref, w_ref, o_ref):
    o_ref[...] = jnp.dot(a_ref[...], w_ref[...].astype(BF16),
                         preferred_element_type=F32).astype(o_ref.dtype)


def _project(a, w, col0, n_cols, tm_pref=1024, tn=512):
    m, k = a.shape
    tm = _tile(m, tm_pref)
    off = col0 // tn
    return pl.pallas_call(
        _proj_kernel,
        grid=(m // tm, n_cols // tn),
        in_specs=[pl.BlockSpec((tm, k), lambda i, j: (i, 0)),
                  pl.BlockSpec((k, tn), lambda i, j: (0, j + off))],
        out_specs=pl.BlockSpec((tm, tn), lambda i, j: (i, j)),
        out_shape=jax.ShapeDtypeStruct((m, n_cols), BF16),
        compiler_params=_params(("parallel", "arbitrary"), 56),
        name="in_proj",
    )(a, w)


def _attn_kernel(sink_ref, q_ref, kvc_ref, kvp_ref, kvm_ref, o_ref):
    first = pl.program_id(0) == 0
    kv_all = jnp.concatenate([kvp_ref[...], kvc_ref[...], kvm_ref[N_PAD:BLOCK, :]], axis=0)
    n_keys = 2 * BLOCK + N_META
    rows = HEADS_PER_KV * BLOCK

    lane = lax.broadcasted_iota(jnp.int32, (BLOCK, BLOCK), 1)
    lo_q = lane < HEAD_DIM
    lane_kv = lax.broadcasted_iota(jnp.int32, (n_keys, BLOCK), 1)
    lo_kv = lane_kv < HEAD_DIM

    qi = lax.broadcasted_iota(jnp.int32, (rows, BLOCK), 0) & (BLOCK - 1)
    kj = lax.broadcasted_iota(jnp.int32, (rows, BLOCK), 1)
    from_prev = kj > qi
    hidden = from_prev & first
    local_head = lax.broadcasted_iota(jnp.int32, (rows, 1), 0) // BLOCK

    def dup_head(tile, odd):
        t = tile.astype(F32)
        r = pltpu.roll(t, HEAD_DIM, 1)
        keep = jnp.logical_not(lo_kv) if odd else lo_kv
        return jnp.where(keep, t, r).astype(BF16)

    for kh in range(N_KV_HEADS):
        t0 = (kh // 2) * BLOCK
        k2 = dup_head(kv_all[:, t0:t0 + BLOCK], kh % 2)
        v2 = dup_head(kv_all[:, KV_WIDTH + t0:KV_WIDTH + t0 + BLOCK], kh % 2)
        pieces = []
        for p in range(HEADS_PER_KV // 2):
            c0 = kh * HEADS_PER_KV * HEAD_DIM + p * BLOCK
            qp = q_ref[:, c0:c0 + BLOCK] * jnp.asarray(HEAD_DIM ** -0.5, BF16)
            zero = jnp.zeros_like(qp)
            pieces.append(jnp.where(lo_q, qp, zero))
            pieces.append(jnp.where(lo_q, zero, qp))
        qs = jnp.concatenate(pieces, axis=0)
        s = lax.dot_general(qs, k2, (((1,), (1,)), ((), ())), preferred_element_type=F32)
        band = jnp.where(from_prev, s[:, :BLOCK], s[:, BLOCK:2 * BLOCK])
        band = jnp.where(hidden, NEG_INF, band)
        meta = s[:, 2 * BLOCK:]
        sink = jnp.zeros((rows, 1), F32)
        for t in range(HEADS_PER_KV):
            sink = jnp.where(local_head == t, sink_ref[kh * HEADS_PER_KV + t], sink)
        m = jnp.maximum(jnp.maximum(jnp.max(band, axis=-1, keepdims=True),
                                    jnp.max(meta, axis=-1, keepdims=True)), sink)
        e_band = jnp.exp(band - m)
        e_meta = jnp.exp(meta - m)
        den = (jnp.sum(e_band, axis=-1, keepdims=True) + jnp.sum(e_meta, axis=-1, keepdims=True)
               + jnp.exp(sink - m))
        e = jnp.concatenate([jnp.where(from_prev, e_band, 0.0), jnp.where(from_prev, 0.0, e_band), e_meta],
                            axis=1).astype(BF16)
        o2 = jnp.dot(e, v2, preferred_element_type=F32) / den
        for p in range(HEADS_PER_KV // 2):
            c0 = kh * HEADS_PER_KV * HEAD_DIM + p * BLOCK
            r0 = 2 * p * BLOCK
            o_ref[:, c0:c0 + BLOCK] = jnp.where(
                lo_q, o2[r0:r0 + BLOCK], o2[r0 + BLOCK:r0 + 2 * BLOCK]).astype(BF16)


def _attention(z, z_lead, sinks):
    t = z.shape[0]
    kv_blk = COL_KV // (2 * KV_WIDTH)
    return pl.pallas_call(
        _attn_kernel,
        grid=(t // BLOCK,),
        in_specs=[pl.BlockSpec(memory_space=pltpu.SMEM),
                  pl.BlockSpec((BLOCK, ATTN_WIDTH), lambda n: (n, 0)),
                  pl.BlockSpec((BLOCK, 2 * KV_WIDTH), lambda n: (n, kv_blk)),
                  pl.BlockSpec((BLOCK, 2 * KV_WIDTH), lambda n: (jnp.maximum(n - 1, 0), kv_blk)),
                  pl.BlockSpec((BLOCK, 2 * KV_WIDTH), lambda n: (0, 0))],
        out_specs=pl.BlockSpec((BLOCK, ATTN_WIDTH), lambda n: (n, 0)),
        out_shape=jax.ShapeDtypeStruct((t, ATTN_WIDTH), BF16),
        compiler_params=_params(("parallel",), 40),
        name="swa_attention",
    )(sinks.astype(F32), z, z, z, z_lead)


def _cmul(ar, ai, br, bi):
    return ar * br - ai * bi, ar * bi + ai * br


def _ssm_tables(lam_re, lam_im, log_dt, b_re, b_im, c_re, c_im, d_skip, n_chunks):
    hi = lax.Precision.HIGHEST
    g, p, h, c = N_SSM_GROUPS, SSM_STATE, SSM_GROUP, SSM_CHUNK
    dt = jnp.exp(log_dt)[:, None]
    mag = jnp.exp(dt * lam_re)
    are = mag * jnp.cos(dt * lam_im)
    aim = mag * jnp.sin(dt * lam_im)
    den = lam_re * lam_re + lam_im * lam_im
    num_re = are - 1.0
    coef_re = (num_re * lam_re + aim * lam_im) / den
    coef_im = (aim * lam_re - num_re * lam_im) / den
    bb_re = coef_re[..., None] * b_re - coef_im[..., None] * b_im
    bb_im = coef_re[..., None] * b_im + coef_im[..., None] * b_re

    pr, pi = [jnp.ones_like(are)], [jnp.zeros_like(are)]
    for _ in range(c):
        r, i = _cmul(pr[-1], pi[-1], are, aim)
        pr.append(r)
        pi.append(i)
    pr, pi = jnp.stack(pr), jnp.stack(pi)

    m_re = pr[:c, :, :, None] * bb_re[None] - pi[:c, :, :, None] * bb_im[None]
    m_im = pr[:c, :, :, None] * bb_im[None] + pi[:c, :, :, None] * bb_re[None]
    nt, gt = SSM_TILES, SSM_TILE_GROUPS
    lag = (jnp.einsum('ghp,kgpj->kgjh', c_re, m_re, precision=hi)
           - jnp.einsum('ghp,kgpj->kgjh', c_im, m_im, precision=hi))
    lagc = lag.reshape(c, nt, gt * h, h).transpose(1, 0, 2, 3)
    rev = np.arange(c - 1, -1, -1)
    bt_re, bt_im = bb_re.transpose(0, 2, 1), bb_im.transpose(0, 2, 1)
    wo_re = pr[rev][:, :, None, :] * bt_re[None] - pi[rev][:, :, None, :] * bt_im[None]
    wo_im = pr[rev][:, :, None, :] * bt_im[None] + pi[rev][:, :, None, :] * bt_re[None]

    def lanes(first, second):
        w = jnp.concatenate([first, second], axis=-1)
        return w.reshape(c, nt, gt * h, 2 * p).transpose(1, 0, 2, 3)

    wout_a, wout_b = lanes(wo_re, wo_im), lanes(wo_im, wo_re)
    ct_re, ct_im = c_re.transpose(0, 2, 1), c_im.transpose(0, 2, 1)
    cin_re = ct_re[None] * pr[1:, :, :, None] - ct_im[None] * pi[1:, :, :, None]
    cin_im = -(ct_re[None] * pi[1:, :, :, None] + ct_im[None] * pr[1:, :, :, None])
    cinc = jnp.stack([cin_re.reshape(c, nt, gt * p, h), cin_im.reshape(c, nt, gt * p, h)], axis=2)
    cinc = cinc.transpose(1, 0, 2, 3, 4).reshape(nt, c, 2 * gt * p, h)
    spread = jnp.tile(jnp.eye(h, dtype=BF16), (1, gt))

    n_steps = max(1, math.ceil(math.log2(n_chunks)))
    qr, qi = [pr[c]], [pi[c]]
    for _ in range(n_steps - 1):
        r, i = _cmul(qr[-1], qi[-1], qr[-1], qi[-1])
        qr.append(r)
        qi.append(i)
    apr = jnp.stack(qr, axis=0).reshape(n_steps, nt, gt * p).transpose(1, 0, 2)
    api = jnp.stack(qi, axis=0).reshape(n_steps, nt, gt * p).transpose(1, 0, 2)
    dsk = d_skip.reshape(nt, 1, gt * h)
    return (lagc.astype(BF16), wout_a.astype(BF16), wout_b.astype(BF16), cinc.astype(BF16), spread,
            apr, api, dsk, n_steps)


def _chunk_steps(u_ref):
    n = u_ref.shape[0] // SSM_CHUNK
    return [u_ref[pl.ds(s, n, stride=SSM_CHUNK), :] for s in range(SSM_CHUNK)]


def _chunk_lanes(u_ref):
    return jnp.concatenate(_chunk_steps(u_ref), axis=1)


def _ssm_state_kernel(u_ref, lead_ref, wa_ref, wb_ref, apr_ref, api_ref, prev_ref, wout_s, *, n_steps):
    p, gh = SSM_STATE, SSM_TILE_GROUPS * SSM_GROUP
    row_g = lax.broadcasted_iota(jnp.int32, (gh, 2 * p), 0) // SSM_GROUP
    lo = lax.broadcasted_iota(jnp.int32, (gh, 2 * p), 1) < p
    zero = jnp.zeros((gh, 2 * p), BF16)
    for s in range(SSM_CHUNK):
        a, b = wa_ref[0, s], wb_ref[0, s]
        for j in range(SSM_TILE_GROUPS // 2):
            even, odd = (row_g == 2 * j) & lo, (row_g == 2 * j + 1) & jnp.logical_not(lo)
            wout_s[s * gh:(s + 1) * gh, j * 2 * p:(j + 1) * 2 * p] = (
                jnp.where(even, a, jnp.where(odd, b, zero)))
            wout_s[s * gh:(s + 1) * gh, (SSM_TILE_GROUPS // 2 + j) * 2 * p:(SSM_TILE_GROUPS // 2 + j + 1) * 2 * p] = (
                jnp.where(even, b, jnp.where(odd, a, zero)))
    wout = wout_s[...]
    st = jnp.dot(_chunk_lanes(u_ref).astype(BF16), wout, preferred_element_type=F32)
    lead = jnp.dot(_chunk_lanes(lead_ref).astype(BF16), wout, preferred_element_type=F32)
    nc, half = st.shape[0], st.shape[1] // 2
    xr, xi = st[:, :half], st[:, half:]
    mr, mi = lead[-1:, :half], lead[-1:, half:]
    row = lax.broadcasted_iota(jnp.int32, (nc, half), 0)
    ar, ai = apr_ref[0, 0:1, :], api_ref[0, 0:1, :]
    xr = xr + jnp.where(row == 0, ar * mr - ai * mi, 0.0)
    xi = xi + jnp.where(row == 0, ar * mi + ai * mr, 0.0)

    def shift(x, d):
        return jnp.where(row >= d, pltpu.roll(x, d, 0), 0.0)

    for j in range(n_steps):
        ar, ai = apr_ref[0, j:j + 1, :], api_ref[0, j:j + 1, :]
        sr, si = shift(xr, 1 << j), shift(xi, 1 << j)
        xr, xi = xr + (ar * sr - ai * si), xi + (ar * si + ai * sr)
    pr = jnp.where(row == 0, mr, shift(xr, 1))
    pi = jnp.where(row == 0, mi, shift(xi, 1))
    prev_ref[0] = jnp.concatenate([pr, pi], axis=1).astype(BF16)


def _ssm_out_kernel(u_ref, prev_ref, lag_ref, cin_ref, spread_ref, d_ref, y_ref, toep_s, wcin_s):
    gh, c = SSM_TILE_GROUPS * SSM_GROUP, SSM_CHUNK

    @pl.when(pl.program_id(1) == 0)
    def _():
        spread = spread_ref[...]
        diag = (lax.broadcasted_iota(jnp.int32, (gh, gh), 0) // SSM_GROUP
                == lax.broadcasted_iota(jnp.int32, (gh, gh), 1) // SSM_GROUP)
        zero = jnp.zeros((gh, gh), BF16)
        blocks = [jnp.where(diag, jnp.dot(lag_ref[0, k], spread, preferred_element_type=F32), 0.0).astype(BF16)
                  for k in range(c)]
        for s in range(c):
            for i in range(c):
                toep_s[s * gh:(s + 1) * gh, i * gh:(i + 1) * gh] = blocks[i - s] if i >= s else zero
        rows = wcin_s.shape[0]
        row_g = (lax.broadcasted_iota(jnp.int32, (rows, gh), 0) % (rows // 2)) // SSM_STATE
        own = row_g == lax.broadcasted_iota(jnp.int32, (rows, gh), 1) // SSM_GROUP
        for i in range(c):
            wcin_s[:, i * gh:(i + 1) * gh] = jnp.where(
                own, jnp.dot(cin_ref[0, i], spread, preferred_element_type=F32), 0.0).astype(BF16)

    us = _chunk_steps(u_ref)
    cb = us[0].shape[0]
    u = jnp.concatenate(us, axis=1).astype(BF16)
    y = (jnp.dot(u, toep_s[...], preferred_element_type=F32)
         + jnp.dot(prev_ref[0], wcin_s[...], preferred_element_type=F32))
    for s in range(SSM_CHUNK):
        ys = y[:, s * LANES:(s + 1) * LANES] + d_ref[0] * us[s]
        y_ref[pl.ds(s, cb, stride=SSM_CHUNK), :] = 0.5 * ys * (1.0 + lax.erf(ys * (0.5 ** 0.5)))


def _ssm(u, u_lead, tables):
    lagc, wout_a, wout_b, cinc, spread, apr, api, dsk, n_steps = tables
    t = u.shape[0]
    nc = t // SSM_CHUNK
    cb = _tile(nc, 256) if nc % 128 == 0 else nc
    lanes = LANES
    k = SSM_CHUNK * lanes
    states = 2 * SSM_TILE_GROUPS * SSM_STATE
    prev = pl.pallas_call(
        functools.partial(_ssm_state_kernel, n_steps=n_steps),
        grid=(SSM_TILES,),
        in_specs=[pl.BlockSpec((t, lanes), lambda g: (0, g)),
                  pl.BlockSpec((BLOCK, lanes), lambda g: (0, g)),
                  pl.BlockSpec((1, SSM_CHUNK, lanes, 2 * SSM_STATE), lambda g: (g, 0, 0, 0)),
                  pl.BlockSpec((1, SSM_CHUNK, lanes, 2 * SSM_STATE), lambda g: (g, 0, 0, 0)),
                  pl.BlockSpec((1, n_steps, states // 2), lambda g: (g, 0, 0)),
                  pl.BlockSpec((1, n_steps, states // 2), lambda g: (g, 0, 0))],
        out_specs=pl.BlockSpec((1, nc, states), lambda g: (g, 0, 0)),
        out_shape=jax.ShapeDtypeStruct((SSM_TILES, nc, states), BF16),
        scratch_shapes=[pltpu.VMEM((k, states), BF16)],
        compiler_params=_params(("parallel",), 48),
        name="s5_states",
    )(u, u_lead, wout_a, wout_b, apr, api)
    return pl.pallas_call(
        _ssm_out_kernel,
        grid=(SSM_TILES, nc // cb),
        in_specs=[pl.BlockSpec((cb * SSM_CHUNK, lanes), lambda g, c: (c, g)),
                  pl.BlockSpec((1, cb, states), lambda g, c: (g, c, 0)),
                  pl.BlockSpec((1, SSM_CHUNK, lanes, SSM_GROUP), lambda g, c: (g, 0, 0, 0)),
                  pl.BlockSpec((1, SSM_CHUNK, states, SSM_GROUP), lambda g, c: (g, 0, 0, 0)),
                  pl.BlockSpec((SSM_GROUP, lanes), lambda g, c: (0, 0)),
                  pl.BlockSpec((1, 1, lanes), lambda g, c: (g, 0, 0))],
        out_specs=pl.BlockSpec((cb * SSM_CHUNK, lanes), lambda g, c: (c, g)),
        out_shape=jax.ShapeDtypeStruct((t, SSM_WIDTH), F32),
        scratch_shapes=[pltpu.VMEM((k, k), BF16), pltpu.VMEM((states, k), BF16)],
        compiler_params=_params(("parallel", "arbitrary"), 48),
        name="s5_outputs",
    )(u, prev, lagc, cinc, spread, dsk)


def _glu_kernel(y_ref, w_ref, o_ref):
    y = y_ref[...]
    gate = jax.nn.sigmoid(jnp.dot(y.astype(BF16), w_ref[...].astype(BF16), preferred_element_type=F32))
    o_ref[...] = (y * gate).astype(BF16)


def _glu(y, w):
    m, d = y.shape
    tm = _tile(m, 1024)
    return pl.pallas_call(
        _glu_kernel,
        grid=(m // tm,),
        in_specs=[pl.BlockSpec((tm, d), lambda i: (i, 0)),
                  pl.BlockSpec((d, d), lambda i: (0, 0))],
        out_specs=pl.BlockSpec((tm, d), lambda i: (i, 0)),
        out_shape=jax.ShapeDtypeStruct((m, d), BF16),
        compiler_params=_params(("parallel",), 40),
        name="ssm_glu",
    )(y, w)


def _mix_kernel(a_ref, y_ref, wa_ref, ws_ref, ga_ref, gs_ref, o_ref):
    attn_d = jnp.dot(a_ref[...], wa_ref[...].astype(BF16), preferred_element_type=F32)
    ssm_d = jnp.dot(y_ref[...], ws_ref[...].astype(BF16), preferred_element_type=F32)
    mix = (jax.nn.sigmoid(ga_ref[...].astype(F32)) * attn_d
           + jax.nn.sigmoid(gs_ref[...].astype(F32)) * ssm_d)
    o_ref[...] = mix.astype(BF16)


def _mix(attn, y2, w_attn_o, w_ssm_o, z):
    m = attn.shape[0]
    tm, tn = _tile(m, 1024), 512
    ga_blk, gs_blk = COL_GA // tn, COL_GS // tn
    return pl.pallas_call(
        _mix_kernel,
        grid=(m // tm, D_MODEL // tn),
        in_specs=[pl.BlockSpec((tm, ATTN_WIDTH), lambda i, j: (i, 0)),
                  pl.BlockSpec((tm, SSM_WIDTH), lambda i, j: (i, 0)),
                  pl.BlockSpec((ATTN_WIDTH, tn), lambda i, j: (0, j)),
                  pl.BlockSpec((SSM_WIDTH, tn), lambda i, j: (0, j)),
                  pl.BlockSpec((tm, tn), lambda i, j: (i, j + ga_blk)),
                  pl.BlockSpec((tm, tn), lambda i, j: (i, j + gs_blk))],
        out_specs=pl.BlockSpec((tm, tn), lambda i, j: (i, j)),
        out_shape=jax.ShapeDtypeStruct((m, D_MODEL), BF16),
        compiler_params=_params(("parallel", "arbitrary"), 48),
        name="gated_merge",
    )(attn, y2, w_attn_o, w_ssm_o, z, z)


def _resid_kernel(a_ref, w_ref, h_ref, o_ref):
    o_ref[...] = DEEPNORM_ALPHA * h_ref[...] + jnp.dot(
        a_ref[...], w_ref[...].astype(BF16), preferred_element_type=F32)


def _out_proj(mix, w_out, h0):
    m = mix.shape[0]
    tm, tn = _tile(m, 1024), 512
    return pl.pallas_call(
        _resid_kernel,
        grid=(m // tm, D_MODEL // tn),
        in_specs=[pl.BlockSpec((tm, D_MODEL), lambda i, j: (i, 0)),
                  pl.BlockSpec((D_MODEL, tn), lambda i, j: (0, j)),
                  pl.BlockSpec((tm, tn), lambda i, j: (i, j))],
        out_specs=pl.BlockSpec((tm, tn), lambda i, j: (i, j)),
        out_shape=jax.ShapeDtypeStruct((m, D_MODEL), F32),
        compiler_params=_params(("parallel", "arbitrary"), 56),
        name="out_proj",
    )(mix, w_out, h0)


def _token_mixer(x, meta_tokens, ln0_g, ln0_b, w_in, attn_sinks, w_attn_o, ssm_tables_args,
                 w_glu, w_ssm_o, w_out, ln1_g, ln1_b):
    t = x.shape[0]
    lead = jnp.concatenate([jnp.zeros((N_PAD, D_MODEL), F32), meta_tokens.astype(F32)], axis=0)
    h0, h0b = _layer_norm(x, ln0_g, ln0_b)
    _, leadb = _layer_norm(lead, ln0_g, ln0_b)

    z = _project(h0b, w_in, 0, IN_WIDTH)
    z_lead = _project(leadb, w_in, COL_KV, 2 * KV_WIDTH + SSM_WIDTH)

    attn = _attention(z, z_lead, attn_sinks)

    lead_valid = (jnp.arange(BLOCK) >= N_PAD)[:, None]
    u_lead = jnp.where(lead_valid, z_lead[:, 2 * KV_WIDTH:].astype(F32), 0.0)
    tables = _ssm_tables(*ssm_tables_args, n_chunks=t // SSM_CHUNK)
    y = _ssm(z[:, COL_U:COL_GA].astype(F32), u_lead, tables)
    y2 = _glu(y, w_glu)

    mix = _mix(attn, y2, w_attn_o, w_ssm_o, z)
    r = _out_proj(mix, w_out, h0)
    return _layer_norm(r, ln1_g, ln1_b, slabs=True)


def _router_kernel(h_ref, wr_ref, bias_ref, idx_ref, wt_ref, pos_ref, cnt_ref):
    tm = h_ref.shape[0]

    @pl.when(pl.program_id(0) == 0)
    def _():
        cnt_ref[...] = jnp.zeros_like(cnt_ref)

    ng, ne = N_EXPERT_GROUPS, EXPERTS_PER_GROUP
    logits = lax.dot_general(wr_ref[...], h_ref[...], (((1,), (1,)), ((), ())),
                             precision=lax.Precision.HIGHEST, preferred_element_type=F32)
    scores = jax.nn.sigmoid(logits)
    sel = (scores + bias_ref[...]).reshape(ng, ne, tm)
    scores = scores.reshape(ng, ne, tm)
    e_in = lax.broadcasted_iota(jnp.int32, (ng, ne, tm), 1)
    e_id = lax.broadcasted_iota(jnp.int32, (ng, ne, tm), 0) * ne + e_in
    neg = -jnp.inf

    top1 = jnp.max(sel, axis=1, keepdims=True)
    first = jnp.min(jnp.where(sel == top1, e_in, ne), axis=1, keepdims=True)
    top2 = jnp.max(jnp.where(e_in == first, neg, sel), axis=1, keepdims=True)
    grp = (top1 + top2)[:, 0, :]

    g_id = lax.broadcasted_iota(jnp.int32, (ng, tm), 0)
    chosen = jnp.zeros((ng, tm), jnp.bool_)
    for _ in range(TOPK_GROUPS):
        best = jnp.max(grp, axis=0, keepdims=True)
        pick = g_id == jnp.min(jnp.where(grp == best, g_id, ng), axis=0, keepdims=True)
        chosen = chosen | pick
        grp = jnp.where(pick, neg, grp)

    cur = jnp.where(chosen[:, None, :], sel, NEG_INF)
    def pick_sum(pick, values):
        part = jnp.sum(jnp.where(pick, values, 0.0), axis=1, keepdims=True)
        return jnp.sum(part, axis=0, keepdims=True)[0]

    ids, wts, picks = [], [], []
    for _ in range(TOP_K):
        best = jnp.max(jnp.max(cur, axis=1, keepdims=True), axis=0, keepdims=True)
        cand = jnp.where(cur == best, e_id, N_EXPERTS)
        win = jnp.min(jnp.min(cand, axis=1, keepdims=True), axis=0, keepdims=True)
        pick = e_id == win
        ids.append(win[0])
        wts.append(pick_sum(pick, scores))
        picks.append(pick)
        cur = jnp.where(pick, neg, cur)
    w = jnp.concatenate(wts, axis=0)
    idx_ref[...] = jnp.concatenate(ids, axis=0)
    wt_ref[...] = w / jnp.sum(w, axis=0, keepdims=True) * ROUTED_SCALE

    taken = jnp.where(cur == neg, 1.0, 0.0).reshape(N_EXPERTS, tm)
    before = (lax.broadcasted_iota(jnp.int32, (tm, tm), 0)
              < lax.broadcasted_iota(jnp.int32, (tm, tm), 1))
    rank = jnp.dot(taken.astype(BF16), jnp.where(before, 1.0, 0.0).astype(BF16),
                   preferred_element_type=F32)
    pos = (cnt_ref[:, 0:1] + rank).reshape(ng, ne, tm)
    pos_ref[...] = jnp.concatenate([pick_sum(p, pos) for p in picks], axis=0).astype(jnp.int32)
    cnt_ref[...] = cnt_ref[...] + jnp.sum(taken, axis=1, keepdims=True)


def _router(h1, w_router, router_bias):
    t, d = h1.shape
    tm = _tile(t, 512)
    return pl.pallas_call(
        _router_kernel,
        grid=(t // tm,),
        in_specs=[pl.BlockSpec((tm, d), lambda i: (i, 0)),
                  pl.BlockSpec((N_EXPERTS, d), lambda i: (0, 0)),
                  pl.BlockSpec((N_EXPERTS, 1), lambda i: (0, 0))],
        out_specs=[pl.BlockSpec((TOP_K, tm), lambda i: (0, i)),
                   pl.BlockSpec((TOP_K, tm), lambda i: (0, i)),
                   pl.BlockSpec((TOP_K, tm), lambda i: (0, i)),
                   pl.BlockSpec((N_EXPERTS, LANES), lambda i: (0, 0))],
        out_shape=[jax.ShapeDtypeStruct((TOP_K, t), jnp.int32),
                   jax.ShapeDtypeStruct((TOP_K, t), F32),
                   jax.ShapeDtypeStruct((TOP_K, t), jnp.int32),
                   jax.ShapeDtypeStruct((N_EXPERTS, LANES), F32)],
        compiler_params=_params(("arbitrary",), 40),
        name="router",
    )(h1, w_router.T, router_bias.reshape(N_EXPERTS, 1))


def _dispatch(idx_t, pos_t, counts):
    t = idx_t.shape[1]
    n_assign = TOP_K * t
    n_blocks = -(-n_assign // ROW_BLOCK) + N_EXPERTS
    n_rows = n_blocks * ROW_BLOCK
    counts = counts[:, 0].astype(jnp.int32)
    padded = (counts + ROW_BLOCK - 1) // ROW_BLOCK * ROW_BLOCK
    ends = jnp.cumsum(padded)
    starts = ends - padded
    own = idx_t[:, :, None] == jnp.arange(N_EXPERTS, dtype=jnp.int32)[None, None, :]
    dest = (jnp.sum(jnp.where(own, starts[None, None, :], 0), axis=-1) + pos_t).reshape(-1)
    tok_flat = jnp.tile(jnp.arange(t, dtype=jnp.int32), TOP_K)
    row_tok = jnp.zeros((n_rows,), jnp.int32).at[dest].set(
        tok_flat, unique_indices=True, mode='promise_in_bounds')
    block_start = jnp.arange(n_blocks, dtype=jnp.int32) * ROW_BLOCK
    block_expert = jnp.minimum(
        jnp.sum((ends[None, :] <= block_start[:, None]).astype(jnp.int32), axis=1), N_EXPERTS - 1)
    n_used = (ends[-1] // ROW_BLOCK).astype(jnp.int32).reshape(1)
    has = counts > 0
    cand = jnp.where(has[None, :] & (jnp.arange(N_EXPERTS)[None, :] > jnp.arange(N_EXPERTS)[:, None]),
                     jnp.arange(N_EXPERTS)[None, :], N_EXPERTS)
    nxt = jnp.min(cand, axis=1)
    next_expert = jnp.where(nxt < N_EXPERTS, nxt, -1).astype(jnp.int32)
    return (row_tok.reshape(n_blocks, 1, ROW_BLOCK), block_expert, n_used, next_expert,
            dest.reshape(TOP_K, t))


WEIGHT_DMA_PRIORITY = 1
GATHER_SLOTS = 4


def _gather_slabs(src_hbm, idx_smem, dst, sem, n, unrolled=False, dst_pitch=None, both_priorities=False):
    dst_pitch = SLAB_PITCH if dst_pitch is None else dst_pitch

    def body(i, carry, priority=0):
        src = idx_smem[0, i] * SLAB_PITCH
        pltpu.make_async_copy(src_hbm.at[pl.ds(src, SLAB)], dst.at[pl.ds(i * dst_pitch, SLAB)],
                              sem).start(priority=priority)
        return carry
    if unrolled:
        for i in range(n):
            body(i, 0, priority=i % 2 if both_priorities else 0)
    else:
        lax.fori_loop(0, n, body, 0, unroll=8)


def _wait_slabs(src_hbm, dst, sem, n):
    pltpu.make_async_copy(src_hbm.at[pl.ds(0, n * SLAB)], dst.at[pl.ds(0, n * SLAB)], sem).wait()


def _slab_rows(ref, n, pitch=None):
    pitch = SLAB_PITCH if pitch is None else pitch
    return jnp.concatenate([ref[pl.ds(s, n, stride=pitch), :] for s in range(SLAB)], axis=1)


def _expert_up_kernel(be_ref, nu_ref, nx_ref, tok_hbm, h_hbm, w1_hbm, w3_hbm, o_ref,
                      idx_smem, xbuf, wstage, wcast, isem, xsem, wsem):
    b = pl.program_id(0)
    n_used = nu_ref[0]
    last = n_used - 1
    e = be_ref[b]
    ns = GATHER_SLOTS
    slot = b % ns
    ahead = (b + ns - 1) % ns

    def idx_copy(blk, s):
        return pltpu.make_async_copy(tok_hbm.at[jnp.minimum(blk, last)], idx_smem.at[s], isem.at[s])

    def weight_copies(ex):
        return (pltpu.make_async_copy(w1_hbm.at[ex], wstage.at[0], wsem.at[0]),
                pltpu.make_async_copy(w3_hbm.at[ex], wstage.at[1], wsem.at[1]))

    @pl.when(b == 0)
    def _():
        for c in weight_copies(e):
            c.start(priority=WEIGHT_DMA_PRIORITY)
        for k in range(ns - 1):
            idx_copy(k, k).start()
            idx_copy(k, k).wait()
            _gather_slabs(h_hbm, idx_smem.at[k], xbuf.at[k], xsem.at[k], ROW_BLOCK)
        idx_copy(ns - 1, ns - 1).start()

    @pl.when(b < n_used)
    def _():
        first_of_expert = jnp.logical_or(b == 0, be_ref[jnp.maximum(b - 1, 0)] != e)

        @pl.when(first_of_expert)
        def _():
            for c in weight_copies(e):
                c.wait()
            rows = wstage.shape[1] // 8
            for m in range(2):
                for r in range(8):
                    wcast[m, r * rows:(r + 1) * rows] = wstage[m, r * rows:(r + 1) * rows].astype(BF16)
            nxt = nx_ref[e]

            @pl.when(nxt >= 0)
            def _():
                for c in weight_copies(nxt):
                    c.start(priority=WEIGHT_DMA_PRIORITY)

        idx_copy(b + ns - 1, ahead).wait()
        _wait_slabs(h_hbm, xbuf.at[slot], xsem.at[slot], ROW_BLOCK)
        _gather_slabs(h_hbm, idx_smem.at[ahead], xbuf.at[ahead], xsem.at[ahead], ROW_BLOCK, unrolled=True)
        idx_copy(b + ns, slot).start()
        x = _slab_rows(xbuf.at[slot], ROW_BLOCK).astype(BF16)
        a = jnp.dot(x, wcast[0], preferred_element_type=F32)
        g = jnp.dot(x, wcast[1], preferred_element_type=F32)
        o_ref[...] = (a * jax.nn.sigmoid(a) * g).astype(BF16)

        @pl.when(b == last)
        def _():
            for k in range(1, ns):
                s = (b + k) % ns
                _wait_slabs(h_hbm, xbuf.at[s], xsem.at[s], ROW_BLOCK)
            idx_copy(last, slot).wait()

    @pl.when(b >= n_used)
    def _():
        o_ref[...] = jnp.zeros_like(o_ref)


def _expert_up(h1, row_tok, block_expert, n_used, next_expert, w1, w3):
    n_blocks = row_tok.shape[0]
    d = w1.shape[1]
    grid_spec = pltpu.PrefetchScalarGridSpec(
        num_scalar_prefetch=3,
        grid=(n_blocks,),
        in_specs=[pl.BlockSpec(memory_space=pl.ANY)] * 4,
        out_specs=pl.BlockSpec((ROW_BLOCK, EXPERT_DIM), lambda b, *_: (b, 0)),
        scratch_shapes=[pltpu.SMEM((GATHER_SLOTS, 1, ROW_BLOCK), jnp.int32),
                        pltpu.VMEM((GATHER_SLOTS, ROW_BLOCK * SLAB_PITCH, LANES), F32),
                        pltpu.VMEM((2, d, EXPERT_DIM), F32),
                        pltpu.VMEM((2, d, EXPERT_DIM), BF16),
                        pltpu.SemaphoreType.DMA((GATHER_SLOTS,)),
                        pltpu.SemaphoreType.DMA((GATHER_SLOTS,)),
                        pltpu.SemaphoreType.DMA((2,))])
    return pl.pallas_call(
        _expert_up_kernel,
        grid_spec=grid_spec,
        out_shape=jax.ShapeDtypeStruct((n_blocks * ROW_BLOCK, EXPERT_DIM), BF16),
        compiler_params=_params(("arbitrary",), 58),
        name="expert_up",
    )(block_expert, n_used, next_expert, row_tok, h1, w1, w3)


def _expert_down_kernel(be_ref, nu_ref, nx_ref, a_ref, w2_hbm, o_ref, wstage, wcast, wsem):
    b = pl.program_id(0)
    n_used = nu_ref[0]
    e = be_ref[b]

    def weight_copy(ex):
        return pltpu.make_async_copy(w2_hbm.at[ex], wstage, wsem)

    @pl.when(b == 0)
    def _():
        weight_copy(e).start(priority=WEIGHT_DMA_PRIORITY)

    @pl.when(b < n_used)
    def _():
        first_of_expert = jnp.logical_or(b == 0, be_ref[jnp.maximum(b - 1, 0)] != e)

        @pl.when(first_of_expert)
        def _():
            weight_copy(e).wait()
            wcast[...] = wstage[...].astype(BF16)
            nxt = nx_ref[e]

            @pl.when(nxt >= 0)
            def _():
                weight_copy(nxt).start(priority=WEIGHT_DMA_PRIORITY)

        _store_slabs(o_ref, jnp.dot(a_ref[...], wcast[...], preferred_element_type=F32))

    @pl.when(b >= n_used)
    def _():
        o_ref[...] = jnp.zeros_like(o_ref)


def _expert_down(act, block_expert, n_used, next_expert, w2):
    n_rows = act.shape[0]
    n_blocks = n_rows // ROW_BLOCK
    d = w2.shape[2]
    grid_spec = pltpu.PrefetchScalarGridSpec(
        num_scalar_prefetch=3,
        grid=(n_blocks,),
        in_specs=[pl.BlockSpec((ROW_BLOCK, EXPERT_DIM), lambda b, *_: (b, 0)),
                  pl.BlockSpec(memory_space=pl.ANY)],
        out_specs=pl.BlockSpec((ROW_BLOCK * SLAB_PITCH, LANES), lambda b, *_: (b, 0)),
        scratch_shapes=[pltpu.VMEM((EXPERT_DIM, d), F32),
                        pltpu.VMEM((EXPERT_DIM, d), BF16),
                        pltpu.SemaphoreType.DMA(())])
    return pl.pallas_call(
        _expert_down_kernel,
        grid_spec=grid_spec,
        out_shape=jax.ShapeDtypeStruct((n_rows * SLAB_PITCH, LANES), F32),
        compiler_params=_params(("arbitrary",), 40),
        name="expert_down",
    )(block_expert, n_used, next_expert, act, w2)


def _shared_kernel(h_ref, w1_ref, w3_ref, w2_ref, o_ref):
    x = h_ref[...]
    a = jnp.dot(x, w1_ref[...], preferred_element_type=F32)
    g = jnp.dot(x, w3_ref[...], preferred_element_type=F32)
    act = (a * jax.nn.sigmoid(a) * g).astype(BF16)
    o_ref[...] = jnp.dot(act, w2_ref[...], preferred_element_type=F32)


def _shared_expert(h1b, w1, w3, w2):
    t, d = h1b.shape
    tm = _tile(t, 512)
    return pl.pallas_call(
        _shared_kernel,
        grid=(t // tm,),
        in_specs=[pl.BlockSpec((tm, d), lambda i: (i, 0)),
                  pl.BlockSpec((d, SHARED_DIM), lambda i: (0, 0)),
                  pl.BlockSpec((d, SHARED_DIM), lambda i: (0, 0)),
                  pl.BlockSpec((SHARED_DIM, d), lambda i: (0, 0))],
        out_specs=pl.BlockSpec((tm, d), lambda i: (i, 0)),
        out_shape=jax.ShapeDtypeStruct((t, d), F32),
        compiler_params=_params(("parallel",), 56),
        name="shared_expert",
    )(h1b, w1.astype(BF16), w3.astype(BF16), w2.astype(BF16))


COMBINE_TOKENS = 64
COMBINE_PITCH = SLAB + 8


def _combine_kernel(dest_hbm, y_hbm, h_ref, s_ref, w_ref, g_ref, b_ref, o_ref,
                    idx_smem, ybuf, stage, isem, ysem):
    i = pl.program_id(0)
    n = pl.num_programs(0)
    slot = i % 2
    rows = TOP_K * COMBINE_TOKENS

    def idx_copy(blk, s):
        return pltpu.make_async_copy(dest_hbm.at[blk], idx_smem.at[s], isem.at[s])

    last = n - 1

    pitch = COMBINE_PITCH
    per_j = COMBINE_TOKENS * pitch

    @pl.when(i == 0)
    def _():
        ybuf[...] = jnp.zeros_like(ybuf)
        idx_copy(0, 0).start()
        idx_copy(0, 0).wait()
        _gather_slabs(y_hbm, idx_smem.at[0], ybuf.at[0], ysem.at[0], rows, dst_pitch=pitch)
        idx_copy(jnp.minimum(1, last), 1).start()

    idx_copy(jnp.minimum(i + 1, last), 1 - slot).wait()
    _wait_slabs(y_hbm, ybuf.at[slot], ysem.at[slot], rows)
    _gather_slabs(y_hbm, idx_smem.at[1 - slot], ybuf.at[1 - slot], ysem.at[1 - slot], rows,
                  unrolled=True, dst_pitch=pitch, both_priorities=True)
    idx_copy(jnp.minimum(i + 2, last), slot).start()
    acc = None
    for j in range(TOP_K):
        wj = jnp.broadcast_to(w_ref[:, j:j + 1, :], (COMBINE_TOKENS, pitch, LANES)).reshape(per_j, LANES)
        term = wj * ybuf[slot, j * per_j:(j + 1) * per_j, :]
        acc = term if acc is None else acc + term
    stage[...] = acc
    routed = _slab_rows(stage, COMBINE_TOKENS, pitch)
    o_ref[...] = _ln_rows(DEEPNORM_ALPHA * h_ref[...] + s_ref[...] + routed, g_ref[...], b_ref[...])

    @pl.when(i == last)
    def _():
        _wait_slabs(y_hbm, ybuf.at[1 - slot], ysem.at[1 - slot], rows)
        idx_copy(last, slot).wait()


def _combine(yb, dest, w_t, h1, shared, g, b):
    t, d = h1.shape
    tm = COMBINE_TOKENS
    n_tiles = t // tm
    dest_tiles = dest.reshape(TOP_K, n_tiles, tm).transpose(1, 0, 2).reshape(n_tiles, 1, TOP_K * tm)
    w_lanes = jnp.broadcast_to(w_t.T[:, :, None], (t, TOP_K, LANES))
    return pl.pallas_call(
        _combine_kernel,
        grid=(n_tiles,),
        in_specs=[pl.BlockSpec(memory_space=pl.ANY),
                  pl.BlockSpec(memory_space=pl.ANY),
                  pl.BlockSpec((tm, d), lambda i: (i, 0)),
                  pl.BlockSpec((tm, d), lambda i: (i, 0)),
                  pl.BlockSpec((tm, TOP_K, LANES), lambda i: (i, 0, 0)),
                  pl.BlockSpec((1, d), lambda i: (0, 0)),
                  pl.BlockSpec((1, d), lambda i: (0, 0))],
        out_specs=pl.BlockSpec((tm, d), lambda i: (i, 0)),
        out_shape=jax.ShapeDtypeStruct((t, d), F32),
        scratch_shapes=[pltpu.SMEM((2, 1, TOP_K * tm), jnp.int32),
                        pltpu.VMEM((2, TOP_K * tm * COMBINE_PITCH, LANES), F32),
                        pltpu.VMEM((tm * COMBINE_PITCH, LANES), F32),
                        pltpu.SemaphoreType.DMA((2,)),
                        pltpu.SemaphoreType.DMA((2,))],
        compiler_params=_params(("arbitrary",), 48),
        name="combine_norm",
    )(dest_tiles, yb, h1, shared, w_lanes, g.reshape(1, d), b.reshape(1, d))


def _channel_mixer(h1, h1b, h1_slabs, w_router, router_bias, expert_w1, expert_w3, expert_w2,
                   shared_w1, shared_w3, shared_w2, ln2_g, ln2_b):
    idx_t, w_t, pos_t, counts = _router(h1, w_router, router_bias)
    row_tok, block_expert, n_used, next_expert, dest = _dispatch(idx_t, pos_t, counts)
    shared = _shared_expert(h1b, shared_w1, shared_w3, shared_w2)
    act = _expert_up(h1_slabs, row_tok, block_expert, n_used, next_expert, expert_w1, expert_w3)
    yb = _expert_down(act, block_expert, n_used, next_expert, expert_w2)
    return _combine(yb, dest, w_t, h1, shared, ln2_g, ln2_b)


def kernel(x, meta_tokens, ln0_g, ln0_b, w_in, attn_sinks, w_attn_o, ssm_lam_re, ssm_lam_im, ssm_log_dt,
           ssm_b_re, ssm_b_im, ssm_c_re, ssm_c_im, ssm_d, w_glu, w_ssm_o, w_out, ln1_g, ln1_b, w_router,
           router_bias, expert_w1, expert_w3, expert_w2, shared_w1, shared_w3, shared_w2, ln2_g, ln2_b):
    assert x.shape[0] == 1 and w_in.shape[0] == 1
    ssm_args = (ssm_lam_re[0], ssm_lam_im[0], ssm_log_dt[0], ssm_b_re[0], ssm_b_im[0],
                ssm_c_re[0], ssm_c_im[0], ssm_d[0])
    h1, h1b, h1_slabs = _token_mixer(x[0], meta_tokens, ln0_g, ln0_b, w_in[0], attn_sinks[0],
                                     w_attn_o[0], ssm_args, w_glu[0], w_ssm_o[0], w_out[0], ln1_g[0], ln1_b[0])
    out = _channel_mixer(h1, h1b, h1_slabs, w_router[0], router_bias[0], expert_w1[0], expert_w3[0],
                         expert_w2[0], shared_w1[0], shared_w3[0], shared_w2[0], ln2_g[0], ln2_b[0])
    return out[None]
```
